```python
import jax, jax.numpy as jnp
from jax import lax
import numpy as np

D_MODEL = 2048
BATCH = 1
SEQ = 8192
DEPTH = 2
DEC_BATCH = 128
DEC_SEQ = 4
PAST_LEN = 8192
PAGE_SIZE = 128

MIX_W = D_MODEL
HEAD_DIM = 64
ATT_W = MIX_W // 2
N_HEADS = ATT_W // HEAD_DIM
N_KV = 4
KV_W = N_KV * HEAD_DIM
WINDOW = 128
BLOCK = WINDOW
CONV_C = MIX_W - ATT_W
CONV_W = 31
D_IN = ATT_W + 2 * KV_W + 2 * CONV_C
N_MEM = 256
X_HEADS = 4
X_HEAD_DIM = 128
X_W = X_HEADS * X_HEAD_DIM
D_FF = 5632
N_EXPERTS = 8
TOP_K = 2
D_FF_E = 7168
N_DENSE = (DEPTH + 1) // 2
N_MOE = DEPTH // 2
EPS = 1e-6
NEG = -1e30

kernel_name = 'hybrid_swa_conformer_decode_step'


def rms_norm(x, g):
    xf = x.astype(jnp.float32)
    y = xf * lax.rsqrt(jnp.mean(xf * xf, axis=-1, keepdims=True) + EPS)
    return (y * g.astype(jnp.float32)).astype(x.dtype)


def layer_norm(x, g, b):
    xf = x.astype(jnp.float32)
    mu = jnp.mean(xf, axis=-1, keepdims=True)
    xc = xf - mu
    y = xc * lax.rsqrt(jnp.mean(xc * xc, axis=-1, keepdims=True) + EPS)
    return (y * g.astype(jnp.float32) + b.astype(jnp.float32)).astype(x.dtype)


def alibi_slopes():
    return 2.0 ** (-8.0 * jnp.arange(1, N_HEADS + 1, dtype=jnp.float32) / N_HEADS)


def split_in(u):
    o1 = ATT_W
    o2 = o1 + KV_W
    o3 = o2 + KV_W
    o4 = o3 + CONV_C
    return u[..., :o1], u[..., o1:o2], u[..., o2:o3], u[..., o3:o4], u[..., o4:]


def sink_attention(q, k, v, dist, valid, sinks):
    g = N_HEADS // N_KV
    qg = q.reshape(q.shape[:-2] + (N_KV, g, HEAD_DIM))
    s = jnp.einsum('...qngd,...snd->...ngqs', qg, k).astype(jnp.float32) * (HEAD_DIM ** -0.5)
    slopes = alibi_slopes().reshape(N_KV, g, 1, 1)
    s = s - slopes * dist[..., None, None, :, :].astype(jnp.float32)
    s = jnp.where(valid[..., None, None, :, :], s, NEG)
    sk = sinks.astype(jnp.float32).reshape(N_KV, g, 1, 1)
    m = jnp.maximum(jnp.max(s, axis=-1, keepdims=True), sk)
    p = jnp.exp(s - m)
    p = p / (jnp.sum(p, axis=-1, keepdims=True) + jnp.exp(sk - m))
    o = jnp.einsum('...ngqs,...snd->...qngd', p.astype(v.dtype), v)
    return o.reshape(o.shape[:-3] + (ATT_W,))


def conv_branch(ext, conv_w, conv_b, ln_g, ln_b):
    y = lax.conv_general_dilated(ext, conv_w[:, None, :].astype(ext.dtype), (1,), 'VALID',
                                 dimension_numbers=('NWC', 'WIO', 'NWC'),
                                 feature_group_count=CONV_C)
    return jax.nn.silu(layer_norm(y + conv_b, ln_g, ln_b))


def merge_out(att, conv_o, g_att, g_conv, w_out):
    h = jnp.concatenate([rms_norm(att, g_att), rms_norm(conv_o, g_conv)], axis=-1)
    return h @ w_out


def mixer_prompt(xn, w_in, sinks, g_att, conv_w, conv_b, ln_g, ln_b, g_conv, w_out):
    bsz, seq, _ = xn.shape
    q, k, v, a, g = split_in(xn @ w_in)
    nb = seq // BLOCK
    qb = q.reshape(bsz, nb, BLOCK, N_HEADS, HEAD_DIM)
    kb = k.reshape(bsz, nb, BLOCK, N_KV, HEAD_DIM)
    vb = v.reshape(bsz, nb, BLOCK, N_KV, HEAD_DIM)
    prev = lambda z: jnp.concatenate([jnp.zeros_like(z[:, :1]), z[:, :-1]], axis=1)
    kc = jnp.concatenate([prev(kb), kb], axis=2)
    vc = jnp.concatenate([prev(vb), vb], axis=2)
    i = jnp.arange(BLOCK)[:, None]
    j = jnp.arange(2 * BLOCK)[None, :]
    dist = BLOCK + i - j
    key_pos = (jnp.arange(nb)[:, None, None] - 1) * BLOCK + j[None]
    valid = (dist >= 0) & (dist <= WINDOW) & (key_pos >= 0)
    att = sink_attention(qb, kc, vc, dist, valid, sinks).reshape(bsz, seq, ATT_W)
    glu = a * jax.nn.sigmoid(g)
    ext = jnp.concatenate([jnp.zeros((bsz, CONV_W - 1, CONV_C), glu.dtype), glu], axis=1)
    conv_o = conv_branch(ext, conv_w, conv_b, ln_g, ln_b)
    y = merge_out(att, conv_o, g_att, g_conv, w_out)
    keep = min(WINDOW, seq)
    k_rows = k.reshape(bsz, seq, N_KV, HEAD_DIM)[:, seq - keep:]
    v_rows = v.reshape(bsz, seq, N_KV, HEAD_DIM)[:, seq - keep:]
    return y, k_rows, v_rows, ext[:, seq:]


def mixer_sample(xn, win_k, win_v, conv_state, w_in, sinks, g_att, conv_w, conv_b, ln_g, ln_b,
                 g_conv, w_out):
    bsz, t, _ = xn.shape
    q, k, v, a, g = split_in(xn @ w_in)
    wb = win_k.shape[1]
    k_all = jnp.concatenate([win_k, k.reshape(bsz, t, N_KV, HEAD_DIM)], axis=1)
    v_all = jnp.concatenate([win_v, v.reshape(bsz, t, N_KV, HEAD_DIM)], axis=1)
    q_pos = PAST_LEN + jnp.arange(t)
    k_pos = jnp.concatenate([PAST_LEN - wb + jnp.arange(wb), PAST_LEN + jnp.arange(t)])
    dist = q_pos[:, None] - k_pos[None, :]
    valid = (dist >= 0) & (dist <= WINDOW)
    att = sink_attention(q.reshape(bsz, t, N_HEADS, HEAD_DIM), k_all, v_all, dist, valid, sinks)
    glu = a * jax.nn.sigmoid(g)
    ext = jnp.concatenate([conv_state, glu], axis=1)
    conv_o = conv_branch(ext, conv_w, conv_b, ln_g, ln_b)
    y = merge_out(att, conv_o, g_att, g_conv, w_out)
    return y, k_all[:, t:], v_all[:, t:], ext[:, t:]


def memory_kv(mem, g, w_xk, w_xv):
    mn = rms_norm(mem, g)
    shp = mem.shape[:2] + (X_HEADS, X_HEAD_DIM)
    return (mn @ w_xk).reshape(shp), (mn @ w_xv).reshape(shp)


def cross_attend(xn, mem_k, mem_v, w_xq, w_xo):
    bsz, t, _ = xn.shape
    q = (xn @ w_xq).reshape(bsz, t, X_HEADS, X_HEAD_DIM)
    s = jnp.einsum('bthd,bmhd->bhtm', q, mem_k).astype(jnp.float32) * (X_HEAD_DIM ** -0.5)
    p = jax.nn.softmax(s, axis=-1).astype(mem_v.dtype)
    o = jnp.einsum('bhtm,bmhd->bthd', p, mem_v).reshape(bsz, t, X_W)
    return o @ w_xo


def swiglu(x, wg, wu, wd):
    return (jax.nn.silu(x @ wg) * (x @ wu)) @ wd


def moe_swiglu(x, w_router, wg, wu, wd):
    logits = (x @ w_router).astype(jnp.float32)
    top_v, top_i = lax.top_k(logits, TOP_K)
    gates = jax.nn.softmax(top_v, axis=-1)
    combine = jnp.sum(jax.nn.one_hot(top_i, N_EXPERTS, dtype=jnp.float32) * gates[..., None], axis=-2)
    combine = combine.astype(x.dtype)
    out = jnp.zeros_like(x)
    for e in range(N_EXPERTS):
        out = out + combine[..., e:e + 1] * swiglu(x, wg[e], wu[e], wd[e])
    return out


def setup_inputs(seed: int = 0) -> dict:
    key = jax.random.key(seed)
    ks = iter(jax.random.split(key, 40))
    f32 = jnp.float32
    nrm = lambda shape, scale: jax.random.normal(next(ks), shape, f32) * scale
    gain = lambda shape: 1.0 + 0.02 * jax.random.normal(next(ks), shape, f32)
    win_buf = min(WINDOW, PAST_LEN)
    return {
        'x_prompt': nrm((BATCH, SEQ, D_MODEL), 1.0),
        'x_sample': nrm((DEC_BATCH, DEC_SEQ, D_MODEL), 1.0),
        'cache_win_k': nrm((DEPTH, DEC_BATCH, win_buf, N_KV, HEAD_DIM), 1.0),
        'cache_win_v': nrm((DEPTH, DEC_BATCH, win_buf, N_KV, HEAD_DIM), 1.0),
        'state_conv': nrm((DEPTH, DEC_BATCH, CONV_W - 1, CONV_C), 0.5),
        'cache_mem_k': nrm((DEPTH, DEC_BATCH, N_MEM, X_HEADS, X_HEAD_DIM), 1.0),
        'cache_mem_v': nrm((DEPTH, DEC_BATCH, N_MEM, X_HEADS, X_HEAD_DIM), 1.0),
        'mem_prompt': nrm((BATCH, N_MEM, D_MODEL), 1.0),
        'norm_mix': gain((DEPTH, D_MODEL)),
        'w_in': nrm((DEPTH, D_MODEL, D_IN), D_MODEL ** -0.5),
        'attn_sinks': nrm((DEPTH, N_HEADS), 0.5),
        'att_out_norm': gain((DEPTH, ATT_W)),
        'conv_w': nrm((DEPTH, CONV_W, CONV_C), CONV_W ** -0.5),
        'conv_b': nrm((DEPTH, CONV_C), 0.02),
        'conv_ln_g': gain((DEPTH, CONV_C)),
        'conv_ln_b': nrm((DEPTH, CONV_C), 0.02),
        'conv_out_norm': gain((DEPTH, CONV_C)),
        'w_out': nrm((DEPTH, MIX_W, D_MODEL), MIX_W ** -0.5),
        'norm_cross': gain((DEPTH, D_MODEL)),
        'norm_mem': gain((DEPTH, D_MODEL)),
        'w_xq': nrm((DEPTH, D_MODEL, X_W), D_MODEL ** -0.5),
        'w_xk': nrm((DEPTH, D_MODEL, X_W), D_MODEL ** -0.5),
        'w_xv': nrm((DEPTH, D_MODEL, X_W), D_MODEL ** -0.5),
        'w_xo': nrm((DEPTH, X_W, D_MODEL), X_W ** -0.5),
        'norm_ffn': gain((DEPTH, D_MODEL)),
        'w_gate': nrm((N_DENSE, D_MODEL, D_FF), D_MODEL ** -0.5),
        'w_up': nrm((N_DENSE, D_MODEL, D_FF), D_MODEL ** -0.5),
        'w_down': nrm((N_DENSE, D_FF, D_MODEL), D_FF ** -0.5),
        'w_router': nrm((N_MOE, D_MODEL, N_EXPERTS), D_MODEL ** -0.5),
        'we_gate': nrm((N_MOE, N_EXPERTS, D_MODEL, D_FF_E), D_MODEL ** -0.5),
        'we_up': nrm((N_MOE, N_EXPERTS, D_MODEL, D_FF_E), D_MODEL ** -0.5),
        'we_down': nrm((N_MOE, N_EXPERTS, D_FF_E, D_MODEL), D_FF_E ** -0.5),
        'final_norm': gain((D_MODEL,)),
    }


def reference(x_prompt, x_sample, cache_win_k, cache_win_v, state_conv, cache_mem_k, cache_mem_v,
              mem_prompt, norm_mix, w_in, attn_sinks, att_out_norm, conv_w, conv_b, conv_ln_g,
              conv_ln_b, conv_out_norm, w_out, norm_cross, norm_mem, w_xq, w_xk, w_xv, w_xo,
              norm_ffn, w_gate, w_up, w_down, w_router, we_gate, we_up, we_down, final_norm):
    xp, xs = x_prompt, x_sample
    pk, pv, pc, pmk, pmv, sk, sv, sc = [], [], [], [], [], [], [], []
    for l in range(DEPTH):
        mix_w = (w_in[l], attn_sinks[l], att_out_norm[l], conv_w[l], conv_b[l], conv_ln_g[l],
                 conv_ln_b[l], conv_out_norm[l], w_out[l])
        hp, kp_, vp_, cp_ = mixer_prompt(rms_norm(xp, norm_mix[l]), *mix_w)
        hs, ks_, vs_, cs_ = mixer_sample(rms_norm(xs, norm_mix[l]), cache_win_k[l], cache_win_v[l],
                                         state_conv[l], *mix_w)
        xp = xp + hp
        xs = xs + hs
        mk, mv = memory_kv(mem_prompt, norm_mem[l], w_xk[l], w_xv[l])
        xp = xp + cross_attend(rms_norm(xp, norm_cross[l]), mk, mv, w_xq[l], w_xo[l])
        xs = xs + cross_attend(rms_norm(xs, norm_cross[l]), cache_mem_k[l], cache_mem_v[l],
                               w_xq[l], w_xo[l])
        fp = rms_norm(xp, norm_ffn[l])
        fs = rms_norm(xs, norm_ffn[l])
        if l % 2 == 0:
            d = l // 2
            xp = xp + swiglu(fp, w_gate[d], w_up[d], w_down[d])
            xs = xs + swiglu(fs, w_gate[d], w_up[d], w_down[d])
        else:
            m = l // 2
            xp = xp + moe_swiglu(fp, w_router[m], we_gate[m], we_up[m], we_down[m])
            xs = xs + moe_swiglu(fs, w_router[m], we_gate[m], we_up[m], we_down[m])
        pk.append(kp_); pv.append(vp_); pc.append(cp_); pmk.append(mk); pmv.append(mv)
        sk.append(ks_); sv.append(vs_); sc.append(cs_)
    y_prompt = rms_norm(xp, final_norm)
    y_sample = rms_norm(xs, final_norm)
    win_k_prompt = jnp.stack(pk)
    win_v_prompt = jnp.stack(pv)
    conv_prompt = jnp.stack(pc)
    mem_k_prompt = jnp.stack(pmk)
    mem_v_prompt = jnp.stack(pmv)
    win_k_sample = jnp.stack(sk)
    win_v_sample = jnp.stack(sv)
    conv_sample = jnp.stack(sc)
    return (y_prompt, y_sample, win_k_prompt, win_v_prompt, conv_prompt, mem_k_prompt,
            mem_v_prompt, win_k_sample, win_v_sample, conv_sample)
```

```python
import functools

import jax
import jax.numpy as jnp
import numpy as np
from jax import lax
from jax.experimental import pallas as pl
from jax.experimental.pallas import tpu as pltpu

F32 = jnp.float32
BF16 = jnp.bfloat16

D_MODEL = 2048
SEQ = 8192
DEPTH = 2
DEC_BATCH = 128
DEC_SEQ = 4
HEAD_DIM = 64
ATT_W = 1024
N_HEADS = 16
N_KV = 4
KV_W = N_KV * HEAD_DIM
WINDOW = 128
CONV_C = 1024
CONV_W = 31
N_MEM = 256
X_HEADS = 4
X_HEAD_DIM = 128
X_W = X_HEADS * X_HEAD_DIM
D_FF = 5632
N_EXPERTS = 8
D_FF_E = 7168
EPS = 1e-6
NEG = -1e30

T_P = SEQ
T_S = DEC_BATCH * DEC_SEQ
T_ALL = T_P + T_S

LANES = 128
HALF = HEAD_DIM
STATE_ROWS = CONV_W - 1
STATE_PAD = 32
KEYS_PAD = 256
VMEM_LIMIT = 56 * 1024 * 1024

SLOPES = [float(2.0 ** (-8.0 * (h + 1) / N_HEADS)) for h in range(N_HEADS)]

U_W = ATT_W + 2 * CONV_C + 2 * KV_W


def _params(sem):
    return pltpu.CompilerParams(dimension_semantics=sem, vmem_limit_bytes=VMEM_LIMIT)


def _rms(x, g):
    r = lax.rsqrt(jnp.mean(x * x, axis=-1, keepdims=True) + EPS)
    return x * r * g


def _sigmoid(x):
    return 1.0 / (1.0 + jnp.exp(-x))


def _rms_matmul_body(x_ref, g_ref, w_ref, o_ref, xn_ref):
    @pl.when(pl.program_id(1) == 0)
    def _():
        xn_ref[...] = _rms(x_ref[...], g_ref[...]).astype(BF16)

    o_ref[...] = jnp.dot(xn_ref[...], w_ref[...], preferred_element_type=F32).astype(o_ref.dtype)


def rms_matmul(x, g, w, tm, tn, out_dtype=F32):
    m, k = x.shape
    n = w.shape[1]
    return pl.pallas_call(
        _rms_matmul_body,
        grid=(m // tm, n // tn),
        in_specs=[pl.BlockSpec((tm, k), lambda i, j: (i, 0)),
                  pl.BlockSpec((1, k), lambda i, j: (0, 0)),
                  pl.BlockSpec((k, tn), lambda i, j: (0, j))],
        out_specs=pl.BlockSpec((tm, tn), lambda i, j: (i, j)),
        out_shape=jax.ShapeDtypeStruct((m, n), out_dtype),
        scratch_shapes=[pltpu.VMEM((tm, k), BF16)],
        compiler_params=_params(("parallel", "arbitrary")),
        name="rms_matmul",
    )(x, g.reshape(1, k), w)


def _matmul_add_body(*refs, n_terms):
    x_ref = refs[0]
    o_ref = refs[-1]
    acc = x_ref[...]
    for t in range(n_terms):
        acc = acc + jnp.dot(refs[1 + t][...], refs[1 + n_terms + t][...],
                            preferred_element_type=F32)
    o_ref[...] = acc


def matmul_add(x, a_list, w, tm, tn):
    m, n = x.shape
    n_terms = len(a_list)
    in_specs = [pl.BlockSpec((tm, tn), lambda i, j: (i, j))]
    for a in a_list:
        in_specs.append(pl.BlockSpec((tm, a.shape[1]), lambda i, j: (i, 0)))
    row = 0
    for a in a_list:
        ka = a.shape[1]
        blk = row // ka
        in_specs.append(pl.BlockSpec((ka, tn), lambda i, j, blk=blk: (blk, j)))
        row += ka
    return pl.pallas_call(
        functools.partial(_matmul_add_body, n_terms=n_terms),
        grid=(m // tm, n // tn),
        in_specs=in_specs,
        out_specs=pl.BlockSpec((tm, tn), lambda i, j: (i, j)),
        out_shape=jax.ShapeDtypeStruct((m, n), F32),
        input_output_aliases={0: 0},
        compiler_params=_params(("parallel", "parallel")),
        name="matmul_add",
    )(x, *a_list, *([w] * n_terms))


def _half_mask(shape, half):
    lane = lax.broadcasted_iota(jnp.int32, shape, len(shape) - 1)
    return (lane >= HALF) if half else (lane < HALF)


def _sink_softmax_pv(s, sink, v_tile, batched):
    m = jnp.maximum(jnp.max(s, axis=-1, keepdims=True), sink)
    p = jnp.exp(s - m)
    den = jnp.sum(p, axis=-1, keepdims=True) + jnp.exp(sink - m)
    if batched:
        o = jnp.einsum("bqk,bkd->bqd", p.astype(BF16), v_tile, preferred_element_type=F32)
    else:
        o = jnp.dot(p.astype(BF16), v_tile, preferred_element_type=F32)
    return o / den


def _swa_prompt_body(sink_ref, q_ref, kp_ref, kc_ref, vp_ref, vc_ref, ga_ref, o_ref):
    i = pl.program_id(0)
    blk = WINDOW
    q = q_ref[...] * (HEAD_DIM ** -0.5)
    k = jnp.concatenate([kp_ref[...], kc_ref[...]], axis=0).astype(BF16)
    v = jnp.concatenate([vp_ref[...], vc_ref[...]], axis=0).astype(BF16)
    row = lax.broadcasted_iota(jnp.int32, (blk, 2 * blk), 0)
    col = lax.broadcasted_iota(jnp.int32, (blk, 2 * blk), 1)
    dist = blk + row - col
    valid = (dist >= 0) & (dist <= WINDOW) & ((col >= blk) | (i > 0))
    distf = dist.astype(F32)
    tiles = [None] * (N_HEADS // 2)
    for n in range(N_KV):
        half = n % 2
        ts = slice((n // 2) * LANES, (n // 2 + 1) * LANES)
        k_m = jnp.where(_half_mask((2 * blk, LANES), half), k[:, ts], jnp.zeros((), BF16))
        v_t = v[:, ts]
        q_al = jnp.concatenate([q[:, (2 * n) * LANES:(2 * n + 1) * LANES],
                                q[:, (2 * n + 1) * LANES:(2 * n + 2) * LANES]], axis=0)
        q_mis = pltpu.roll(q_al, HALF, 1)
        lhs = jnp.concatenate([q_al, q_mis], axis=0).astype(BF16)
        s = lax.dot_general(lhs, k_m, (((1,), (1,)), ((), ())), preferred_element_type=F32)
        outs = []
        for b4 in range(4):
            a, j = divmod(b4, 2)
            h = 4 * n + 2 * j + (half if a == 0 else 1 - half)
            sb = s[b4 * blk:(b4 + 1) * blk]
            sb = jnp.where(valid, sb - SLOPES[h] * distf, NEG)
            outs.append(_sink_softmax_pv(sb, sink_ref[h], v_t, False))
        keep = _half_mask((blk, LANES), half)
        for j in range(2):
            tiles[2 * n + j] = jnp.where(keep, outs[j], pltpu.roll(outs[2 + j], HALF, 1))
    att = jnp.concatenate(tiles, axis=1)
    o_ref[...] = _rms(att, ga_ref[...]).astype(o_ref.dtype)


def swa_prompt(u, sinks, g_att):
    blk = WINDOW
    nb = T_P // blk
    kcol = (ATT_W + 2 * CONV_C) // KV_W
    prev = lambda i: jnp.maximum(i - 1, 0)
    return pl.pallas_call(
        _swa_prompt_body,
        grid=(nb,),
        in_specs=[pl.BlockSpec(memory_space=pltpu.SMEM),
                  pl.BlockSpec((blk, ATT_W), lambda i: (i, 0)),
                  pl.BlockSpec((blk, KV_W), lambda i: (prev(i), kcol)),
                  pl.BlockSpec((blk, KV_W), lambda i: (i, kcol)),
                  pl.BlockSpec((blk, KV_W), lambda i: (prev(i), kcol + 1)),
                  pl.BlockSpec((blk, KV_W), lambda i: (i, kcol + 1)),
                  pl.BlockSpec((1, ATT_W), lambda i: (0, 0))],
        out_specs=pl.BlockSpec((blk, ATT_W), lambda i: (i, 0)),
        out_shape=jax.ShapeDtypeStruct((T_P, ATT_W), BF16),
        compiler_params=_params(("parallel",)),
        name="swa_prompt",
    )(sinks, u, u, u, u, u, g_att.reshape(1, ATT_W))


def _swa_sample_body(sink_ref, q_ref, kn_ref, vn_ref, kc_ref, vc_ref, bias_ref, ga_ref,
                     o_ref, ko_ref, vo_ref, kall_ref, vall_ref):
    bt = q_ref.shape[0]
    wb = kc_ref.shape[1]
    kall_ref[:, 0:wb, :] = kc_ref[...]
    kall_ref[:, wb:wb + DEC_SEQ, :] = kn_ref[...]
    kall_ref[:, wb + DEC_SEQ:, :] = jnp.zeros((bt, KEYS_PAD - wb - DEC_SEQ, KV_W), F32)
    vall_ref[:, 0:wb, :] = vc_ref[...]
    vall_ref[:, wb:wb + DEC_SEQ, :] = vn_ref[...]
    vall_ref[:, wb + DEC_SEQ:, :] = jnp.zeros((bt, KEYS_PAD - wb - DEC_SEQ, KV_W), F32)
    ko_ref[...] = kall_ref[:, DEC_SEQ:DEC_SEQ + wb, :]
    vo_ref[...] = vall_ref[:, DEC_SEQ:DEC_SEQ + wb, :]

    q = q_ref[...] * (HEAD_DIM ** -0.5)
    k = kall_ref[...].astype(BF16)
    v = vall_ref[...].astype(BF16)
    tiles = [None] * (N_HEADS // 2)
    for n in range(N_KV):
        half = n % 2
        ts = slice((n // 2) * LANES, (n // 2 + 1) * LANES)
        k_m = jnp.where(_half_mask((bt, KEYS_PAD, LANES), half), k[:, :, ts], jnp.zeros((), BF16))
        v_t = v[:, :, ts]
        q_al = jnp.concatenate([q[:, :, (2 * n) * LANES:(2 * n + 1) * LANES],
                                q[:, :, (2 * n + 1) * LANES:(2 * n + 2) * LANES]], axis=1)
        q_mis = pltpu.roll(q_al, HALF, 2)
        lhs = jnp.concatenate([q_al, q_mis], axis=1).astype(BF16)
        s = jnp.einsum("bqd,bkd->bqk", lhs, k_m, preferred_element_type=F32)
        bias = bias_ref[n]
        s = jnp.where(bias > 0.5 * NEG, s + bias, NEG)
        sink = sink_ref[n]
        o = _sink_softmax_pv(s, sink, v_t, True)
        keep = _half_mask((bt, DEC_SEQ, LANES), half)
        for j in range(2):
            o_al = o[:, j * DEC_SEQ:(j + 1) * DEC_SEQ]
            o_mis = pltpu.roll(o[:, (2 + j) * DEC_SEQ:(3 + j) * DEC_SEQ], HALF, 2)
            tiles[2 * n + j] = jnp.where(keep, o_al, o_mis)
    att = jnp.concatenate(tiles, axis=2)
    o_ref[...] = _rms(att, ga_ref[...]).astype(o_ref.dtype)


def _sample_bias_and_sinks(sinks):
    wb = WINDOW
    bias = np.full((N_KV, 4 * DEC_SEQ, KEYS_PAD), NEG, np.float32)
    head = np.zeros((N_KV, 4 * DEC_SEQ), np.int32)
    for n in range(N_KV):
        half = n % 2
        for a in range(2):
            for j in range(2):
                h = 4 * n + 2 * j + (half if a == 0 else 1 - half)
                for t in range(DEC_SEQ):
                    r = (2 * a + j) * DEC_SEQ + t
                    head[n, r] = h
                    for kk in range(wb + DEC_SEQ):
                        d = t + wb - kk
                        if 0 <= d <= WINDOW:
                            bias[n, r, kk] = -SLOPES[h] * d
    sink_rows = sinks[jnp.asarray(head)][..., None]
    return jnp.asarray(bias), sink_rows


def swa_sample(q_s, k_new, v_new, cache_k, cache_v, sinks, g_att, bt):
    wb = cache_k.shape[1]
    bias, sink_rows = _sample_bias_and_sinks(sinks)
    nq = 4 * DEC_SEQ
    b3 = lambda i: (i, 0, 0)
    z3 = lambda i: (0, 0, 0)
    return pl.pallas_call(
        _swa_sample_body,
        grid=(DEC_BATCH // bt,),
        in_specs=[pl.BlockSpec((N_KV, nq, 1), z3),
                  pl.BlockSpec((bt, DEC_SEQ, ATT_W), b3),
                  pl.BlockSpec((bt, DEC_SEQ, KV_W), b3),
                  pl.BlockSpec((bt, DEC_SEQ, KV_W), b3),
                  pl.BlockSpec((bt, wb, KV_W), b3),
                  pl.BlockSpec((bt, wb, KV_W), b3),
                  pl.BlockSpec((N_KV, nq, KEYS_PAD), z3),
                  pl.BlockSpec((1, 1, ATT_W), z3)],
        out_specs=[pl.BlockSpec((bt, DEC_SEQ, ATT_W), b3),
                   pl.BlockSpec((bt, wb, KV_W), b3),
                   pl.BlockSpec((bt, wb, KV_W), b3)],
        out_shape=[jax.ShapeDtypeStruct((DEC_BATCH, DEC_SEQ, ATT_W), BF16),
                   jax.ShapeDtypeStruct((DEC_BATCH, wb, KV_W), F32),
                   jax.ShapeDtypeStruct((DEC_BATCH, wb, KV_W), F32)],
        scratch_shapes=[pltpu.VMEM((bt, KEYS_PAD, KV_W), F32),
                        pltpu.VMEM((bt, KEYS_PAD, KV_W), F32)],
        compiler_params=_params(("parallel",)),
        name="swa_sample",
    )(sink_rows, q_s, k_new, v_new, cache_k, cache_v, bias, g_att.reshape(1, 1, ATT_W))


def _conv_post(y, cb, lg, lb, gc):
    y = y + cb
    mu = jnp.mean(y, axis=-1, keepdims=True)
    yc = y - mu
    z = yc * lax.rsqrt(jnp.mean(yc * yc, axis=-1, keepdims=True) + EPS) * lg + lb
    c = z * _sigmoid(z)
    return _rms(c, gc)


def _conv_prompt_body(ap_ref, gp_ref, a_ref, g_ref, cw_ref, cb_ref, lg_ref, lb_ref, gc_ref,
                      o_ref, st_ref, ext_ref, y_ref, *, tt, tc):
    i = pl.program_id(0)
    glu_prev = ap_ref[...] * _sigmoid(gp_ref[...])
    ext_ref[0:STATE_PAD, :] = jnp.where(i > 0, glu_prev, 0.0)
    glu = a_ref[...] * _sigmoid(g_ref[...])
    ext_ref[STATE_PAD:STATE_PAD + tt, :] = glu
    st_ref[...] = glu[tt - STATE_PAD:]
    first = STATE_PAD - STATE_ROWS
    for cblk in range(CONV_C // LANES):
        cs = slice(cblk * LANES, (cblk + 1) * LANES)
        for t0 in range(0, tt, tc):
            acc = None
            for j in range(CONV_W):
                lo = t0 + first + j
                term = ext_ref[lo:lo + tc, cs] * cw_ref[j:j + 1, cs]
                acc = term if acc is None else acc + term
            y_ref[t0:t0 + tc, cs] = acc
    o_ref[...] = _conv_post(y_ref[...], cb_ref[...], lg_ref[...], lb_ref[...],
                            gc_ref[...]).astype(o_ref.dtype)


def conv_prompt(u, conv_w, conv_b, ln_g, ln_b, g_conv, tt=128, tc=64):
    nt = T_P // tt
    per = tt // STATE_PAD
    prev = lambda i: jnp.maximum(i * per - 1, 0)
    acol, gcol = ATT_W // CONV_C, ATT_W // CONV_C + 1
    vec = lambda: pl.BlockSpec((1, CONV_C), lambda i: (0, 0))
    out, state = pl.pallas_call(
        functools.partial(_conv_prompt_body, tt=tt, tc=tc),
        grid=(nt,),
        in_specs=[pl.BlockSpec((STATE_PAD, CONV_C), lambda i: (prev(i), acol)),
                  pl.BlockSpec((STATE_PAD, CONV_C), lambda i: (prev(i), gcol)),
                  pl.BlockSpec((tt, CONV_C), lambda i: (i, acol)),
                  pl.BlockSpec((tt, CONV_C), lambda i: (i, gcol)),
                  pl.BlockSpec((CONV_W, CONV_C), lambda i: (0, 0)),
                  vec(), vec(), vec(), vec()],
        out_specs=[pl.BlockSpec((tt, CONV_C), lambda i: (i, 0)),
                   pl.BlockSpec((STATE_PAD, CONV_C), lambda i: (0, 0))],
        out_shape=[jax.ShapeDtypeStruct((T_P, CONV_C), BF16),
                   jax.ShapeDtypeStruct((STATE_PAD, CONV_C), F32)],
        scratch_shapes=[pltpu.VMEM((STATE_PAD + tt, CONV_C), F32),
                        pltpu.VMEM((tt, CONV_C), F32)],
        compiler_params=_params(("arbitrary",)),
        name="conv_prompt",
    )(u, u, u, u, conv_w, conv_b.reshape(1, -1), ln_g.reshape(1, -1), ln_b.reshape(1, -1),
      g_conv.reshape(1, -1))
    return out, state[STATE_PAD - STATE_ROWS:]


def _conv_sample_body(st_ref, a_ref, g_ref, cw_ref, cb_ref, lg_ref, lb_ref, gc_ref,
                      o_ref, so_ref):
    glu = a_ref[...] * _sigmoid(g_ref[...])
    ext = jnp.concatenate([st_ref[...], glu], axis=1)
    so_ref[...] = ext[:, DEC_SEQ:]
    w = cw_ref[...]
    for t in range(DEC_SEQ):
        y = jnp.sum(ext[:, t:t + CONV_W] * w[None], axis=1)
        o_ref[t] = _conv_post(y, cb_ref[...], lg_ref[...], lb_ref[...],
                              gc_ref[...]).astype(o_ref.dtype)


def conv_sample(state, a_s, g_s, conv_w, conv_b, ln_g, ln_b, g_conv, bt):
    b3 = lambda i: (i, 0, 0)
    vec = lambda: pl.BlockSpec((1, CONV_C), lambda i: (0, 0))
    return pl.pallas_call(
        _conv_sample_body,
        grid=(DEC_BATCH // bt,),
        in_specs=[pl.BlockSpec((bt, STATE_ROWS, CONV_C), b3),
                  pl.BlockSpec((bt, DEC_SEQ, CONV_C), b3),
                  pl.BlockSpec((bt, DEC_SEQ, CONV_C), b3),
                  pl.BlockSpec((CONV_W, CONV_C), lambda i: (0, 0)),
                  vec(), vec(), vec(), vec()],
        out_specs=[pl.BlockSpec((DEC_SEQ, bt, CONV_C), lambda i: (0, i, 0)),
                   pl.BlockSpec((bt, STATE_ROWS, CONV_C), b3)],
        out_shape=[jax.ShapeDtypeStruct((DEC_SEQ, DEC_BATCH, CONV_C), BF16),
                   jax.ShapeDtypeStruct((DEC_BATCH, STATE_ROWS, CONV_C), F32)],
        compiler_params=_params(("parallel",)),
        name="conv_sample",
    )(state, a_s, g_s, conv_w, conv_b.reshape(1, -1), ln_g.reshape(1, -1), ln_b.reshape(1, -1),
      g_conv.reshape(1, -1))


def _cross_prompt_body(q_ref, k_ref, v_ref, o_ref):
    q = q_ref[...].astype(BF16)
    k = k_ref[...].astype(BF16)
    v = v_ref[...].astype(BF16)
    outs = []
    for h in range(X_HEADS):
        hs = slice(h * X_HEAD_DIM, (h + 1) * X_HEAD_DIM)
        s = lax.dot_general(q[:, hs], k[:, hs], (((1,), (1,)), ((), ())),
                            preferred_element_type=F32) * (X_HEAD_DIM ** -0.5)
        p = jnp.exp(s - jnp.max(s, axis=-1, keepdims=True))
        den = jnp.sum(p, axis=-1, keepdims=True)
        outs.append(jnp.dot(p.astype(BF16), v[:, hs], preferred_element_type=F32) / den)
    o_ref[...] = jnp.concatenate(outs, axis=1).astype(o_ref.dtype)


def cross_prompt(q, mem_k, mem_v, tq=512):
    return pl.pallas_call(
        _cross_prompt_body,
        grid=(T_P // tq,),
        in_specs=[pl.BlockSpec((tq, X_W), lambda i: (i, 0)),
                  pl.BlockSpec((N_MEM, X_W), lambda i: (0, 0)),
                  pl.BlockSpec((N_MEM, X_W), lambda i: (0, 0))],
        out_specs=pl.BlockSpec((tq, X_W), lambda i: (i, 0)),
        out_shape=jax.ShapeDtypeStruct((T_P, X_W), BF16),
        compiler_params=_params(("parallel",)),
        name="cross_prompt",
    )(q, mem_k, mem_v)


def _cross_sample_body(q_ref, k_ref, v_ref, o_ref):
    q = q_ref[...].astype(BF16)
    outs = []
    for h in range(X_HEADS):
        hs = slice(h * X_HEAD_DIM, (h + 1) * X_HEAD_DIM)
        kh = k_ref[:, :, hs].astype(BF16)
        vh = v_ref[:, :, hs].astype(BF16)
        s = jnp.einsum("btd,bmd->btm", q[:, :, hs], kh,
                       preferred_element_type=F32) * (X_HEAD_DIM ** -0.5)
        p = jnp.exp(s - jnp.max(s, axis=-1, keepdims=True))
        den = jnp.sum(p, axis=-1, keepdims=True)
        outs.append(jnp.einsum("btm,bmd->btd", p.astype(BF16), vh,
                               preferred_element_type=F32) / den)
    o_ref[...] = jnp.concatenate(outs, axis=2).astype(o_ref.dtype)


def cross_sample(q_pad, mem_k, mem_v, bt=8):
    rows = q_pad.shape[1]
    b3 = lambda i: (i, 0, 0)
    return pl.pallas_call(
        _cross_sample_body,
        grid=(DEC_BATCH // bt,),
        in_specs=[pl.BlockSpec((bt, rows, X_W), b3),
                  pl.BlockSpec((bt, N_MEM, X_W), b3),
                  pl.BlockSpec((bt, N_MEM, X_W), b3)],
        out_specs=pl.BlockSpec((bt, rows, X_W), b3),
        out_shape=jax.ShapeDtypeStruct((DEC_BATCH, rows, X_W), BF16),
        compiler_params=_params(("parallel",)),
        name="cross_sample",
    )(q_pad, mem_k, mem_v)


def _swiglu_step(x, wg_ref, wu_ref, wd_ref):
    hg = jnp.dot(x, wg_ref[...].astype(BF16), preferred_element_type=F32)
    hu = jnp.dot(x, wu_ref[...].astype(BF16), preferred_element_type=F32)
    h = (hg * _sigmoid(hg) * hu).astype(BF16)
    return jnp.dot(h, wd_ref[...].astype(BF16), preferred_element_type=F32)


def _dense_ffn_body(x_ref, g_ref, wg_ref, wu_ref, wd_ref, o_ref, xn_ref):
    @pl.when(pl.program_id(1) == 0)
    def _():
        x = x_ref[...]
        xn_ref[...] = _rms(x, g_ref[...]).astype(BF16)
        o_ref[...] = x

    o_ref[...] += _swiglu_step(xn_ref[...], wg_ref, wu_ref, wd_ref)


def dense_ffn(x, g, wg, wu, wd, tm, tf):
    m, d = x.shape
    dff = wg.shape[1]
    return pl.pallas_call(
        _dense_ffn_body,
        grid=(m // tm, dff // tf),
        in_specs=[pl.BlockSpec((tm, d), lambda i, f: (i, 0)),
                  pl.BlockSpec((1, d), lambda i, f: (0, 0)),
                  pl.BlockSpec((d, tf), lambda i, f: (0, f)),
                  pl.BlockSpec((d, tf), lambda i, f: (0, f)),
                  pl.BlockSpec((tf, d), lambda i, f: (f, 0))],
        out_specs=pl.BlockSpec((tm, d), lambda i, f: (i, 0)),
        out_shape=jax.ShapeDtypeStruct((m, d), F32),
        scratch_shapes=[pltpu.VMEM((tm, d), BF16)],
        compiler_params=_params(("parallel", "arbitrary")),
        name="dense_ffn",
    )(x, g.reshape(1, d), wg, wu, wd)


def _router_body(x_ref, g_ref, wr_ref, idx_ref, gate_ref):
    xn = _rms(x_ref[...], g_ref[...])
    logits = jnp.dot(xn, wr_ref[...], preferred_element_type=F32,
                     precision=lax.Precision.HIGHEST)
    lane = lax.broadcasted_iota(jnp.int32, logits.shape, 1)
    lg = jnp.where(lane < N_EXPERTS, logits, -jnp.inf)
    m1 = jnp.max(lg, axis=-1, keepdims=True)
    i1 = jnp.min(jnp.where(lg == m1, lane, LANES), axis=-1, keepdims=True)
    lg2 = jnp.where(lane == i1, -jnp.inf, lg)
    m2 = jnp.max(lg2, axis=-1, keepdims=True)
    i2 = jnp.min(jnp.where(lg2 == m2, lane, LANES), axis=-1, keepdims=True)
    e = jnp.exp(m2 - m1)
    den = 1.0 + e
    idx_ref[...] = jnp.where(lane == 0, i1, jnp.where(lane == 1, i2, 0))
    gate_ref[...] = jnp.where(lane == 0, 1.0 / den, jnp.where(lane == 1, e / den, 0.0))


def router(x, g, w_router, tm=512):
    m, d = x.shape
    wr = jnp.zeros((d, LANES), F32).at[:, :N_EXPERTS].set(w_router)
    return pl.pallas_call(
        _router_body,
        grid=(m // tm,),
        in_specs=[pl.BlockSpec((tm, d), lambda i: (i, 0)),
                  pl.BlockSpec((1, d), lambda i: (0, 0)),
                  pl.BlockSpec((d, LANES), lambda i: (0, 0))],
        out_specs=[pl.BlockSpec((tm, LANES), lambda i: (i, 0)),
                   pl.BlockSpec((tm, LANES), lambda i: (i, 0))],
        out_shape=[jax.ShapeDtypeStruct((m, LANES), jnp.int32),
                   jax.ShapeDtypeStruct((m, LANES), F32)],
        compiler_params=_params(("parallel",)),
        name="router",
    )(x, g.reshape(1, d), wr)


def _row_copy(src_hbm, row, dst_ref, r, sem):
    return pltpu.make_async_copy(src_hbm.at[pl.ds(row, 1)], dst_ref.at[pl.ds(r, 1)], sem)


def _gather_norm_body(src_ref, x_hbm, g_ref, o_ref, buf_ref, sem, *, rows):
    base = pl.program_id(0) * rows

    def start(r, c):
        _row_copy(x_hbm, src_ref[base + r], buf_ref, r, sem).start()
        return c

    def wait(r, c):
        _row_copy(x_hbm, 0, buf_ref, r, sem).wait()
        return c

    lax.fori_loop(0, rows, start, 0)
    lax.fori_loop(0, rows, wait, 0)
    o_ref[...] = _rms(buf_ref[...], g_ref[...]).astype(o_ref.dtype)


def gather_norm(x, g, src_rows, rows=512):
    d = x.shape[1]
    n = src_rows.shape[0]
    grid_spec = pltpu.PrefetchScalarGridSpec(
        num_scalar_prefetch=1,
        grid=(n // rows,),
        in_specs=[pl.BlockSpec(memory_space=pl.ANY),
                  pl.BlockSpec((1, d), lambda i, s: (0, 0))],
        out_specs=pl.BlockSpec((rows, d), lambda i, s: (i, 0)),
        scratch_shapes=[pltpu.VMEM((rows, d), F32), pltpu.SemaphoreType.DMA(())],
    )
    return pl.pallas_call(
        functools.partial(_gather_norm_body, rows=rows),
        grid_spec=grid_spec,
        out_shape=jax.ShapeDtypeStruct((n, d), BF16),
        compiler_params=_params(("arbitrary",)),
        name="gather_norm",
    )(src_rows, x, g.reshape(1, d))


def _moe_ffn_body(te_ref, tv_ref, x_ref, wg_ref, wu_ref, wd_ref, o_ref):
    i = pl.program_id(0)

    @pl.when(pl.program_id(1) == 0)
    def _():
        o_ref[...] = jnp.zeros(o_ref.shape, o_ref.dtype)

    @pl.when(tv_ref[i] > 0)
    def _():
        o_ref[...] += _swiglu_step(x_ref[...], wg_ref, wu_ref, wd_ref)


def moe_ffn(xs, tile_expert, tile_valid, wg, wu, wd, tm, tf):
    n, d = xs.shape
    dff = wg.shape[2]
    nf = dff // tf
    fi = lambda i, f, te, tv: jnp.where(tv[i] > 0, f, nf - 1)
    grid_spec = pltpu.PrefetchScalarGridSpec(
        num_scalar_prefetch=2,
        grid=(n // tm, nf),
        in_specs=[pl.BlockSpec((tm, d), lambda i, f, te, tv: (i, 0)),
                  pl.BlockSpec((None, d, tf), lambda i, f, te, tv: (te[i], 0, fi(i, f, te, tv))),
                  pl.BlockSpec((None, d, tf), lambda i, f, te, tv: (te[i], 0, fi(i, f, te, tv))),
                  pl.BlockSpec((None, tf, d), lambda i, f, te, tv: (te[i], fi(i, f, te, tv), 0))],
        out_specs=pl.BlockSpec((tm, d), lambda i, f, te, tv: (i, 0)),
    )
    return pl.pallas_call(
        _moe_ffn_body,
        grid_spec=grid_spec,
        out_shape=jax.ShapeDtypeStruct((n, d), F32),
        compiler_params=_params(("parallel", "arbitrary")),
        name="moe_ffn",
    )(tile_expert, tile_valid, xs, wg, wu, wd)


def _combine_body(pos_ref, x_ref, gate_ref, y_hbm, o_ref, buf_ref, sem, *, rows, n_tok):
    base = pl.program_id(0) * rows

    def start(r, c):
        _row_copy(y_hbm, pos_ref[base + r], buf_ref.at[0], r, sem).start()
        _row_copy(y_hbm, pos_ref[n_tok + base + r], buf_ref.at[1], r, sem).start()
        return c

    def wait(r, c):
        _row_copy(y_hbm, 0, buf_ref.at[0], r, sem).wait()
        _row_copy(y_hbm, 0, buf_ref.at[1], r, sem).wait()
        return c

    lax.fori_loop(0, rows, start, 0)
    lax.fori_loop(0, rows, wait, 0)
    gates = gate_ref[...]
    o_ref[...] = x_ref[...] + (gates[:, 0:1] * buf_ref[0] + gates[:, 1:2] * buf_ref[1])


def moe_combine(x, gates, y, pos, rows=256):
    n_tok, d = x.shape
    grid_spec = pltpu.PrefetchScalarGridSpec(
        num_scalar_prefetch=1,
        grid=(n_tok // rows,),
        in_specs=[pl.BlockSpec((rows, d), lambda i, p: (i, 0)),
                  pl.BlockSpec((rows, LANES), lambda i, p: (i, 0)),
                  pl.BlockSpec(memory_space=pl.ANY)],
        out_specs=pl.BlockSpec((rows, d), lambda i, p: (i, 0)),
        scratch_shapes=[pltpu.VMEM((2, rows, d), F32), pltpu.SemaphoreType.DMA(())],
    )
    return pl.pallas_call(
        functools.partial(_combine_body, rows=rows, n_tok=n_tok),
        grid_spec=grid_spec,
        out_shape=jax.ShapeDtypeStruct((n_tok, d), F32),
        compiler_params=_params(("arbitrary",)),
        name="moe_combine",
    )(pos, x, gates, y)


def moe_layer(x, g, w_router, wg, wu, wd, tm=1024, tf=256):
    n_tok = x.shape[0]
    idx_full, gate_full = router(x, g, w_router)
    e_flat = idx_full[:, :2].T.reshape(-1)
    n_assign = e_flat.shape[0]
    n_tiles = n_assign // tm + N_EXPERTS
    order = jnp.argsort(e_flat, stable=True).astype(jnp.int32)
    counts = jnp.sum((e_flat[:, None] == jnp.arange(N_EXPERTS)[None, :]).astype(jnp.int32), axis=0)
    tiles_per = (counts + tm - 1) // tm
    tile_end = jnp.cumsum(tiles_per)
    tile_start = tile_end - tiles_per
    sorted_start = jnp.cumsum(counts) - counts
    n_used = tile_end[-1]
    tile_ids = jnp.arange(n_tiles, dtype=jnp.int32)
    tile_valid = (tile_ids < n_used).astype(jnp.int32)
    last_tile = jnp.maximum(n_used - 1, 0)
    tile_expert = jnp.searchsorted(tile_end, jnp.minimum(tile_ids, last_tile),
                                   side="right").astype(jnp.int32)
    slot = jnp.arange(n_tiles * tm, dtype=jnp.int32)
    slot_e = tile_expert[slot // tm]
    rank = slot - tile_start[slot_e] * tm
    real = (rank < counts[slot_e]) & (tile_valid[slot // tm] > 0)
    sorted_pos = jnp.clip(sorted_start[slot_e] + rank, 0, n_assign - 1)
    src_rows = jnp.where(real, order[sorted_pos] % n_tok, 0).astype(jnp.int32)
    inv = jnp.zeros((n_assign,), jnp.int32).at[order].set(jnp.arange(n_assign, dtype=jnp.int32))
    pos = (tile_start[e_flat] * tm + inv - sorted_start[e_flat]).astype(jnp.int32)

    xs = gather_norm(x, g, src_rows)
    y = moe_ffn(xs, tile_expert, tile_valid, wg, wu, wd, tm, tf)
    return moe_combine(x, gate_full, y, pos)


def _final_norm_body(x_ref, g_ref, o_ref):
    o_ref[...] = _rms(x_ref[...], g_ref[...])


def final_norm(x, g, first_block, n_blocks, tm=512):
    d = x.shape[1]
    return pl.pallas_call(
        _final_norm_body,
        grid=(n_blocks,),
        in_specs=[pl.BlockSpec((tm, d), lambda i: (first_block + i, 0)),
                  pl.BlockSpec((1, d), lambda i: (0, 0))],
        out_specs=pl.BlockSpec((tm, d), lambda i: (i, 0)),
        out_shape=jax.ShapeDtypeStruct((n_blocks * tm, d), F32),
        compiler_params=_params(("parallel",)),
        name="final_norm",
    )(x, g.reshape(1, d))


def kernel(x_prompt, x_sample, cache_win_k, cache_win_v, state_conv, cache_mem_k, cache_mem_v,
           mem_prompt, norm_mix, w_in, attn_sinks, att_out_norm, conv_w, conv_b, conv_ln_g,
           conv_ln_b, conv_out_norm, w_out, norm_cross, norm_mem, w_xq, w_xk, w_xv, w_xo,
           norm_ffn, w_gate, w_up, w_down, w_router, we_gate, we_up, we_down, final_norm_g):
    x = jnp.concatenate([x_prompt.reshape(T_P, D_MODEL), x_sample.reshape(T_S, D_MODEL)], axis=0)
    mem = mem_prompt.reshape(N_MEM, D_MODEL)
    o1 = ATT_W
    o2 = o1 + KV_W
    o3 = o2 + KV_W
    o4 = o3 + CONV_C
    wb = cache_win_k.shape[2]
    pk, pv, pc, pmk, pmv, sk, sv, sc = [], [], [], [], [], [], [], []
    for l in range(DEPTH):
        wl = w_in[l]
        w_u = jnp.concatenate([wl[:, :o1], wl[:, o3:o4], wl[:, o4:], wl[:, o1:o2], wl[:, o2:o3]],
                              axis=1).astype(BF16)
        u = rms_matmul(x, norm_mix[l], w_u, tm=512, tn=512)
        ka, kg, kk, kv = ATT_W, ATT_W + CONV_C, ATT_W + 2 * CONV_C, ATT_W + 2 * CONV_C + KV_W

        att_p = swa_prompt(u, attn_sinks[l], att_out_norm[l])
        conv_p, state_p = conv_prompt(u, conv_w[l], conv_b[l], conv_ln_g[l], conv_ln_b[l],
                                      conv_out_norm[l])
        us = u[T_P:]
        q_s = us[:, :ATT_W].reshape(DEC_BATCH, DEC_SEQ, ATT_W)
        a_s = us[:, ka:kg].reshape(DEC_BATCH, DEC_SEQ, CONV_C)
        g_s = us[:, kg:kk].reshape(DEC_BATCH, DEC_SEQ, CONV_C)
        k_s = us[:, kk:kv].reshape(DEC_BATCH, DEC_SEQ, KV_W)
        v_s = us[:, kv:].reshape(DEC_BATCH, DEC_SEQ, KV_W)
        att_s, k_win, v_win = swa_sample(q_s, k_s, v_s,
                                         cache_win_k[l].reshape(DEC_BATCH, wb, KV_W),
                                         cache_win_v[l].reshape(DEC_BATCH, wb, KV_W),
                                         attn_sinks[l], att_out_norm[l], bt=16)
        conv_s, state_s = conv_sample(state_conv[l], a_s, g_s, conv_w[l], conv_b[l],
                                      conv_ln_g[l], conv_ln_b[l], conv_out_norm[l], bt=16)
        conv_s = conv_s.transpose(1, 0, 2).reshape(T_S, CONV_C)
        att = jnp.concatenate([att_p, att_s.reshape(T_S, ATT_W)], axis=0)
        cnv = jnp.concatenate([conv_p, conv_s], axis=0)
        x = matmul_add(x, [att, cnv], w_out[l].astype(BF16), tm=512, tn=512)

        keep = min(WINDOW, SEQ)
        pk.append(u[T_P - keep:T_P, kk:kv].reshape(1, keep, N_KV, HEAD_DIM))
        pv.append(u[T_P - keep:T_P, kv:].reshape(1, keep, N_KV, HEAD_DIM))
        pc.append(state_p.reshape(1, STATE_ROWS, CONV_C))
        sk.append(k_win.reshape(DEC_BATCH, wb, N_KV, HEAD_DIM))
        sv.append(v_win.reshape(DEC_BATCH, wb, N_KV, HEAD_DIM))
        sc.append(state_s)

        mk = rms_matmul(mem, norm_mem[l], w_xk[l].astype(BF16), tm=N_MEM, tn=X_W)
        mv = rms_matmul(mem, norm_mem[l], w_xv[l].astype(BF16), tm=N_MEM, tn=X_W)
        pmk.append(mk.reshape(1, N_MEM, X_HEADS, X_HEAD_DIM))
        pmv.append(mv.reshape(1, N_MEM, X_HEADS, X_HEAD_DIM))
        qx = rms_matmul(x, norm_cross[l], w_xq[l].astype(BF16), tm=512, tn=X_W)
        o_p = cross_prompt(qx, mk, mv)
        qs_pad = jnp.pad(qx[T_P:].reshape(DEC_BATCH, DEC_SEQ, X_W), ((0, 0), (0, 8 - DEC_SEQ), (0, 0)))
        o_s = cross_sample(qs_pad, cache_mem_k[l].reshape(DEC_BATCH, N_MEM, X_W),
                           cache_mem_v[l].reshape(DEC_BATCH, N_MEM, X_W))
        o_all = jnp.concatenate([o_p, o_s[:, :DEC_SEQ].reshape(T_S, X_W)], axis=0)
        x = matmul_add(x, [o_all], w_xo[l].astype(BF16), tm=512, tn=512)

        if l % 2 == 0:
            d = l // 2
            x = dense_ffn(x, norm_ffn[l], w_gate[d], w_up[d], w_down[d], tm=1088, tf=256)
        else:
            m = l // 2
            x = moe_layer(x, norm_ffn[l], w_router[m], we_gate[m], we_up[m], we_down[m])

    y_prompt = final_norm(x, final_norm_g, 0, T_P // 512).reshape(1, SEQ, D_MODEL)
    y_sample = final_norm(x, final_norm_g, T_P // 512, T_S // 512).reshape(DEC_BATCH, DEC_SEQ, D_MODEL)
    return (y_prompt, y_sample, jnp.stack(pk), jnp.stack(pv), jnp.stack(pc), jnp.stack(pmk),
            jnp.stack(pmv), jnp.stack(sk), jnp.stack(sv), jnp.stack(sc))
```

```python
import functools

import jax
import jax.numpy as jnp
import numpy as np
from jax import lax
from jax.experimental import pallas as pl
from jax.experimental.pallas import tpu as pltpu

F32 = jnp.float32
BF16 = jnp.bfloat16

D_MODEL = 2048
SEQ = 8192
DEPTH = 2
DEC_BATCH = 128
DEC_SEQ = 4
HEAD_DIM = 64
ATT_W = 1024
N_HEADS = 16
N_KV = 4
KV_W = N_KV * HEAD_DIM
WINDOW = 128
CONV_C = 1024
CONV_W = 31
N_MEM = 256
X_HEADS = 4
X_HEAD_DIM = 128
X_W = X_HEADS * X_HEAD_DIM
D_FF = 5632
N_EXPERTS = 8
D_FF_E = 7168
EPS = 1e-6
NEG = -1e30

T_P = SEQ
T_S = DEC_BATCH * DEC_SEQ
T_ALL = T_P + T_S

LANES = 128
HALF = HEAD_DIM
STATE_ROWS = CONV_W - 1
STATE_PAD = 32
KEYS_PAD = 256
VMEM_LIMIT = 56 * 1024 * 1024

SLOPES = [float(2.0 ** (-8.0 * (h + 1) / N_HEADS)) for h in range(N_HEADS)]

U_W = ATT_W + 2 * CONV_C + 2 * KV_W


def _params(sem):
    return pltpu.CompilerParams(dimension_semantics=sem, vmem_limit_bytes=VMEM_LIMIT)


def _rms(x, g):
    r = lax.rsqrt(jnp.mean(x * x, axis=-1, keepdims=True) + EPS)
    return x * r * g


def _sigmoid(x):
    return 1.0 / (1.0 + jnp.exp(-x))


def _rms_matmul_body(x_ref, g_ref, w_ref, o_ref, xn_ref):
    @pl.when(pl.program_id(1) == 0)
    def _():
        xn_ref[...] = _rms(x_ref[...], g_ref[...]).astype(BF16)

    o_ref[...] = jnp.dot(xn_ref[...], w_ref[...], preferred_element_type=F32).astype(o_ref.dtype)


def rms_matmul(x, g, w, tm, tn, out_dtype=F32):
    m, k = x.shape
    n = w.shape[1]
    return pl.pallas_call(
        _rms_matmul_body,
        grid=(m // tm, n // tn),
        in_specs=[pl.BlockSpec((tm, k), lambda i, j: (i, 0)),
                  pl.BlockSpec((1, k), lambda i, j: (0, 0)),
                  pl.BlockSpec((k, tn), lambda i, j: (0, j))],
        out_specs=pl.BlockSpec((tm, tn), lambda i, j: (i, j)),
        out_shape=jax.ShapeDtypeStruct((m, n), out_dtype),
        scratch_shapes=[pltpu.VMEM((tm, k), BF16)],
        compiler_params=_params(("parallel", "arbitrary")),
        name="rms_matmul",
    )(x, g.reshape(1, k), w)


def _matmul_add_body(*refs, n_terms):
    x_ref = refs[0]
    o_ref = refs[-1]
    acc = x_ref[...]
    for t in range(n_terms):
        acc = acc + jnp.dot(refs[1 + t][...], refs[1 + n_terms + t][...],
                            preferred_element_type=F32)
    o_ref[...] = acc


def matmul_add(x, a_list, w, tm, tn):
    m, n = x.shape
    n_terms = len(a_list)
    in_specs = [pl.BlockSpec((tm, tn), lambda i, j: (i, j))]
    for a in a_list:
        in_specs.append(pl.BlockSpec((tm, a.shape[1]), lambda i, j: (i, 0)))
    row = 0
    for a in a_list:
        ka = a.shape[1]
        blk = row // ka
        in_specs.append(pl.BlockSpec((ka, tn), lambda i, j, blk=blk: (blk, j)))
        row += ka
    return pl.pallas_call(
        functools.partial(_matmul_add_body, n_terms=n_terms),
        grid=(m // tm, n // tn),
        in_specs=in_specs,
        out_specs=pl.BlockSpec((tm, tn), lambda i, j: (i, j)),
        out_shape=jax.ShapeDtypeStruct((m, n), F32),
        input_output_aliases={0: 0},
        compiler_params=_params(("parallel", "parallel")),
        name="matmul_add",
    )(x, *a_list, *([w] * n_terms))


def _half_mask(shape, half):
    lane = lax.broadcasted_iota(jnp.int32, shape, len(shape) - 1)
    return (lane >= HALF) if half else (lane < HALF)


def _sink_softmax_pv(s, sink, v_tile, batched):
    m = jnp.maximum(jnp.max(s, axis=-1, keepdims=True), sink)
    p = jnp.exp(s - m)
    den = jnp.sum(p, axis=-1, keepdims=True) + jnp.exp(sink - m)
    if batched:
        o = jnp.einsum("bqk,bkd->bqd", p.astype(BF16), v_tile, preferred_element_type=F32)
    else:
        o = jnp.dot(p.astype(BF16), v_tile, preferred_element_type=F32)
    return o / den


def _swa_prompt_body(sink_ref, q_ref, kp_ref, kc_ref, vp_ref, vc_ref, ga_ref, o_ref):
    i = pl.program_id(0)
    blk = WINDOW
    q = q_ref[...] * (HEAD_DIM ** -0.5)
    k = jnp.concatenate([kp_ref[...], kc_ref[...]], axis=0).astype(BF16)
    v = jnp.concatenate([vp_ref[...], vc_ref[...]], axis=0).astype(BF16)
    row = lax.broadcasted_iota(jnp.int32, (blk, 2 * blk), 0)
    col = lax.broadcasted_iota(jnp.int32, (blk, 2 * blk), 1)
    dist = blk + row - col
    valid = (dist >= 0) & (dist <= WINDOW) & ((col >= blk) | (i > 0))
    distf = dist.astype(F32)
    tiles = [None] * (N_HEADS // 2)
    for n in range(N_KV):
        half = n % 2
        ts = slice((n // 2) * LANES, (n // 2 + 1) * LANES)
        k_m = jnp.where(_half_mask((2 * blk, LANES), half), k[:, ts], jnp.zeros((), BF16))
        v_t = v[:, ts]
        q_al = jnp.concatenate([q[:, (2 * n) * LANES:(2 * n + 1) * LANES],
                                q[:, (2 * n + 1) * LANES:(2 * n + 2) * LANES]], axis=0)
        q_mis = pltpu.roll(q_al, HALF, 1)
        lhs = jnp.concatenate([q_al, q_mis], axis=0).astype(BF16)
        s = lax.dot_general(lhs, k_m, (((1,), (1,)), ((), ())), preferred_element_type=F32)
        outs = []
        for b4 in range(4):
            a, j = divmod(b4, 2)
            h = 4 * n + 2 * j + (half if a == 0 else 1 - half)
            sb = s[b4 * blk:(b4 + 1) * blk]
            sb = jnp.where(valid, sb - SLOPES[h] * distf, NEG)
            outs.append(_sink_softmax_pv(sb, sink_ref[h], v_t, False))
        keep = _half_mask((blk, LANES), half)
        for j in range(2):
            tiles[2 * n + j] = jnp.where(keep, outs[j], pltpu.roll(outs[2 + j], HALF, 1))
    att = jnp.concatenate(tiles, axis=1)
    o_ref[...] = _rms(att, ga_ref[...]).astype(o_ref.dtype)


def swa_prompt(u, sinks, g_att):
    blk = WINDOW
    nb = T_P // blk
    kcol = (ATT_W + 2 * CONV_C) // KV_W
    prev = lambda i: jnp.maximum(i - 1, 0)
    return pl.pallas_call(
        _swa_prompt_body,
        grid=(nb,),
        in_specs=[pl.BlockSpec(memory_space=pltpu.SMEM),
                  pl.BlockSpec((blk, ATT_W), lambda i: (i, 0)),
                  pl.BlockSpec((blk, KV_W), lambda i: (prev(i), kcol)),
                  pl.BlockSpec((blk, KV_W), lambda i: (i, kcol)),
                  pl.BlockSpec((blk, KV_W), lambda i: (prev(i), kcol + 1)),
                  pl.BlockSpec((blk, KV_W), lambda i: (i, kcol + 1)),
                  pl.BlockSpec((1, ATT_W), lambda i: (0, 0))],
        out_specs=pl.BlockSpec((blk, ATT_W), lambda i: (i, 0)),
        out_shape=jax.ShapeDtypeStruct((T_P, ATT_W), BF16),
        compiler_params=_params(("parallel",)),
        name="swa_prompt",
    )(sinks, u, u, u, u, u, g_att.reshape(1, ATT_W))


def _swa_sample_body(sink_ref, q_ref, kn_ref, vn_ref, kc_ref, vc_ref, bias_ref, ga_ref,
                     o_ref, ko_ref, vo_ref, kall_ref, vall_ref):
    bt = q_ref.shape[0]
    wb = kc_ref.shape[1]
    kall_ref[:, 0:wb, :] = kc_ref[...]
    kall_ref[:, wb:wb + DEC_SEQ, :] = kn_ref[...]
    kall_ref[:, wb + DEC_SEQ:, :] = jnp.zeros((bt, KEYS_PAD - wb - DEC_SEQ, KV_W), F32)
    vall_ref[:, 0:wb, :] = vc_ref[...]
    vall_ref[:, wb:wb + DEC_SEQ, :] = vn_ref[...]
    vall_ref[:, wb + DEC_SEQ:, :] = jnp.zeros((bt, KEYS_PAD - wb - DEC_SEQ, KV_W), F32)
    ko_ref[...] = kall_ref[:, DEC_SEQ:DEC_SEQ + wb, :]
    vo_ref[...] = vall_ref[:, DEC_SEQ:DEC_SEQ + wb, :]

    q = q_ref[...] * (HEAD_DIM ** -0.5)
    k = kall_ref[...].astype(BF16)
    v = vall_ref[...].astype(BF16)
    tiles = [None] * (N_HEADS // 2)
    for n in range(N_KV):
        half = n % 2
        ts = slice((n // 2) * LANES, (n // 2 + 1) * LANES)
        k_m = jnp.where(_half_mask((bt, KEYS_PAD, LANES), half), k[:, :, ts], jnp.zeros((), BF16))
        v_t = v[:, :, ts]
        q_al = jnp.concatenate([q[:, :, (2 * n) * LANES:(2 * n + 1) * LANES],
                                q[:, :, (2 * n + 1) * LANES:(2 * n + 2) * LANES]], axis=1)
        q_mis = pltpu.roll(q_al, HALF, 2)
        lhs = jnp.concatenate([q_al, q_mis], axis=1).astype(BF16)
        s = jnp.einsum("bqd,bkd->bqk", lhs, k_m, preferred_element_type=F32)
        bias = bias_ref[n]
        s = jnp.where(bias > 0.5 * NEG, s + bias, NEG)
        sink = sink_ref[n]
        o = _sink_softmax_pv(s, sink, v_t, True)
        keep = _half_mask((bt, DEC_SEQ, LANES), half)
        for j in range(2):
            o_al = o[:, j * DEC_SEQ:(j + 1) * DEC_SEQ]
            o_mis = pltpu.roll(o[:, (2 + j) * DEC_SEQ:(3 + j) * DEC_SEQ], HALF, 2)
            tiles[2 * n + j] = jnp.where(keep, o_al, o_mis)
    att = jnp.concatenate(tiles, axis=2)
    o_ref[...] = _rms(att, ga_ref[...]).astype(o_ref.dtype)


def _sample_bias_and_sinks(sinks):
    wb = WINDOW
    bias = np.full((N_KV, 4 * DEC_SEQ, KEYS_PAD), NEG, np.float32)
    head = np.zeros((N_KV, 4 * DEC_SEQ), np.int32)
    for n in range(N_KV):
        half = n % 2
        for a in range(2):
            for j in range(2):
                h = 4 * n + 2 * j + (half if a == 0 else 1 - half)
                for t in range(DEC_SEQ):
                    r = (2 * a + j) * DEC_SEQ + t
                    head[n, r] = h
                    for kk in range(wb + DEC_SEQ):
                        d = t + wb - kk
                        if 0 <= d <= WINDOW:
                            bias[n, r, kk] = -SLOPES[h] * d
    sink_rows = sinks[jnp.asarray(head)][..., None]
    return jnp.asarray(bias), sink_rows


def swa_sample(q_s, k_new, v_new, cache_k, cache_v, sinks, g_att, bt):
    wb = cache_k.shape[1]
    bias, sink_rows = _sample_bias_and_sinks(sinks)
    nq = 4 * DEC_SEQ
    b3 = lambda i: (i, 0, 0)
    z3 = lambda i: (0, 0, 0)
    return pl.pallas_call(
        _swa_sample_body,
        grid=(DEC_BATCH // bt,),
        in_specs=[pl.BlockSpec((N_KV, nq, 1), z3),
                  pl.BlockSpec((bt, DEC_SEQ, ATT_W), b3),
                  pl.BlockSpec((bt, DEC_SEQ, KV_W), b3),
                  pl.BlockSpec((bt, DEC_SEQ, KV_W), b3),
                  pl.BlockSpec((bt, wb, KV_W), b3),
                  pl.BlockSpec((bt, wb, KV_W), b3),
                  pl.BlockSpec((N_KV, nq, KEYS_PAD), z3),
                  pl.BlockSpec((1, 1, ATT_W), z3)],
        out_specs=[pl.BlockSpec((bt, DEC_SEQ, ATT_W), b3),
                   pl.BlockSpec((bt, wb, KV_W), b3),
                   pl.BlockSpec((bt, wb, KV_W), b3)],
        out_shape=[jax.ShapeDtypeStruct((DEC_BATCH, DEC_SEQ, ATT_W), BF16),
                   jax.ShapeDtypeStruct((DEC_BATCH, wb, KV_W), F32),
                   jax.ShapeDtypeStruct((DEC_BATCH, wb, KV_W), F32)],
        scratch_shapes=[pltpu.VMEM((bt, KEYS_PAD, KV_W), F32),
                        pltpu.VMEM((bt, KEYS_PAD, KV_W), F32)],
        compiler_params=_params(("parallel",)),
        name="swa_sample",
    )(sink_rows, q_s, k_new, v_new, cache_k, cache_v, bias, g_att.reshape(1, 1, ATT_W))


def _conv_post(y, cb, lg, lb, gc):
    y = y + cb
    mu = jnp.mean(y, axis=-1, keepdims=True)
    yc = y - mu
    z = yc * lax.rsqrt(jnp.mean(yc * yc, axis=-1, keepdims=True) + EPS) * lg + lb
    c = z * _sigmoid(z)
    return _rms(c, gc)


def _conv_prompt_body(ap_ref, gp_ref, a_ref, g_ref, cw_ref, cb_ref, lg_ref, lb_ref, gc_ref,
                      o_ref, st_ref, ext_ref, y_ref, *, tt, tc):
    i = pl.program_id(0)
    glu_prev = ap_ref[...] * _sigmoid(gp_ref[...])
    ext_ref[0:STATE_PAD, :] = jnp.where(i > 0, glu_prev, 0.0)
    glu = a_ref[...] * _sigmoid(g_ref[...])
    ext_ref[STATE_PAD:STATE_PAD + tt, :] = glu
    st_ref[...] = glu[tt - STATE_PAD:]
    first = STATE_PAD - STATE_ROWS
    for cblk in range(CONV_C // LANES):
        cs = slice(cblk * LANES, (cblk + 1) * LANES)
        for t0 in range(0, tt, tc):
            acc = None
            for j in range(CONV_W):
                lo = t0 + first + j
                term = ext_ref[lo:lo + tc, cs] * cw_ref[j:j + 1, cs]
                acc = term if acc is None else acc + term
            y_ref[t0:t0 + tc, cs] = acc
    o_ref[...] = _conv_post(y_ref[...], cb_ref[...], lg_ref[...], lb_ref[...],
                            gc_ref[...]).astype(o_ref.dtype)


def conv_prompt(u, conv_w, conv_b, ln_g, ln_b, g_conv, tt=128, tc=64):
    nt = T_P // tt
    per = tt // STATE_PAD
    prev = lambda i: jnp.maximum(i * per - 1, 0)
    acol, gcol = ATT_W // CONV_C, ATT_W // CONV_C + 1
    vec = lambda: pl.BlockSpec((1, CONV_C), lambda i: (0, 0))
    out, state = pl.pallas_call(
        functools.partial(_conv_prompt_body, tt=tt, tc=tc),
        grid=(nt,),
        in_specs=[pl.BlockSpec((STATE_PAD, CONV_C), lambda i: (prev(i), acol)),
                  pl.BlockSpec((STATE_PAD, CONV_C), lambda i: (prev(i), gcol)),
                  pl.BlockSpec((tt, CONV_C), lambda i: (i, acol)),
                  pl.BlockSpec((tt, CONV_C), lambda i: (i, gcol)),
                  pl.BlockSpec((CONV_W, CONV_C), lambda i: (0, 0)),
                  vec(), vec(), vec(), vec()],
        out_specs=[pl.BlockSpec((tt, CONV_C), lambda i: (i, 0)),
                   pl.BlockSpec((STATE_PAD, CONV_C), lambda i: (0, 0))],
        out_shape=[jax.ShapeDtypeStruct((T_P, CONV_C), BF16),
                   jax.ShapeDtypeStruct((STATE_PAD, CONV_C), F32)],
        scratch_shapes=[pltpu.VMEM((STATE_PAD + tt, CONV_C), F32),
                        pltpu.VMEM((tt, CONV_C), F32)],
        compiler_params=_params(("arbitrary",)),
        name="conv_prompt",
    )(u, u, u, u, conv_w, conv_b.reshape(1, -1), ln_g.reshape(1, -1), ln_b.reshape(1, -1),
      g_conv.reshape(1, -1))
    return out, state[STATE_PAD - STATE_ROWS:]


def _conv_sample_body(st_ref, a_ref, g_ref, cw_ref, cb_ref, lg_ref, lb_ref, gc_ref,
                      o_ref, so_ref):
    glu = a_ref[...] * _sigmoid(g_ref[...])
    ext = jnp.concatenate([st_ref[...], glu], axis=1)
    so_ref[...] = ext[:, DEC_SEQ:]
    w = cw_ref[...]
    for t in range(DEC_SEQ):
        y = jnp.sum(ext[:, t:t + CONV_W] * w[None], axis=1)
        o_ref[t] = _conv_post(y, cb_ref[...], lg_ref[...], lb_ref[...],
                              gc_ref[...]).astype(o_ref.dtype)


def conv_sample(state, a_s, g_s, conv_w, conv_b, ln_g, ln_b, g_conv, bt):
    b3 = lambda i: (i, 0, 0)
    vec = lambda: pl.BlockSpec((1, CONV_C), lambda i: (0, 0))
    return pl.pallas_call(
        _conv_sample_body,
        grid=(DEC_BATCH // bt,),
        in_specs=[pl.BlockSpec((bt, STATE_ROWS, CONV_C), b3),
                  pl.BlockSpec((bt, DEC_SEQ, CONV_C), b3),
                  pl.BlockSpec((bt, DEC_SEQ, CONV_C), b3),
                  pl.BlockSpec((CONV_W, CONV_C), lambda i: (0, 0)),
                  vec(), vec(), vec(), vec()],
        out_specs=[pl.BlockSpec((DEC_SEQ, bt, CONV_C), lambda i: (0, i, 0)),
                   pl.BlockSpec((bt, STATE_ROWS, CONV_C), b3)],
        out_shape=[jax.ShapeDtypeStruct((DEC_SEQ, DEC_BATCH, CONV_C), BF16),
                   jax.ShapeDtypeStruct((DEC_BATCH, STATE_ROWS, CONV_C), F32)],
        compiler_params=_params(("parallel",)),
        name="conv_sample",
    )(state, a_s, g_s, conv_w, conv_b.reshape(1, -1), ln_g.reshape(1, -1), ln_b.reshape(1, -1),
      g_conv.reshape(1, -1))


def _cross_prompt_body(q_ref, k_ref, v_ref, o_ref):
    q = q_ref[...].astype(BF16)
    k = k_ref[...].astype(BF16)
    v = v_ref[...].astype(BF16)
    outs = []
    for h in range(X_HEADS):
        hs = slice(h * X_HEAD_DIM, (h + 1) * X_HEAD_DIM)
        s = lax.dot_general(q[:, hs], k[:, hs], (((1,), (1,)), ((), ())),
                            preferred_element_type=F32) * (X_HEAD_DIM ** -0.5)
        p = jnp.exp(s - jnp.max(s, axis=-1, keepdims=True))
        den = jnp.sum(p, axis=-1, keepdims=True)
        outs.append(jnp.dot(p.astype(BF16), v[:, hs], preferred_element_type=F32) / den)
    o_ref[...] = jnp.concatenate(outs, axis=1).astype(o_ref.dtype)


def cross_prompt(q, mem_k, mem_v, tq=512):
    return pl.pallas_call(
        _cross_prompt_body,
        grid=(T_P // tq,),
        in_specs=[pl.BlockSpec((tq, X_W), lambda i: (i, 0)),
                  pl.BlockSpec((N_MEM, X_W), lambda i: (0, 0)),
                  pl.BlockSpec((N_MEM, X_W), lambda i: (0, 0))],
        out_specs=pl.BlockSpec((tq, X_W), lambda i: (i, 0)),
        out_shape=jax.ShapeDtypeStruct((T_P, X_W), BF16),
        compiler_params=_params(("parallel",)),
        name="cross_prompt",
    )(q, mem_k, mem_v)


def _cross_sample_body(q_ref, k_ref, v_ref, o_ref):
    q = q_ref[...].astype(BF16)
    k = k_ref[...].astype(BF16)
    v = v_ref[...].astype(BF16)
    s = jnp.einsum("bqd,bkd->bqk", q, k, preferred_element_type=F32) * (X_HEAD_DIM ** -0.5)
    row = lax.broadcasted_iota(jnp.int32, s.shape[1:], 0)
    col = lax.broadcasted_iota(jnp.int32, s.shape[1:], 1)
    same_head = (col % X_HEADS) == (row // DEC_SEQ)
    s = jnp.where(same_head, s, NEG)
    p = jnp.exp(s - jnp.max(s, axis=-1, keepdims=True))
    den = jnp.sum(p, axis=-1, keepdims=True)
    o = jnp.einsum("bqk,bkd->bqd", p.astype(BF16), v, preferred_element_type=F32) / den
    o_ref[...] = o.astype(o_ref.dtype)


def cross_sample(q_ht, mem_k, mem_v, layer, bt=8):
    rows = X_HEADS * DEC_SEQ
    b3 = lambda i: (i, 0, 0)
    b4 = lambda i: (layer, i, 0, 0)
    return pl.pallas_call(
        _cross_sample_body,
        grid=(DEC_BATCH // bt,),
        in_specs=[pl.BlockSpec((bt, rows, X_HEAD_DIM), b3),
                  pl.BlockSpec((None, bt, N_MEM * X_HEADS, X_HEAD_DIM), b4),
                  pl.BlockSpec((None, bt, N_MEM * X_HEADS, X_HEAD_DIM), b4)],
        out_specs=pl.BlockSpec((bt, rows, X_HEAD_DIM), b3),
        out_shape=jax.ShapeDtypeStruct((DEC_BATCH, rows, X_HEAD_DIM), BF16),
        compiler_params=_params(("parallel",)),
        name="cross_sample",
    )(q_ht, mem_k, mem_v)


def _swiglu_step(x, wg_ref, wu_ref, wd_ref):
    hg = jnp.dot(x, wg_ref[...].astype(BF16), preferred_element_type=F32)
    hu = jnp.dot(x, wu_ref[...].astype(BF16), preferred_element_type=F32)
    h = (hg * _sigmoid(hg) * hu).astype(BF16)
    return jnp.dot(h, wd_ref[...].astype(BF16), preferred_element_type=F32)


def _dense_ffn_body(x_ref, a_ref, wa_ref, g_ref, wg_ref, wu_ref, wd_ref, o_ref, xn_ref):
    @pl.when(pl.program_id(1) == 0)
    def _():
        x = x_ref[...] + jnp.dot(a_ref[...], wa_ref[...], preferred_element_type=F32)
        xn_ref[...] = _rms(x, g_ref[...]).astype(BF16)
        o_ref[...] = x

    o_ref[...] += _swiglu_step(xn_ref[...], wg_ref, wu_ref, wd_ref)


def dense_ffn(x, a, wa, g, wg, wu, wd, tm, tf):
    m, d = x.shape
    ka = a.shape[1]
    dff = wg.shape[1]
    return pl.pallas_call(
        _dense_ffn_body,
        grid=(m // tm, dff // tf),
        in_specs=[pl.BlockSpec((tm, d), lambda i, f: (i, 0), pipeline_mode=pl.Buffered(1)),
                  pl.BlockSpec((tm, ka), lambda i, f: (i, 0)),
                  pl.BlockSpec((ka, d), lambda i, f: (0, 0)),
                  pl.BlockSpec((1, d), lambda i, f: (0, 0)),
                  pl.BlockSpec((d, tf), lambda i, f: (0, f)),
                  pl.BlockSpec((d, tf), lambda i, f: (0, f)),
                  pl.BlockSpec((tf, d), lambda i, f: (f, 0))],
        out_specs=pl.BlockSpec((tm, d), lambda i, f: (i, 0)),
        out_shape=jax.ShapeDtypeStruct((m, d), F32),
        scratch_shapes=[pltpu.VMEM((tm, d), BF16)],
        compiler_params=_params(("parallel", "arbitrary")),
        name="dense_ffn",
    )(x, a, wa, g.reshape(1, d), wg, wu, wd)


def _router_body(x_ref, a_ref, wa_ref, g_ref, wr_ref, x1_ref, idx_ref, gate_ref, cnt_ref,
                 carry_ref):
    @pl.when(pl.program_id(0) == 0)
    def _():
        carry_ref[...] = jnp.zeros(carry_ref.shape, F32)

    x1 = x_ref[...] + jnp.dot(a_ref[...], wa_ref[...], preferred_element_type=F32)
    x1_ref[...] = x1
    xn = _rms(x1, g_ref[...])
    logits = jnp.dot(xn, wr_ref[...], preferred_element_type=F32,
                     precision=lax.Precision.HIGHEST)
    tm = logits.shape[0]
    lane = lax.broadcasted_iota(jnp.int32, logits.shape, 1)
    lg = jnp.where(lane < N_EXPERTS, logits, -jnp.inf)
    m1 = jnp.max(lg, axis=-1, keepdims=True)
    i1 = jnp.min(jnp.where(lg == m1, lane, LANES), axis=-1, keepdims=True)
    lg2 = jnp.where(lane == i1, -jnp.inf, lg)
    m2 = jnp.max(lg2, axis=-1, keepdims=True)
    i2 = jnp.min(jnp.where(lg2 == m2, lane, LANES), axis=-1, keepdims=True)
    e = jnp.exp(m2 - m1)
    den = 1.0 + e
    gate_ref[...] = jnp.where(lane == 0, 1.0 / den, jnp.where(lane == 1, e / den, 0.0))

    chosen = (lane == i1) | (lane == i2)
    r = lax.broadcasted_iota(jnp.int32, (tm, tm), 0)
    c = lax.broadcasted_iota(jnp.int32, (tm, tm), 1)
    earlier = jnp.where(c < r, 1.0, 0.0).astype(BF16)
    before = jnp.dot(earlier, jnp.where(chosen, 1.0, 0.0).astype(BF16),
                     preferred_element_type=F32) + carry_ref[...]
    r1 = jnp.sum(jnp.where(lane == i1, before, 0.0), axis=-1, keepdims=True).astype(jnp.int32)
    r2 = jnp.sum(jnp.where(lane == i2, before, 0.0), axis=-1, keepdims=True).astype(jnp.int32)
    idx_ref[...] = jnp.where(lane == 0, i1, jnp.where(lane == 1, i2,
                             jnp.where(lane == 2, r1, jnp.where(lane == 3, r2, 0))))
    carry_ref[...] += jnp.sum(jnp.where(chosen, 1.0, 0.0), axis=0, keepdims=True)
    cnt_ref[...] = carry_ref[...]


def router(x, a, wa, g, w_router, tm=512):
    m, d = x.shape
    ka = a.shape[1]
    wr = jnp.zeros((d, LANES), F32).at[:, :N_EXPERTS].set(w_router)
    return pl.pallas_call(
        _router_body,
        grid=(m // tm,),
        in_specs=[pl.BlockSpec((tm, d), lambda i: (i, 0)),
                  pl.BlockSpec((tm, ka), lambda i: (i, 0)),
                  pl.BlockSpec((ka, d), lambda i: (0, 0)),
                  pl.BlockSpec((1, d), lambda i: (0, 0)),
                  pl.BlockSpec((d, LANES), lambda i: (0, 0))],
        out_specs=[pl.BlockSpec((tm, d), lambda i: (i, 0)),
                   pl.BlockSpec((tm, LANES), lambda i: (i, 0)),
                   pl.BlockSpec((tm, LANES), lambda i: (i, 0)),
                   pl.BlockSpec((1, LANES), lambda i: (0, 0))],
        out_shape=[jax.ShapeDtypeStruct((m, d), F32),
                   jax.ShapeDtypeStruct((m, LANES), jnp.int32),
                   jax.ShapeDtypeStruct((m, LANES), F32),
                   jax.ShapeDtypeStruct((1, LANES), F32)],
        scratch_shapes=[pltpu.VMEM((1, LANES), F32)],
        compiler_params=_params(("arbitrary",)),
        name="router",
    )(x, a, wa, g.reshape(1, d), wr)


def _row_copy(src_hbm, row, dst_ref, r, sem):
    return pltpu.make_async_copy(src_hbm.at[pl.ds(row, 1)], dst_ref.at[pl.ds(r, 1)], sem)


def _gather_norm_body(src_ref, x_hbm, g_ref, o_ref, buf_ref, sem, *, rows):
    base = pl.program_id(0) * rows

    def start(r, c):
        _row_copy(x_hbm, src_ref[base + r], buf_ref, r, sem).start()
        return c

    def wait(r, c):
        _row_copy(x_hbm, 0, buf_ref, r, sem).wait()
        return c

    lax.fori_loop(0, rows, start, 0)
    lax.fori_loop(0, rows, wait, 0)
    o_ref[...] = _rms(buf_ref[...], g_ref[...]).astype(o_ref.dtype)


def gather_norm(x, g, src_rows, rows=512):
    d = x.shape[1]
    n = src_rows.shape[0]
    grid_spec = pltpu.PrefetchScalarGridSpec(
        num_scalar_prefetch=1,
        grid=(n // rows,),
        in_specs=[pl.BlockSpec(memory_space=pl.ANY),
                  pl.BlockSpec((1, d), lambda i, s: (0, 0))],
        out_specs=pl.BlockSpec((rows, d), lambda i, s: (i, 0)),
        scratch_shapes=[pltpu.VMEM((rows, d), F32), pltpu.SemaphoreType.DMA(())],
    )
    return pl.pallas_call(
        functools.partial(_gather_norm_body, rows=rows),
        grid_spec=grid_spec,
        out_shape=jax.ShapeDtypeStruct((n, d), BF16),
        compiler_params=_params(("arbitrary",)),
        name="gather_norm",
    )(src_rows, x, g.reshape(1, d))


def _moe_ffn_body(vt_ref, ve_ref, vlive_ref, vfirst_ref, vsub_ref, x_ref, wg_ref, wu_ref, wd_ref,
                  o_ref, wgb_ref, wub_ref, wdb_ref, *, sub, n_sub):
    v = pl.program_id(0)

    @pl.when((pl.program_id(1) == 0) & (vfirst_ref[v] > 0))
    def _():
        o_ref[...] = jnp.zeros(o_ref.shape, o_ref.dtype)

    @pl.when(vlive_ref[v] > 0)
    def _():
        wgb_ref[...] = wg_ref[...].astype(BF16)
        wub_ref[...] = wu_ref[...].astype(BF16)
        wdb_ref[...] = wd_ref[...].astype(BF16)

    for j in range(n_sub):
        @pl.when(vsub_ref[v * n_sub + j] > 0)
        def _(j=j):
            rows = slice(j * sub, (j + 1) * sub)
            o_ref[rows, :] += _swiglu_step(x_ref[rows, :], wgb_ref, wub_ref, wdb_ref)


def moe_ffn(xs, vis_tile, vis_expert, vis_live, vis_first, vis_sub, wg, wu, wd, tm, sub, tf):
    n, d = xs.shape
    dff = wg.shape[2]
    nf = dff // tf
    n_vis = vis_tile.shape[0]
    fi = lambda v, f, live: jnp.where(live[v] > 0, f, nf - 1)
    grid_spec = pltpu.PrefetchScalarGridSpec(
        num_scalar_prefetch=5,
        grid=(n_vis, nf),
        in_specs=[pl.BlockSpec((tm, d), lambda v, f, vt, ve, lv, fs, sb: (vt[v], 0)),
                  pl.BlockSpec((None, d, tf), lambda v, f, vt, ve, lv, fs, sb: (ve[v], 0, fi(v, f, lv))),
                  pl.BlockSpec((None, d, tf), lambda v, f, vt, ve, lv, fs, sb: (ve[v], 0, fi(v, f, lv))),
                  pl.BlockSpec((None, tf, d), lambda v, f, vt, ve, lv, fs, sb: (ve[v], fi(v, f, lv), 0))],
        out_specs=pl.BlockSpec((tm, d), lambda v, f, vt, ve, lv, fs, sb: (vt[v], 0)),
        scratch_shapes=[pltpu.VMEM((d, tf), BF16), pltpu.VMEM((d, tf), BF16),
                        pltpu.VMEM((tf, d), BF16)],
    )
    return pl.pallas_call(
        functools.partial(_moe_ffn_body, sub=sub, n_sub=tm // sub),
        grid_spec=grid_spec,
        out_shape=jax.ShapeDtypeStruct((n, d), F32),
        compiler_params=_params(("arbitrary", "arbitrary")),
        name="moe_ffn",
    )(vis_tile, vis_expert, vis_live, vis_first, vis_sub, xs, wg, wu, wd)


def _combine_body(pos_ref, x_ref, gate_ref, y_hbm, o_ref, buf_ref, sem, *, rows, n_tok):
    base = pl.program_id(0) * rows

    def start(r, c):
        _row_copy(y_hbm, pos_ref[base + r], buf_ref.at[0], r, sem).start()
        _row_copy(y_hbm, pos_ref[n_tok + base + r], buf_ref.at[1], r, sem).start()
        return c

    def wait(r, c):
        _row_copy(y_hbm, 0, buf_ref.at[0], r, sem).wait()
        _row_copy(y_hbm, 0, buf_ref.at[1], r, sem).wait()
        return c

    lax.fori_loop(0, rows, start, 0)
    lax.fori_loop(0, rows, wait, 0)
    gates = gate_ref[...]
    o_ref[...] = x_ref[...] + (gates[:, 0:1] * buf_ref[0] + gates[:, 1:2] * buf_ref[1])


def moe_combine(x, gates, y, pos, rows=256):
    n_tok, d = x.shape
    grid_spec = pltpu.PrefetchScalarGridSpec(
        num_scalar_prefetch=1,
        grid=(n_tok // rows,),
        in_specs=[pl.BlockSpec((rows, d), lambda i, p: (i, 0)),
                  pl.BlockSpec((rows, LANES), lambda i, p: (i, 0)),
                  pl.BlockSpec(memory_space=pl.ANY)],
        out_specs=pl.BlockSpec((rows, d), lambda i, p: (i, 0)),
        scratch_shapes=[pltpu.VMEM((2, rows, d), F32), pltpu.SemaphoreType.DMA(())],
    )
    return pl.pallas_call(
        functools.partial(_combine_body, rows=rows, n_tok=n_tok),
        grid_spec=grid_spec,
        out_shape=jax.ShapeDtypeStruct((n_tok, d), F32),
        compiler_params=_params(("arbitrary",)),
        name="moe_combine",
    )(pos, x, gates, y)


def moe_layer(x, a, wa, g, w_router, wg, wu, wd, tm=1024, sub=256, tf=256):
    n_tok = x.shape[0]
    i32 = jnp.int32
    x1, idx_full, gate_full, cnt = router(x, a, wa, g, w_router)
    experts = idx_full[:, 0:2]
    ranks = idx_full[:, 2:4]
    counts = cnt[0, :N_EXPERTS].astype(i32)
    n_assign = 2 * n_tok
    n_sub = tm // sub
    n_sb = n_assign // sub + N_EXPERTS
    n_tiles = n_sb // n_sub
    n_vis = n_tiles + N_EXPERTS - 1

    sb_per = (counts + sub - 1) // sub
    sb_end = jnp.cumsum(sb_per)
    sb_start = sb_end - sb_per
    sb_used = sb_end[-1]
    sb_ids = jnp.arange(n_sb, dtype=i32)
    sb_valid = sb_ids < sb_used
    sb_expert = jnp.minimum(jnp.searchsorted(sb_end, sb_ids, side="right"), N_EXPERTS - 1).astype(i32)
    row_start = sb_start * sub

    pos = (row_start[experts] + ranks).astype(i32)
    pos = pos.T.reshape(-1)

    order = jnp.argsort(experts.reshape(-1), stable=True).astype(i32)
    sorted_start = jnp.cumsum(counts) - counts
    row = jnp.arange(n_sb * sub, dtype=i32)
    row_e = sb_expert[row // sub]
    rank = row - row_start[row_e]
    real = sb_valid[row // sub] & (rank < counts[row_e])
    sorted_pos = jnp.clip(sorted_start[row_e] + rank, 0, n_assign - 1)
    src_rows = jnp.where(real, order[sorted_pos] // 2, row % n_tok).astype(i32)

    prev_e = jnp.concatenate([jnp.full((1,), -1, i32), sb_expert[:-1]])
    new_vis = sb_valid & ((sb_ids % n_sub == 0) | (sb_expert != prev_e))
    vis_cum = jnp.cumsum(new_vis.astype(i32))
    vis_used = vis_cum[-1]
    tiles_used = (sb_used + n_sub - 1) // n_sub
    v_ids = jnp.arange(n_vis, dtype=i32)
    live = v_ids < vis_used
    first_sb = jnp.searchsorted(vis_cum, jnp.minimum(v_ids, vis_used - 1) + 1, side="left").astype(i32)
    spare_tile = tiles_used + (v_ids - vis_used)
    vis_tile = jnp.where(live, first_sb // n_sub, jnp.minimum(spare_tile, n_tiles - 1)).astype(i32)
    vis_expert = sb_expert[first_sb]
    vis_first = jnp.where(live, first_sb % n_sub == 0, spare_tile < n_tiles).astype(i32)
    sb_of = vis_tile[:, None] * n_sub + jnp.arange(n_sub, dtype=i32)[None, :]
    vis_sub = (live[:, None] & sb_valid[sb_of] & (sb_expert[sb_of] == vis_expert[:, None]))
    vis_sub = vis_sub.astype(i32).reshape(-1)

    xs = gather_norm(x1, g, src_rows)
    y = moe_ffn(xs, vis_tile, vis_expert, live.astype(i32), vis_first, vis_sub, wg, wu, wd,
                tm, sub, tf)
    return moe_combine(x1, gate_full, y, pos)


def _final_norm_body(x_ref, g_ref, o_ref):
    o_ref[...] = _rms(x_ref[...], g_ref[...])


def final_norm(x, g, first_block, n_blocks, tm=512):
    d = x.shape[1]
    return pl.pallas_call(
        _final_norm_body,
        grid=(n_blocks,),
        in_specs=[pl.BlockSpec((tm, d), lambda i: (first_block + i, 0)),
                  pl.BlockSpec((1, d), lambda i: (0, 0))],
        out_specs=pl.BlockSpec((tm, d), lambda i: (i, 0)),
        out_shape=jax.ShapeDtypeStruct((n_blocks * tm, d), F32),
        compiler_params=_params(("parallel",)),
        name="final_norm",
    )(x, g.reshape(1, d))


def kernel(x_prompt, x_sample, cache_win_k, cache_win_v, state_conv, cache_mem_k, cache_mem_v,
           mem_prompt, norm_mix, w_in, attn_sinks, att_out_norm, conv_w, conv_b, conv_ln_g,
           conv_ln_b, conv_out_norm, w_out, norm_cross, norm_mem, w_xq, w_xk, w_xv, w_xo,
           norm_ffn, w_gate, w_up, w_down, w_router, we_gate, we_up, we_down, final_norm_g):
    x = jnp.concatenate([x_prompt.reshape(T_P, D_MODEL), x_sample.reshape(T_S, D_MODEL)], axis=0)
    mem = mem_prompt.reshape(N_MEM, D_MODEL)
    o1 = ATT_W
    o2 = o1 + KV_W
    o3 = o2 + KV_W
    o4 = o3 + CONV_C
    wb = cache_win_k.shape[2]
    mem_k_rows = cache_mem_k.reshape(DEPTH, DEC_BATCH, N_MEM * X_HEADS, X_HEAD_DIM)
    mem_v_rows = cache_mem_v.reshape(DEPTH, DEC_BATCH, N_MEM * X_HEADS, X_HEAD_DIM)
    pk, pv, pc, pmk, pmv, sk, sv, sc = [], [], [], [], [], [], [], []
    for l in range(DEPTH):
        wl = w_in[l]
        w_u = jnp.concatenate([wl[:, :o1], wl[:, o3:o4], wl[:, o4:], wl[:, o1:o2], wl[:, o2:o3]],
                              axis=1).astype(BF16)
        u = rms_matmul(x, norm_mix[l], w_u, tm=512, tn=U_W // 2)
        ka, kg, kk, kv = ATT_W, ATT_W + CONV_C, ATT_W + 2 * CONV_C, ATT_W + 2 * CONV_C + KV_W

        att_p = swa_prompt(u, attn_sinks[l], att_out_norm[l])
        conv_p, state_p = conv_prompt(u, conv_w[l], conv_b[l], conv_ln_g[l], conv_ln_b[l],
                                      conv_out_norm[l])
        us = u[T_P:]
        q_s = us[:, :ATT_W].reshape(DEC_BATCH, DEC_SEQ, ATT_W)
        a_s = us[:, ka:kg].reshape(DEC_BATCH, DEC_SEQ, CONV_C)
        g_s = us[:, kg:kk].reshape(DEC_BATCH, DEC_SEQ, CONV_C)
        k_s = us[:, kk:kv].reshape(DEC_BATCH, DEC_SEQ, KV_W)
        v_s = us[:, kv:].reshape(DEC_BATCH, DEC_SEQ, KV_W)
        att_s, k_win, v_win = swa_sample(q_s, k_s, v_s,
                                         cache_win_k[l].reshape(DEC_BATCH, wb, KV_W),
                                         cache_win_v[l].reshape(DEC_BATCH, wb, KV_W),
                                         attn_sinks[l], att_out_norm[l], bt=16)
        conv_s, state_s = conv_sample(state_conv[l], a_s, g_s, conv_w[l], conv_b[l],
                                      conv_ln_g[l], conv_ln_b[l], conv_out_norm[l], bt=16)
        conv_s = conv_s.transpose(1, 0, 2).reshape(T_S, CONV_C)
        att = jnp.concatenate([att_p, att_s.reshape(T_S, ATT_W)], axis=0)
        cnv = jnp.concatenate([conv_p, conv_s], axis=0)
        x = matmul_add(x, [att, cnv], w_out[l].astype(BF16), tm=512, tn=1024)

        keep = min(WINDOW, SEQ)
        pk.append(u[T_P - keep:T_P, kk:kv].reshape(1, keep, N_KV, HEAD_DIM))
        pv.append(u[T_P - keep:T_P, kv:].reshape(1, keep, N_KV, HEAD_DIM))
        pc.append(state_p.reshape(1, STATE_ROWS, CONV_C))
        sk.append(k_win.reshape(DEC_BATCH, wb, N_KV, HEAD_DIM))
        sv.append(v_win.reshape(DEC_BATCH, wb, N_KV, HEAD_DIM))
        sc.append(state_s)

        mk = rms_matmul(mem, norm_mem[l], w_xk[l].astype(BF16), tm=N_MEM, tn=X_W)
        mv = rms_matmul(mem, norm_mem[l], w_xv[l].astype(BF16), tm=N_MEM, tn=X_W)
        pmk.append(mk.reshape(1, N_MEM, X_HEADS, X_HEAD_DIM))
        pmv.append(mv.reshape(1, N_MEM, X_HEADS, X_HEAD_DIM))
        qx = rms_matmul(x, norm_cross[l], w_xq[l].astype(BF16), tm=512, tn=X_W)
        o_p = cross_prompt(qx, mk, mv)
        q_ht = qx[T_P:].reshape(DEC_BATCH, DEC_SEQ, X_HEADS, X_HEAD_DIM).transpose(0, 2, 1, 3)
        o_s = cross_sample(q_ht.reshape(DEC_BATCH, X_HEADS * DEC_SEQ, X_HEAD_DIM), mem_k_rows,
                           mem_v_rows, l)
        o_s = o_s.reshape(DEC_BATCH, X_HEADS, DEC_SEQ, X_HEAD_DIM).transpose(0, 2, 1, 3)
        o_all = jnp.concatenate([o_p, o_s.reshape(T_S, X_W)], axis=0)
        wxo = w_xo[l].astype(BF16)

        if l % 2 == 0:
            d = l // 2
            x = dense_ffn(x, o_all, wxo, norm_ffn[l], w_gate[d], w_up[d], w_down[d], tm=1088, tf=256)
        else:
            m = l // 2
            x = moe_layer(x, o_all, wxo, norm_ffn[l], w_router[m], we_gate[m], we_up[m], we_down[m])

    y_prompt = final_norm(x, final_norm_g, 0, T_P // 512).reshape(1, SEQ, D_MODEL)
    y_sample = final_norm(x, final_norm_g, T_P // 512, T_S // 512).reshape(DEC_BATCH, DEC_SEQ, D_MODEL)
    return (y_prompt, y_sample, jnp.stack(pk), jnp.stack(pv), jnp.stack(pc), jnp.stack(pmk),
            jnp.stack(pmv), jnp.stack(sk), jnp.stack(sv), jnp.stack(sc))
```

```python
import functools

import jax
import jax.numpy as jnp
import numpy as np
from jax import lax
from jax.experimental import pallas as pl
from jax.experimental.pallas import tpu as pltpu

F32 = jnp.float32
BF16 = jnp.bfloat16

D_MODEL = 2048
SEQ = 8192
DEPTH = 2
DEC_BATCH = 128
DEC_SEQ = 4
HEAD_DIM = 64
ATT_W = 1024
N_HEADS = 16
N_KV = 4
KV_W = N_KV * HEAD_DIM
WINDOW = 128
CONV_C = 1024
CONV_W = 31
N_MEM = 256
X_HEADS = 4
X_HEAD_DIM = 128
X_W = X_HEADS * X_HEAD_DIM
D_FF = 5632
N_EXPERTS = 8
D_FF_E = 7168
EPS = 1e-6
NEG = -1e30

T_P = SEQ
T_S = DEC_BATCH * DEC_SEQ
T_ALL = T_P + T_S

LANES = 128
SUBLANES = 8
HALF = HEAD_DIM
STATE_ROWS = CONV_W - 1
STATE_PAD = 32
KEYS_PAD = 256
VMEM_LIMIT = 56 * 1024 * 1024

SLOPES = [float(2.0 ** (-8.0 * (h + 1) / N_HEADS)) for h in range(N_HEADS)]

U_W = ATT_W + 2 * CONV_C + 2 * KV_W


def _params(sem):
    return pltpu.CompilerParams(dimension_semantics=sem, vmem_limit_bytes=VMEM_LIMIT)


def _rms(x, g):
    r = lax.rsqrt(jnp.mean(x * x, axis=-1, keepdims=True) + EPS)
    return x * r * g


def _sigmoid(x):
    return 1.0 / (1.0 + jnp.exp(-x))


def _rms_matmul_body(x_ref, g_ref, w_ref, o_ref, xn_ref):
    @pl.when(pl.program_id(1) == 0)
    def _():
        xn_ref[...] = _rms(x_ref[...], g_ref[...]).astype(BF16)

    o_ref[...] = jnp.dot(xn_ref[...], w_ref[...], preferred_element_type=F32).astype(o_ref.dtype)


def rms_matmul(x, g, w, tm, tn, out_dtype=F32):
    m, k = x.shape
    n = w.shape[1]
    return pl.pallas_call(
        _rms_matmul_body,
        grid=(m // tm, n // tn),
        in_specs=[pl.BlockSpec((tm, k), lambda i, j: (i, 0)),
                  pl.BlockSpec((1, k), lambda i, j: (0, 0)),
                  pl.BlockSpec((k, tn), lambda i, j: (0, j))],
        out_specs=pl.BlockSpec((tm, tn), lambda i, j: (i, j)),
        out_shape=jax.ShapeDtypeStruct((m, n), out_dtype),
        scratch_shapes=[pltpu.VMEM((tm, k), BF16)],
        compiler_params=_params(("parallel", "arbitrary")),
        name="rms_matmul",
    )(x, g.reshape(1, k), w)


def _matmul_add_body(*refs, n_terms):
    x_ref = refs[0]
    o_ref = refs[-1]
    acc = x_ref[...]
    for t in range(n_terms):
        acc = acc + jnp.dot(refs[1 + t][...], refs[1 + n_terms + t][...],
                            preferred_element_type=F32)
    o_ref[...] = acc


def matmul_add(x, a_list, w, tm, tn):
    m, n = x.shape
    n_terms = len(a_list)
    in_specs = [pl.BlockSpec((tm, tn), lambda i, j: (i, j))]
    for a in a_list:
        in_specs.append(pl.BlockSpec((tm, a.shape[1]), lambda i, j: (i, 0)))
    row = 0
    for a in a_list:
        ka = a.shape[1]
        blk = row // ka
        in_specs.append(pl.BlockSpec((ka, tn), lambda i, j, blk=blk: (blk, j)))
        row += ka
    return pl.pallas_call(
        functools.partial(_matmul_add_body, n_terms=n_terms),
        grid=(m // tm, n // tn),
        in_specs=in_specs,
        out_specs=pl.BlockSpec((tm, tn), lambda i, j: (i, j)),
        out_shape=jax.ShapeDtypeStruct((m, n), F32),
        input_output_aliases={0: 0},
        compiler_params=_params(("parallel", "parallel")),
        name="matmul_add",
    )(x, *a_list, *([w] * n_terms))


def _half_mask(shape, half):
    lane = lax.broadcasted_iota(jnp.int32, shape, len(shape) - 1)
    return (lane >= HALF) if half else (lane < HALF)


def _sink_softmax_pv(s, sink, v_tile, batched):
    m = jnp.maximum(jnp.max(s, axis=-1, keepdims=True), sink)
    p = jnp.exp(s - m)
    den = jnp.sum(p, axis=-1, keepdims=True) + jnp.exp(sink - m)
    if batched:
        o = jnp.einsum("bqk,bkd->bqd", p.astype(BF16), v_tile, preferred_element_type=F32)
    else:
        o = jnp.dot(p.astype(BF16), v_tile, preferred_element_type=F32)
    return o / den


def _swa_prompt_body(sink_ref, q_ref, kp_ref, kc_ref, vp_ref, vc_ref, ga_ref, o_ref):
    i = pl.program_id(0)
    blk = WINDOW
    q = q_ref[...] * (HEAD_DIM ** -0.5)
    k = jnp.concatenate([kp_ref[...], kc_ref[...]], axis=0).astype(BF16)
    v = jnp.concatenate([vp_ref[...], vc_ref[...]], axis=0).astype(BF16)
    row = lax.broadcasted_iota(jnp.int32, (blk, 2 * blk), 0)
    col = lax.broadcasted_iota(jnp.int32, (blk, 2 * blk), 1)
    dist = blk + row - col
    valid = (dist >= 0) & (dist <= WINDOW) & ((col >= blk) | (i > 0))
    distf = dist.astype(F32)
    tiles = [None] * (N_HEADS // 2)
    for n in range(N_KV):
        half = n % 2
        ts = slice((n // 2) * LANES, (n // 2 + 1) * LANES)
        k_m = jnp.where(_half_mask((2 * blk, LANES), half), k[:, ts], jnp.zeros((), BF16))
        v_t = v[:, ts]
        q_al = jnp.concatenate([q[:, (2 * n) * LANES:(2 * n + 1) * LANES],
                                q[:, (2 * n + 1) * LANES:(2 * n + 2) * LANES]], axis=0)
        q_mis = pltpu.roll(q_al, HALF, 1)
        lhs = jnp.concatenate([q_al, q_mis], axis=0).astype(BF16)
        s = lax.dot_general(lhs, k_m, (((1,), (1,)), ((), ())), preferred_element_type=F32)
        outs = []
        for b4 in range(4):
            a, j = divmod(b4, 2)
            h = 4 * n + 2 * j + (half if a == 0 else 1 - half)
            sb = s[b4 * blk:(b4 + 1) * blk]
            sb = jnp.where(valid, sb - SLOPES[h] * distf, NEG)
            outs.append(_sink_softmax_pv(sb, sink_ref[h], v_t, False))
        keep = _half_mask((blk, LANES), half)
        for j in range(2):
            tiles[2 * n + j] = jnp.where(keep, outs[j], pltpu.roll(outs[2 + j], HALF, 1))
    att = jnp.concatenate(tiles, axis=1)
    o_ref[...] = _rms(att, ga_ref[...]).astype(o_ref.dtype)


def swa_prompt(u, sinks, g_att):
    blk = WINDOW
    nb = T_P // blk
    kcol = (ATT_W + 2 * CONV_C) // KV_W
    prev = lambda i: jnp.maximum(i - 1, 0)
    return pl.pallas_call(
        _swa_prompt_body,
        grid=(nb,),
        in_specs=[pl.BlockSpec(memory_space=pltpu.SMEM),
                  pl.BlockSpec((blk, ATT_W), lambda i: (i, 0)),
                  pl.BlockSpec((blk, KV_W), lambda i: (prev(i), kcol)),
                  pl.BlockSpec((blk, KV_W), lambda i: (i, kcol)),
                  pl.BlockSpec((blk, KV_W), lambda i: (prev(i), kcol + 1)),
                  pl.BlockSpec((blk, KV_W), lambda i: (i, kcol + 1)),
                  pl.BlockSpec((1, ATT_W), lambda i: (0, 0))],
        out_specs=pl.BlockSpec((blk, ATT_W), lambda i: (i, 0)),
        out_shape=jax.ShapeDtypeStruct((T_P, ATT_W), BF16),
        compiler_params=_params(("parallel",)),
        name="swa_prompt",
    )(sinks, u, u, u, u, u, g_att.reshape(1, ATT_W))


def _swa_sample_body(sink_ref, q_ref, kn_ref, vn_ref, kc_ref, vc_ref, bias_ref, ga_ref,
                     o_ref, ko_ref, vo_ref, kall_ref, vall_ref):
    bt = q_ref.shape[0]
    wb = kc_ref.shape[1]
    kall_ref[:, 0:wb, :] = kc_ref[...]
    kall_ref[:, wb:wb + DEC_SEQ, :] = kn_ref[...]
    kall_ref[:, wb + DEC_SEQ:, :] = jnp.zeros((bt, KEYS_PAD - wb - DEC_SEQ, KV_W), F32)
    vall_ref[:, 0:wb, :] = vc_ref[...]
    vall_ref[:, wb:wb + DEC_SEQ, :] = vn_ref[...]
    vall_ref[:, wb + DEC_SEQ:, :] = jnp.zeros((bt, KEYS_PAD - wb - DEC_SEQ, KV_W), F32)
    ko_ref[...] = kall_ref[:, DEC_SEQ:DEC_SEQ + wb, :]
    vo_ref[...] = vall_ref[:, DEC_SEQ:DEC_SEQ + wb, :]

    q = q_ref[...] * (HEAD_DIM ** -0.5)
    k = kall_ref[...].astype(BF16)
    v = vall_ref[...].astype(BF16)
    tiles = [None] * (N_HEADS // 2)
    for n in range(N_KV):
        half = n % 2
        ts = slice((n // 2) * LANES, (n // 2 + 1) * LANES)
        k_m = jnp.where(_half_mask((bt, KEYS_PAD, LANES), half), k[:, :, ts], jnp.zeros((), BF16))
        v_t = v[:, :, ts]
        q_al = jnp.concatenate([q[:, :, (2 * n) * LANES:(2 * n + 1) * LANES],
                                q[:, :, (2 * n + 1) * LANES:(2 * n + 2) * LANES]], axis=1)
        q_mis = pltpu.roll(q_al, HALF, 2)
        lhs = jnp.concatenate([q_al, q_mis], axis=1).astype(BF16)
        s = jnp.einsum("bqd,bkd->bqk", lhs, k_m, preferred_element_type=F32)
        bias = bias_ref[n]
        s = jnp.where(bias > 0.5 * NEG, s + bias, NEG)
        sink = sink_ref[n]
        o = _sink_softmax_pv(s, sink, v_t, True)
        keep = _half_mask((bt, DEC_SEQ, LANES), half)
        for j in range(2):
            o_al = o[:, j * DEC_SEQ:(j + 1) * DEC_SEQ]
            o_mis = pltpu.roll(o[:, (2 + j) * DEC_SEQ:(3 + j) * DEC_SEQ], HALF, 2)
            tiles[2 * n + j] = jnp.where(keep, o_al, o_mis)
    att = jnp.concatenate(tiles, axis=2)
    o_ref[...] = _rms(att, ga_ref[...]).astype(o_ref.dtype)


def _sample_bias_and_sinks(sinks):
    wb = WINDOW
    bias = np.full((N_KV, 4 * DEC_SEQ, KEYS_PAD), NEG, np.float32)
    head = np.zeros((N_KV, 4 * DEC_SEQ), np.int32)
    for n in range(N_KV):
        half = n % 2
        for a in range(2):
            for j in range(2):
                h = 4 * n + 2 * j + (half if a == 0 else 1 - half)
                for t in range(DEC_SEQ):
                    r = (2 * a + j) * DEC_SEQ + t
                    head[n, r] = h
                    for kk in range(wb + DEC_SEQ):
                        d = t + wb - kk
                        if 0 <= d <= WINDOW:
                            bias[n, r, kk] = -SLOPES[h] * d
    sink_rows = sinks[jnp.asarray(head)][..., None]
    return jnp.asarray(bias), sink_rows


def swa_sample(q_s, k_new, v_new, cache_k, cache_v, sinks, g_att, bt):
    wb = cache_k.shape[1]
    bias, sink_rows = _sample_bias_and_sinks(sinks)
    nq = 4 * DEC_SEQ
    b3 = lambda i: (i, 0, 0)
    z3 = lambda i: (0, 0, 0)
    return pl.pallas_call(
        _swa_sample_body,
        grid=(DEC_BATCH // bt,),
        in_specs=[pl.BlockSpec((N_KV, nq, 1), z3),
                  pl.BlockSpec((bt, DEC_SEQ, ATT_W), b3),
                  pl.BlockSpec((bt, DEC_SEQ, KV_W), b3),
                  pl.BlockSpec((bt, DEC_SEQ, KV_W), b3),
                  pl.BlockSpec((bt, wb, KV_W), b3),
                  pl.BlockSpec((bt, wb, KV_W), b3),
                  pl.BlockSpec((N_KV, nq, KEYS_PAD), z3),
                  pl.BlockSpec((1, 1, ATT_W), z3)],
        out_specs=[pl.BlockSpec((bt, DEC_SEQ, ATT_W), b3),
                   pl.BlockSpec((bt, wb, KV_W), b3),
                   pl.BlockSpec((bt, wb, KV_W), b3)],
        out_shape=[jax.ShapeDtypeStruct((DEC_BATCH, DEC_SEQ, ATT_W), BF16),
                   jax.ShapeDtypeStruct((DEC_BATCH, wb, KV_W), F32),
                   jax.ShapeDtypeStruct((DEC_BATCH, wb, KV_W), F32)],
        scratch_shapes=[pltpu.VMEM((bt, KEYS_PAD, KV_W), F32),
                        pltpu.VMEM((bt, KEYS_PAD, KV_W), F32)],
        compiler_params=_params(("parallel",)),
        name="swa_sample",
    )(sink_rows, q_s, k_new, v_new, cache_k, cache_v, bias, g_att.reshape(1, 1, ATT_W))


def _conv_post(y, cb, lg, lb, gc):
    y = y + cb
    mu = jnp.mean(y, axis=-1, keepdims=True)
    yc = y - mu
    z = yc * lax.rsqrt(jnp.mean(yc * yc, axis=-1, keepdims=True) + EPS) * lg + lb
    c = z * _sigmoid(z)
    return _rms(c, gc)


def _conv_prompt_body(ap_ref, gp_ref, a_ref, g_ref, cw_ref, cb_ref, lg_ref, lb_ref, gc_ref,
                      o_ref, st_ref, ext_ref, y_ref, *, tt, tc):
    i = pl.program_id(0)
    glu_prev = ap_ref[...] * _sigmoid(gp_ref[...])
    ext_ref[0:STATE_PAD, :] = jnp.where(i > 0, glu_prev, 0.0)
    glu = a_ref[...] * _sigmoid(g_ref[...])
    ext_ref[STATE_PAD:STATE_PAD + tt, :] = glu
    st_ref[...] = glu[tt - STATE_PAD:]
    first = STATE_PAD - STATE_ROWS
    for cblk in range(CONV_C // LANES):
        cs = slice(cblk * LANES, (cblk + 1) * LANES)
        for t0 in range(0, tt, tc):
            acc = None
            for lo in range(SUBLANES):
                n_hi = (CONV_W - lo + SUBLANES - 1) // SUBLANES
                start = t0 + first + lo
                win = ext_ref[start:start + tc + SUBLANES * (n_hi - 1), cs]
                for hi in range(n_hi):
                    j = SUBLANES * hi + lo
                    term = win[SUBLANES * hi:SUBLANES * hi + tc] * cw_ref[j:j + 1, cs]
                    acc = term if acc is None else acc + term
            y_ref[t0:t0 + tc, cs] = acc
    o_ref[...] = _conv_post(y_ref[...], cb_ref[...], lg_ref[...], lb_ref[...],
                            gc_ref[...]).astype(o_ref.dtype)


def conv_prompt(u, conv_w, conv_b, ln_g, ln_b, g_conv, tt=128, tc=64):
    nt = T_P // tt
    per = tt // STATE_PAD
    prev = lambda i: jnp.maximum(i * per - 1, 0)
    acol, gcol = ATT_W // CONV_C, ATT_W // CONV_C + 1
    vec = lambda: pl.BlockSpec((1, CONV_C), lambda i: (0, 0))
    out, state = pl.pallas_call(
        functools.partial(_conv_prompt_body, tt=tt, tc=tc),
        grid=(nt,),
        in_specs=[pl.BlockSpec((STATE_PAD, CONV_C), lambda i: (prev(i), acol)),
                  pl.BlockSpec((STATE_PAD, CONV_C), lambda i: (prev(i), gcol)),
                  pl.BlockSpec((tt, CONV_C), lambda i: (i, acol)),
                  pl.BlockSpec((tt, CONV_C), lambda i: (i, gcol)),
                  pl.BlockSpec((CONV_W, CONV_C), lambda i: (0, 0)),
                  vec(), vec(), vec(), vec()],
        out_specs=[pl.BlockSpec((tt, CONV_C), lambda i: (i, 0)),
                   pl.BlockSpec((STATE_PAD, CONV_C), lambda i: (0, 0))],
        out_shape=[jax.ShapeDtypeStruct((T_P, CONV_C), BF16),
                   jax.ShapeDtypeStruct((STATE_PAD, CONV_C), F32)],
        scratch_shapes=[pltpu.VMEM((STATE_PAD + tt, CONV_C), F32),
                        pltpu.VMEM((tt, CONV_C), F32)],
        compiler_params=_params(("arbitrary",)),
        name="conv_prompt",
    )(u, u, u, u, conv_w, conv_b.reshape(1, -1), ln_g.reshape(1, -1), ln_b.reshape(1, -1),
      g_conv.reshape(1, -1))
    return out, state[STATE_PAD - STATE_ROWS:]


def _conv_sample_body(st_ref, a_ref, g_ref, cw_ref, cb_ref, lg_ref, lb_ref, gc_ref,
                      o_ref, so_ref):
    glu = a_ref[...] * _sigmoid(g_ref[...])
    ext = jnp.concatenate([st_ref[...], glu], axis=1)
    so_ref[...] = ext[:, DEC_SEQ:]
    w = cw_ref[...]
    for t in range(DEC_SEQ):
        y = jnp.sum(ext[:, t:t + CONV_W] * w[None], axis=1)
        o_ref[t] = _conv_post(y, cb_ref[...], lg_ref[...], lb_ref[...],
                              gc_ref[...]).astype(o_ref.dtype)


def conv_sample(state, a_s, g_s, conv_w, conv_b, ln_g, ln_b, g_conv, bt):
    b3 = lambda i: (i, 0, 0)
    vec = lambda: pl.BlockSpec((1, CONV_C), lambda i: (0, 0))
    return pl.pallas_call(
        _conv_sample_body,
        grid=(DEC_BATCH // bt,),
        in_specs=[pl.BlockSpec((bt, STATE_ROWS, CONV_C), b3),
                  pl.BlockSpec((bt, DEC_SEQ, CONV_C), b3),
                  pl.BlockSpec((bt, DEC_SEQ, CONV_C), b3),
                  pl.BlockSpec((CONV_W, CONV_C), lambda i: (0, 0)),
                  vec(), vec(), vec(), vec()],
        out_specs=[pl.BlockSpec((DEC_SEQ, bt, CONV_C), lambda i: (0, i, 0)),
                   pl.BlockSpec((bt, STATE_ROWS, CONV_C), b3)],
        out_shape=[jax.ShapeDtypeStruct((DEC_SEQ, DEC_BATCH, CONV_C), BF16),
                   jax.ShapeDtypeStruct((DEC_BATCH, STATE_ROWS, CONV_C), F32)],
        compiler_params=_params(("parallel",)),
        name="conv_sample",
    )(state, a_s, g_s, conv_w, conv_b.reshape(1, -1), ln_g.reshape(1, -1), ln_b.reshape(1, -1),
      g_conv.reshape(1, -1))


def _cross_prompt_body(q_ref, k_ref, v_ref, o_ref):
    q = q_ref[...].astype(BF16)
    k = k_ref[...].astype(BF16)
    v = v_ref[...].astype(BF16)
    outs = []
    for h in range(X_HEADS):
        hs = slice(h * X_HEAD_DIM, (h + 1) * X_HEAD_DIM)
        s = lax.dot_general(q[:, hs], k[:, hs], (((1,), (1,)), ((), ())),
                            preferred_element_type=F32) * (X_HEAD_DIM ** -0.5)
        p = jnp.exp(s - jnp.max(s, axis=-1, keepdims=True))
        den = jnp.sum(p, axis=-1, keepdims=True)
        outs.append(jnp.dot(p.astype(BF16), v[:, hs], preferred_element_type=F32) / den)
    o_ref[...] = jnp.concatenate(outs, axis=1).astype(o_ref.dtype)


def cross_prompt(q, mem_k, mem_v, tq=512):
    return pl.pallas_call(
        _cross_prompt_body,
        grid=(T_P // tq,),
        in_specs=[pl.BlockSpec((tq, X_W), lambda i: (i, 0)),
                  pl.BlockSpec((N_MEM, X_W), lambda i: (0, 0)),
                  pl.BlockSpec((N_MEM, X_W), lambda i: (0, 0))],
        out_specs=pl.BlockSpec((tq, X_W), lambda i: (i, 0)),
        out_shape=jax.ShapeDtypeStruct((T_P, X_W), BF16),
        compiler_params=_params(("parallel",)),
        name="cross_prompt",
    )(q, mem_k, mem_v)


def _cross_sample_body(q_ref, k_ref, v_ref, o_ref):
    q = q_ref[...].astype(BF16)
    k = k_ref[...].astype(BF16)
    v = v_ref[...].astype(BF16)
    s = jnp.einsum("bqd,bkd->bqk", q, k, preferred_element_type=F32) * (X_HEAD_DIM ** -0.5)
    row = lax.broadcasted_iota(jnp.int32, s.shape[1:], 0)
    col = lax.broadcasted_iota(jnp.int32, s.shape[1:], 1)
    same_head = (col % X_HEADS) == (row // DEC_SEQ)
    s = jnp.where(same_head, s, NEG)
    p = jnp.exp(s - jnp.max(s, axis=-1, keepdims=True))
    den = jnp.sum(p, axis=-1, keepdims=True)
    o = jnp.einsum("bqk,bkd->bqd", p.astype(BF16), v, preferred_element_type=F32) / den
    o_ref[...] = o.astype(o_ref.dtype)


def cross_sample(q_ht, mem_k, mem_v, layer, bt=8):
    rows = X_HEADS * DEC_SEQ
    b3 = lambda i: (i, 0, 0)
    b4 = lambda i: (layer, i, 0, 0)
    return pl.pallas_call(
        _cross_sample_body,
        grid=(DEC_BATCH // bt,),
        in_specs=[pl.BlockSpec((bt, rows, X_HEAD_DIM), b3),
                  pl.BlockSpec((None, bt, N_MEM * X_HEADS, X_HEAD_DIM), b4),
                  pl.BlockSpec((None, bt, N_MEM * X_HEADS, X_HEAD_DIM), b4)],
        out_specs=pl.BlockSpec((bt, rows, X_HEAD_DIM), b3),
        out_shape=jax.ShapeDtypeStruct((DEC_BATCH, rows, X_HEAD_DIM), BF16),
        compiler_params=_params(("parallel",)),
        name="cross_sample",
    )(q_ht, mem_k, mem_v)


def _swiglu_step(x, wg_ref, wu_ref, wd_ref):
    hg = jnp.dot(x, wg_ref[...].astype(BF16), preferred_element_type=F32)
    hu = jnp.dot(x, wu_ref[...].astype(BF16), preferred_element_type=F32)
    h = (hg * _sigmoid(hg) * hu).astype(BF16)
    return jnp.dot(h, wd_ref[...].astype(BF16), preferred_element_type=F32)


def _dense_ffn_body(x_ref, a_ref, wa_ref, g_ref, wg_ref, wu_ref, wd_ref, o_ref, xn_ref):
    @pl.when(pl.program_id(1) == 0)
    def _():
        x = x_ref[...] + jnp.dot(a_ref[...], wa_ref[...], preferred_element_type=F32)
        xn_ref[...] = _rms(x, g_ref[...]).astype(BF16)
        o_ref[...] = x

    o_ref[...] += _swiglu_step(xn_ref[...], wg_ref, wu_ref, wd_ref)


def dense_ffn(x, a, wa, g, wg, wu, wd, tm, tf):
    m, d = x.shape
    ka = a.shape[1]
    dff = wg.shape[1]
    return pl.pallas_call(
        _dense_ffn_body,
        grid=(m // tm, dff // tf),
        in_specs=[pl.BlockSpec((tm, d), lambda i, f: (i, 0), pipeline_mode=pl.Buffered(1)),
                  pl.BlockSpec((tm, ka), lambda i, f: (i, 0)),
                  pl.BlockSpec((ka, d), lambda i, f: (0, 0)),
                  pl.BlockSpec((1, d), lambda i, f: (0, 0)),
                  pl.BlockSpec((d, tf), lambda i, f: (0, f)),
                  pl.BlockSpec((d, tf), lambda i, f: (0, f)),
                  pl.BlockSpec((tf, d), lambda i, f: (f, 0))],
        out_specs=pl.BlockSpec((tm, d), lambda i, f: (i, 0)),
        out_shape=jax.ShapeDtypeStruct((m, d), F32),
        scratch_shapes=[pltpu.VMEM((tm, d), BF16)],
        compiler_params=_params(("parallel", "arbitrary")),
        name="dense_ffn",
    )(x, a, wa, g.reshape(1, d), wg, wu, wd)


def _router_body(x_ref, a_ref, wa_ref, g_ref, wr_ref, x1_ref, idx_ref, gate_ref, cnt_ref,
                 carry_ref):
    @pl.when(pl.program_id(0) == 0)
    def _():
        carry_ref[...] = jnp.zeros(carry_ref.shape, F32)

    x1 = x_ref[...] + jnp.dot(a_ref[...], wa_ref[...], preferred_element_type=F32)
    x1_ref[...] = x1
    xn = _rms(x1, g_ref[...])
    logits = jnp.dot(xn, wr_ref[...], preferred_element_type=F32,
                     precision=lax.Precision.HIGHEST)
    tm = logits.shape[0]
    lane = lax.broadcasted_iota(jnp.int32, logits.shape, 1)
    lg = jnp.where(lane < N_EXPERTS, logits, -jnp.inf)
    m1 = jnp.max(lg, axis=-1, keepdims=True)
    i1 = jnp.min(jnp.where(lg == m1, lane, LANES), axis=-1, keepdims=True)
    lg2 = jnp.where(lane == i1, -jnp.inf, lg)
    m2 = jnp.max(lg2, axis=-1, keepdims=True)
    i2 = jnp.min(jnp.where(lg2 == m2, lane, LANES), axis=-1, keepdims=True)
    e = jnp.exp(m2 - m1)
    den = 1.0 + e
    gate_ref[...] = jnp.where(lane == 0, 1.0 / den, jnp.where(lane == 1, e / den, 0.0))

    chosen = (lane == i1) | (lane == i2)
    r = lax.broadcasted_iota(jnp.int32, (tm, tm), 0)
    c = lax.broadcasted_iota(jnp.int32, (tm, tm), 1)
    earlier = jnp.where(c < r, 1.0, 0.0).astype(BF16)
    before = jnp.dot(earlier, jnp.where(chosen, 1.0, 0.0).astype(BF16),
                     preferred_element_type=F32) + carry_ref[...]
    r1 = jnp.sum(jnp.where(lane == i1, before, 0.0), axis=-1, keepdims=True).astype(jnp.int32)
    r2 = jnp.sum(jnp.where(lane == i2, before, 0.0), axis=-1, keepdims=True).astype(jnp.int32)
    idx_ref[...] = jnp.where(lane == 0, i1, jnp.where(lane == 1, i2,
                             jnp.where(lane == 2, r1, jnp.where(lane == 3, r2, 0))))
    carry_ref[...] += jnp.sum(jnp.where(chosen, 1.0, 0.0), axis=0, keepdims=True)
    cnt_ref[...] = carry_ref[...]


def router(x, a, wa, g, w_router, tm=512):
    m, d = x.shape
    ka = a.shape[1]
    wr = jnp.zeros((d, LANES), F32).at[:, :N_EXPERTS].set(w_router)
    return pl.pallas_call(
        _router_body,
        grid=(m // tm,),
        in_specs=[pl.BlockSpec((tm, d), lambda i: (i, 0)),
                  pl.BlockSpec((tm, ka), lambda i: (i, 0)),
                  pl.BlockSpec((ka, d), lambda i: (0, 0)),
                  pl.BlockSpec((1, d), lambda i: (0, 0)),
                  pl.BlockSpec((d, LANES), lambda i: (0, 0))],
        out_specs=[pl.BlockSpec((tm, d), lambda i: (i, 0)),
                   pl.BlockSpec((tm, LANES), lambda i: (i, 0)),
                   pl.BlockSpec((tm, LANES), lambda i: (i, 0)),
                   pl.BlockSpec((1, LANES), lambda i: (0, 0))],
        out_shape=[jax.ShapeDtypeStruct((m, d), F32),
                   jax.ShapeDtypeStruct((m, LANES), jnp.int32),
                   jax.ShapeDtypeStruct((m, LANES), F32),
                   jax.ShapeDtypeStruct((1, LANES), F32)],
        scratch_shapes=[pltpu.VMEM((1, LANES), F32)],
        compiler_params=_params(("arbitrary",)),
        name="router",
    )(x, a, wa, g.reshape(1, d), wr)


def _row_copy(src_hbm, row, dst_ref, r, sem):
    return pltpu.make_async_copy(src_hbm.at[pl.ds(row, 1)], dst_ref.at[pl.ds(r, 1)], sem)


def _gather_norm_body(src_ref, x_hbm, g_ref, o_ref, buf_ref, sem, *, rows):
    base = pl.program_id(0) * rows

    def start(r, c):
        _row_copy(x_hbm, src_ref[base + r], buf_ref, r, sem).start()
        return c

    def wait(r, c):
        _row_copy(x_hbm, 0, buf_ref, r, sem).wait()
        return c

    lax.fori_loop(0, rows, start, 0, unroll=8)
    lax.fori_loop(0, rows, wait, 0, unroll=8)
    o_ref[...] = _rms(buf_ref[...], g_ref[...]).astype(o_ref.dtype)


def gather_norm(x, g, src_rows, rows=512):
    d = x.shape[1]
    n = src_rows.shape[0]
    grid_spec = pltpu.PrefetchScalarGridSpec(
        num_scalar_prefetch=1,
        grid=(n // rows,),
        in_specs=[pl.BlockSpec(memory_space=pl.ANY),
                  pl.BlockSpec((1, d), lambda i, s: (0, 0))],
        out_specs=pl.BlockSpec((rows, d), lambda i, s: (i, 0)),
        scratch_shapes=[pltpu.VMEM((rows, d), F32), pltpu.SemaphoreType.DMA(())],
    )
    return pl.pallas_call(
        functools.partial(_gather_norm_body, rows=rows),
        grid_spec=grid_spec,
        out_shape=jax.ShapeDtypeStruct((n, d), BF16),
        compiler_params=_params(("arbitrary",)),
        name="gather_norm",
    )(src_rows, x, g.reshape(1, d))


def _moe_ffn_body(vt_ref, ve_ref, vlive_ref, vfirst_ref, vlo_ref, vcnt_ref, x_ref, wg_ref, wu_ref,
                  wd_ref, o_ref, wgb_ref, wub_ref, wdb_ref, *, sub, n_sub):
    v = pl.program_id(0)

    @pl.when((pl.program_id(1) == 0) & (vfirst_ref[v] > 0))
    def _():
        o_ref[...] = jnp.zeros(o_ref.shape, o_ref.dtype)

    @pl.when(vlive_ref[v] > 0)
    def _():
        wgb_ref[...] = wg_ref[...].astype(BF16)
        wub_ref[...] = wu_ref[...].astype(BF16)
        wdb_ref[...] = wd_ref[...].astype(BF16)

    lo = pl.multiple_of(vlo_ref[v] * sub, sub)
    for n in range(1, n_sub + 1):
        @pl.when(vcnt_ref[v] == n)
        def _(n=n):
            rows = pl.ds(lo, n * sub)
            o_ref[rows, :] += _swiglu_step(x_ref[rows, :], wgb_ref, wub_ref, wdb_ref)


def moe_ffn(xs, vis_tile, vis_expert, vis_live, vis_first, vis_lo, vis_cnt, wg, wu, wd, tm, sub, tf):
    n, d = xs.shape
    dff = wg.shape[2]
    nf = dff // tf
    n_vis = vis_tile.shape[0]
    fi = lambda v, f, live: jnp.where(live[v] > 0, f, nf - 1)
    grid_spec = pltpu.PrefetchScalarGridSpec(
        num_scalar_prefetch=6,
        grid=(n_vis, nf),
        in_specs=[pl.BlockSpec((tm, d), lambda v, f, vt, ve, lv, *_: (vt[v], 0)),
                  pl.BlockSpec((None, d, tf), lambda v, f, vt, ve, lv, *_: (ve[v], 0, fi(v, f, lv))),
                  pl.BlockSpec((None, d, tf), lambda v, f, vt, ve, lv, *_: (ve[v], 0, fi(v, f, lv))),
                  pl.BlockSpec((None, tf, d), lambda v, f, vt, ve, lv, *_: (ve[v], fi(v, f, lv), 0))],
        out_specs=pl.BlockSpec((tm, d), lambda v, f, vt, ve, lv, *_: (vt[v], 0)),
        scratch_shapes=[pltpu.VMEM((d, tf), BF16), pltpu.VMEM((d, tf), BF16),
                        pltpu.VMEM((tf, d), BF16)],
    )
    return pl.pallas_call(
        functools.partial(_moe_ffn_body, sub=sub, n_sub=tm // sub),
        grid_spec=grid_spec,
        out_shape=jax.ShapeDtypeStruct((n, d), F32),
        compiler_params=_params(("arbitrary", "arbitrary")),
        name="moe_ffn",
    )(vis_tile, vis_expert, vis_live, vis_first, vis_lo, vis_cnt, xs, wg, wu, wd)


def _combine_body(pos_ref, x_ref, gate_ref, y_hbm, o_ref, buf_ref, sem, *, rows, n_tok):
    base = pl.program_id(0) * rows

    def start(r, c):
        _row_copy(y_hbm, pos_ref[base + r], buf_ref.at[0], r, sem).start()
        _row_copy(y_hbm, pos_ref[n_tok + base + r], buf_ref.at[1], r, sem).start()
        return c

    def wait(r, c):
        _row_copy(y_hbm, 0, buf_ref.at[0], r, sem).wait()
        _row_copy(y_hbm, 0, buf_ref.at[1], r, sem).wait()
        return c

    lax.fori_loop(0, rows, start, 0, unroll=8)
    lax.fori_loop(0, rows, wait, 0, unroll=8)
    gates = gate_ref[...]
    o_ref[...] = x_ref[...] + (gates[:, 0:1] * buf_ref[0] + gates[:, 1:2] * buf_ref[1])


def moe_combine(x, gates, y, pos, rows=256):
    n_tok, d = x.shape
    grid_spec = pltpu.PrefetchScalarGridSpec(
        num_scalar_prefetch=1,
        grid=(n_tok // rows,),
        in_specs=[pl.BlockSpec((rows, d), lambda i, p: (i, 0)),
                  pl.BlockSpec((rows, LANES), lambda i, p: (i, 0)),
                  pl.BlockSpec(memory_space=pl.ANY)],
        out_specs=pl.BlockSpec((rows, d), lambda i, p: (i, 0)),
        scratch_shapes=[pltpu.VMEM((2, rows, d), F32), pltpu.SemaphoreType.DMA(())],
    )
    return pl.pallas_call(
        functools.partial(_combine_body, rows=rows, n_tok=n_tok),
        grid_spec=grid_spec,
        out_shape=jax.ShapeDtypeStruct((n_tok, d), F32),
        compiler_params=_params(("arbitrary",)),
        name="moe_combine",
    )(pos, x, gates, y)


def moe_layer(x, a, wa, g, w_router, wg, wu, wd, tm=1024, sub=256, tf=256):
    n_tok = x.shape[0]
    i32 = jnp.int32
    x1, idx_full, gate_full, cnt = router(x, a, wa, g, w_router)
    experts = idx_full[:, 0:2]
    ranks = idx_full[:, 2:4]
    counts = cnt[0, :N_EXPERTS].astype(i32)
    n_assign = 2 * n_tok
    n_sub = tm // sub
    n_sb = n_assign // sub + N_EXPERTS
    n_tiles = n_sb // n_sub
    n_vis = n_tiles + N_EXPERTS - 1

    e_ids = jnp.arange(N_EXPERTS, dtype=i32)
    sb_per = (counts + sub - 1) // sub
    sb_end = jnp.cumsum(sb_per)
    sb_start = sb_end - sb_per
    sb_used = sb_end[-1]
    sb_ids = jnp.arange(n_sb, dtype=i32)
    sb_valid = sb_ids < sb_used
    sb_expert = jnp.minimum(jnp.sum((sb_ids[:, None] >= sb_end[None, :]).astype(i32), axis=1),
                            N_EXPERTS - 1)
    row_start = sb_start * sub

    start_of = jnp.sum(jnp.where(experts[:, :, None] == e_ids, row_start, 0), axis=-1)
    pos = (start_of + ranks).astype(i32).T.reshape(-1)

    n_pad = n_sb * sub - n_assign
    pad_end = jnp.cumsum(sb_per * sub - counts)
    pad_ids = jnp.arange(n_pad, dtype=i32)
    pad_e = jnp.sum((pad_ids[:, None] >= pad_end[None, :]).astype(i32), axis=1)
    keys = jnp.concatenate([experts.reshape(-1) * 2, pad_e * 2 + 1])
    toks = jnp.concatenate([jnp.arange(n_assign, dtype=i32) // 2, (pad_ids * 8) % n_tok])
    _, src_rows = lax.sort((keys, toks), num_keys=1, is_stable=True)

    prev_e = jnp.concatenate([jnp.full((1,), -1, i32), sb_expert[:-1]])
    new_vis = sb_valid & ((sb_ids % n_sub == 0) | (sb_expert != prev_e))
    vis_cum = jnp.cumsum(new_vis.astype(i32))
    vis_used = vis_cum[-1]
    tiles_used = (sb_used + n_sub - 1) // n_sub
    v_ids = jnp.arange(n_vis, dtype=i32)
    live = v_ids < vis_used
    v_eff = jnp.minimum(v_ids, vis_used - 1)
    first_sb = jnp.sum((vis_cum[None, :] <= v_eff[:, None]).astype(i32), axis=1)
    is_first = sb_ids[None, :] == first_sb[:, None]
    vis_expert = jnp.sum(jnp.where(is_first, sb_expert[None, :], 0), axis=1)
    spare_tile = tiles_used + (v_ids - vis_used)
    vis_tile = jnp.where(live, first_sb // n_sub, jnp.minimum(spare_tile, n_tiles - 1)).astype(i32)
    vis_first = jnp.where(live, first_sb % n_sub == 0, spare_tile < n_tiles).astype(i32)
    in_run = ((sb_ids[None, :] // n_sub == vis_tile[:, None]) & sb_valid[None, :]
              & (sb_expert[None, :] == vis_expert[:, None]) & live[:, None])
    vis_cnt = jnp.sum(in_run.astype(i32), axis=1)
    vis_lo = first_sb % n_sub

    xs = gather_norm(x1, g, src_rows.astype(i32))
    y = moe_ffn(xs, vis_tile, vis_expert.astype(i32), live.astype(i32), vis_first,
                vis_lo.astype(i32), vis_cnt, wg, wu, wd, tm, sub, tf)
    return moe_combine(x1, gate_full, y, pos)


def _final_norm_body(x_ref, g_ref, o_ref):
    o_ref[...] = _rms(x_ref[...], g_ref[...])


def final_norm(x, g, first_block, n_blocks, tm=512):
    d = x.shape[1]
    return pl.pallas_call(
        _final_norm_body,
        grid=(n_blocks,),
        in_specs=[pl.BlockSpec((tm, d), lambda i: (first_block + i, 0)),
                  pl.BlockSpec((1, d), lambda i: (0, 0))],
        out_specs=pl.BlockSpec((tm, d), lambda i: (i, 0)),
        out_shape=jax.ShapeDtypeStruct((n_blocks * tm, d), F32),
        compiler_params=_params(("parallel",)),
        name="final_norm",
    )(x, g.reshape(1, d))


def kernel(x_prompt, x_sample, cache_win_k, cache_win_v, state_conv, cache_mem_k, cache_mem_v,
           mem_prompt, norm_mix, w_in, attn_sinks, att_out_norm, conv_w, conv_b, conv_ln_g,
           conv_ln_b, conv_out_norm, w_out, norm_cross, norm_mem, w_xq, w_xk, w_xv, w_xo,
           norm_ffn, w_gate, w_up, w_down, w_router, we_gate, we_up, we_down, final_norm_g):
    x = jnp.concatenate([x_prompt.reshape(T_P, D_MODEL), x_sample.reshape(T_S, D_MODEL)], axis=0)
    mem = mem_prompt.reshape(N_MEM, D_MODEL)
    o1 = ATT_W
    o2 = o1 + KV_W
    o3 = o2 + KV_W
    o4 = o3 + CONV_C
    wb = cache_win_k.shape[2]
    mem_k_rows = cache_mem_k.reshape(DEPTH, DEC_BATCH, N_MEM * X_HEADS, X_HEAD_DIM)
    mem_v_rows = cache_mem_v.reshape(DEPTH, DEC_BATCH, N_MEM * X_HEADS, X_HEAD_DIM)
    pk, pv, pc, pmk, pmv, sk, sv, sc = [], [], [], [], [], [], [], []
    for l in range(DEPTH):
        wl = w_in[l]
        w_u = jnp.concatenate([wl[:, :o1], wl[:, o3:o4], wl[:, o4:], wl[:, o1:o2], wl[:, o2:o3]],
                              axis=1).astype(BF16)
        u = rms_matmul(x, norm_mix[l], w_u, tm=512, tn=U_W // 2)
        ka, kg, kk, kv = ATT_W, ATT_W + CONV_C, ATT_W + 2 * CONV_C, ATT_W + 2 * CONV_C + KV_W

        att_p = swa_prompt(u, attn_sinks[l], att_out_norm[l])
        conv_p, state_p = conv_prompt(u, conv_w[l], conv_b[l], conv_ln_g[l], conv_ln_b[l],
                                      conv_out_norm[l])
        us = u[T_P:]
        q_s = us[:, :ATT_W].reshape(DEC_BATCH, DEC_SEQ, ATT_W)
        a_s = us[:, ka:kg].reshape(DEC_BATCH, DEC_SEQ, CONV_C)
        g_s = us[:, kg:kk].reshape(DEC_BATCH, DEC_SEQ, CONV_C)
        k_s = us[:, kk:kv].reshape(DEC_BATCH, DEC_SEQ, KV_W)
        v_s = us[:, kv:].reshape(DEC_BATCH, DEC_SEQ, KV_W)
        att_s, k_win, v_win = swa_sample(q_s, k_s, v_s,
                                         cache_win_k[l].reshape(DEC_BATCH, wb, KV_W),
                                         cache_win_v[l].reshape(DEC_BATCH, wb, KV_W),
                                         attn_sinks[l], att_out_norm[l], bt=16)
        conv_s, state_s = conv_sample(state_conv[l], a_s, g_s, conv_w[l], conv_b[l],
                                      conv_ln_g[l], conv_ln_b[l], conv_out_norm[l], bt=16)
        conv_s = conv_s.transpose(1, 0, 2).reshape(T_S, CONV_C)
        att = jnp.concatenate([att_p, att_s.reshape(T_S, ATT_W)], axis=0)
        cnv = jnp.concatenate([conv_p, conv_s], axis=0)
        x = matmul_add(x, [att, cnv], w_out[l].astype(BF16), tm=512, tn=1024)

        keep = min(WINDOW, SEQ)
        pk.append(u[T_P - keep:T_P, kk:kv].reshape(1, keep, N_KV, HEAD_DIM))
        pv.append(u[T_P - keep:T_P, kv:].reshape(1, keep, N_KV, HEAD_DIM))
        pc.append(state_p.reshape(1, STATE_ROWS, CONV_C))
        sk.append(k_win.reshape(DEC_BATCH, wb, N_KV, HEAD_DIM))
        sv.append(v_win.reshape(DEC_BATCH, wb, N_KV, HEAD_DIM))
        sc.append(state_s)

        mk = rms_matmul(mem, norm_mem[l], w_xk[l].astype(BF16), tm=N_MEM, tn=X_W)
        mv = rms_matmul(mem, norm_mem[l], w_xv[l].astype(BF16), tm=N_MEM, tn=X_W)
        pmk.append(mk.reshape(1, N_MEM, X_HEADS, X_HEAD_DIM))
        pmv.append(mv.reshape(1, N_MEM, X_HEADS, X_HEAD_DIM))
        qx = rms_matmul(x, norm_cross[l], w_xq[l].astype(BF16), tm=512, tn=X_W)
        o_p = cross_prompt(qx, mk, mv)
        q_ht = qx[T_P:].reshape(DEC_BATCH, DEC_SEQ, X_HEADS, X_HEAD_DIM).transpose(0, 2, 1, 3)
        o_s = cross_sample(q_ht.reshape(DEC_BATCH, X_HEADS * DEC_SEQ, X_HEAD_DIM), mem_k_rows,
                           mem_v_rows, l)
        o_s = o_s.reshape(DEC_BATCH, X_HEADS, DEC_SEQ, X_HEAD_DIM).transpose(0, 2, 1, 3)
        o_all = jnp.concatenate([o_p, o_s.reshape(T_S, X_W)], axis=0)
        wxo = w_xo[l].astype(BF16)

        if l % 2 == 0:
            d = l // 2
            x = dense_ffn(x, o_all, wxo, norm_ffn[l], w_gate[d], w_up[d], w_down[d], tm=1088, tf=256)
        else:
            m = l // 2
            x = moe_layer(x, o_all, wxo, norm_ffn[l], w_router[m], we_gate[m], we_up[m], we_down[m])

    y_prompt = final_norm(x, final_norm_g, 0, T_P // 512).reshape(1, SEQ, D_MODEL)
    y_sample = final_norm(x, final_norm_g, T_P // 512, T_S // 512).reshape(DEC_BATCH, DEC_SEQ, D_MODEL)
    return (y_prompt, y_sample, jnp.stack(pk), jnp.stack(pv), jnp.stack(pc), jnp.stack(pmk),
            jnp.stack(pmv), jnp.stack(sk), jnp.stack(sv), jnp.stack(sc))
```

```python
import functools

import jax
import jax.numpy as jnp
import numpy as np
from jax import lax
from jax.experimental import pallas as pl
from jax.experimental.pallas import tpu as pltpu

F32 = jnp.float32
BF16 = jnp.bfloat16

D_MODEL = 2048
SEQ = 8192
DEPTH = 2
DEC_BATCH = 128
DEC_SEQ = 4
HEAD_DIM = 64
ATT_W = 1024
N_HEADS = 16
N_KV = 4
KV_W = N_KV * HEAD_DIM
WINDOW = 128
CONV_C = 1024
CONV_W = 31
N_MEM = 256
X_HEADS = 4
X_HEAD_DIM = 128
X_W = X_HEADS * X_HEAD_DIM
D_FF = 5632
N_EXPERTS = 8
D_FF_E = 7168
EPS = 1e-6
NEG = -1e30

T_P = SEQ
T_S = DEC_BATCH * DEC_SEQ
T_ALL = T_P + T_S

LANES = 128
SUBLANES = 8
HALF = HEAD_DIM
STATE_ROWS = CONV_W - 1
STATE_PAD = 32
KEYS_PAD = 256
VMEM_LIMIT = 56 * 1024 * 1024

SLOPES = [float(2.0 ** (-8.0 * (h + 1) / N_HEADS)) for h in range(N_HEADS)]

U_W = ATT_W + 2 * CONV_C + 2 * KV_W


def _params(sem):
    return pltpu.CompilerParams(dimension_semantics=sem, vmem_limit_bytes=VMEM_LIMIT)


def _rms(x, g):
    r = lax.rsqrt(jnp.mean(x * x, axis=-1, keepdims=True) + EPS)
    return x * r * g


def _sigmoid(x):
    return 1.0 / (1.0 + jnp.exp(-x))


def _parts(x):
    return x if isinstance(x, tuple) else (x,)


def _n_rows(parts):
    return sum(p.shape[0] for p in parts)


def _row_specs(parts, tm, width, col):
    if len(parts) == 1:
        return [pl.BlockSpec((tm, width), lambda i, j: (i, col(j)))]
    n_p = parts[0].shape[0] // tm
    return [pl.BlockSpec((tm, width), lambda i, j: (jnp.minimum(i, n_p - 1), col(j))),
            pl.BlockSpec((tm, width), lambda i, j: (jnp.maximum(i - n_p, 0), col(j)))]


def _row_value(refs, n_p):
    if len(refs) == 1:
        return refs[0][...]
    return jnp.where(pl.program_id(0) < n_p, refs[0][...], refs[1][...])


def _rms_matmul_body(*refs, n_x, n_p):
    x_refs = refs[:n_x]
    g_ref, w_ref, o_ref, xn_ref = refs[n_x:]

    @pl.when(pl.program_id(1) == 0)
    def _():
        xn_ref[...] = _rms(_row_value(x_refs, n_p), g_ref[...]).astype(BF16)

    o_ref[...] = jnp.dot(xn_ref[...], w_ref[...], preferred_element_type=F32).astype(o_ref.dtype)


def rms_matmul(x, g, w, tm, tn, out_dtype=F32):
    parts = _parts(x)
    m, k = _n_rows(parts), parts[0].shape[1]
    n = w.shape[1]
    return pl.pallas_call(
        functools.partial(_rms_matmul_body, n_x=len(parts), n_p=parts[0].shape[0] // tm),
        grid=(m // tm, n // tn),
        in_specs=_row_specs(parts, tm, k, lambda j: 0) + [
            pl.BlockSpec((1, k), lambda i, j: (0, 0)),
            pl.BlockSpec((k, tn), lambda i, j: (0, j))],
        out_specs=pl.BlockSpec((tm, tn), lambda i, j: (i, j)),
        out_shape=jax.ShapeDtypeStruct((m, n), out_dtype),
        scratch_shapes=[pltpu.VMEM((tm, k), BF16)],
        compiler_params=_params(("parallel", "arbitrary")),
        name="rms_matmul",
    )(*parts, g.reshape(1, k), w)


def _matmul_add_body(*refs, counts, n_p):
    o_ref = refs[-1]
    n_terms = len(counts) - 1
    w_refs = refs[len(refs) - 1 - n_terms:len(refs) - 1]
    at = counts[0]
    acc = _row_value(refs[:at], n_p)
    for t in range(n_terms):
        a = _row_value(refs[at:at + counts[1 + t]], n_p)
        at += counts[1 + t]
        acc = acc + jnp.dot(a, w_refs[t][...], preferred_element_type=F32)
    o_ref[...] = acc


def matmul_add(x, a_list, w, tm, tn):
    x_parts = _parts(x)
    a_parts = [_parts(a) for a in a_list]
    m, n = _n_rows(x_parts), x_parts[0].shape[1]
    in_specs = _row_specs(x_parts, tm, tn, lambda j: j)
    operands = list(x_parts)
    for parts in a_parts:
        in_specs += _row_specs(parts, tm, parts[0].shape[1], lambda j: 0)
        operands += list(parts)
    row = 0
    for parts in a_parts:
        ka = parts[0].shape[1]
        in_specs.append(pl.BlockSpec((ka, tn), lambda i, j, blk=row // ka: (blk, j)))
        operands.append(w)
        row += ka
    counts = (len(x_parts),) + tuple(len(p) for p in a_parts)
    return pl.pallas_call(
        functools.partial(_matmul_add_body, counts=counts, n_p=T_P // tm),
        grid=(m // tm, n // tn),
        in_specs=in_specs,
        out_specs=pl.BlockSpec((tm, tn), lambda i, j: (i, j)),
        out_shape=jax.ShapeDtypeStruct((m, n), F32),
        input_output_aliases={0: 0} if len(x_parts) == 1 else {},
        compiler_params=_params(("parallel", "parallel")),
        name="matmul_add",
    )(*operands)


def _half_mask(shape, half):
    lane = lax.broadcasted_iota(jnp.int32, shape, len(shape) - 1)
    return (lane >= HALF) if half else (lane < HALF)


def _sink_softmax_pv(s, sink, v_tile, batched):
    m = jnp.maximum(jnp.max(s, axis=-1, keepdims=True), sink)
    p = jnp.exp(s - m)
    den = jnp.sum(p, axis=-1, keepdims=True) + jnp.exp(sink - m)
    if batched:
        o = jnp.einsum("bqk,bkd->bqd", p.astype(BF16), v_tile, preferred_element_type=F32)
    else:
        o = jnp.dot(p.astype(BF16), v_tile, preferred_element_type=F32)
    return o / den


def _swa_prompt_body(sink_ref, q_ref, kp_ref, kc_ref, vp_ref, vc_ref, ga_ref, o_ref):
    i = pl.program_id(0)
    blk = WINDOW
    q = q_ref[...] * (HEAD_DIM ** -0.5)
    k = jnp.concatenate([kp_ref[...], kc_ref[...]], axis=0).astype(BF16)
    v = jnp.concatenate([vp_ref[...], vc_ref[...]], axis=0).astype(BF16)
    row = lax.broadcasted_iota(jnp.int32, (blk, 2 * blk), 0)
    col = lax.broadcasted_iota(jnp.int32, (blk, 2 * blk), 1)
    dist = blk + row - col
    valid = (dist >= 0) & (dist <= WINDOW) & ((col >= blk) | (i > 0))
    distf = dist.astype(F32)
    tiles = [None] * (N_HEADS // 2)
    for n in range(N_KV):
        half = n % 2
        ts = slice((n // 2) * LANES, (n // 2 + 1) * LANES)
        k_m = jnp.where(_half_mask((2 * blk, LANES), half), k[:, ts], jnp.zeros((), BF16))
        v_t = v[:, ts]
        q_al = jnp.concatenate([q[:, (2 * n) * LANES:(2 * n + 1) * LANES],
                                q[:, (2 * n + 1) * LANES:(2 * n + 2) * LANES]], axis=0)
        q_mis = pltpu.roll(q_al, HALF, 1)
        lhs = jnp.concatenate([q_al, q_mis], axis=0).astype(BF16)
        s = lax.dot_general(lhs, k_m, (((1,), (1,)), ((), ())), preferred_element_type=F32)
        outs = []
        for b4 in range(4):
            a, j = divmod(b4, 2)
            h = 4 * n + 2 * j + (half if a == 0 else 1 - half)
            sb = s[b4 * blk:(b4 + 1) * blk]
            sb = jnp.where(valid, sb - SLOPES[h] * distf, NEG)
            outs.append(_sink_softmax_pv(sb, sink_ref[h], v_t, False))
        keep = _half_mask((blk, LANES), half)
        for j in range(2):
            tiles[2 * n + j] = jnp.where(keep, outs[j], pltpu.roll(outs[2 + j], HALF, 1))
    att = jnp.concatenate(tiles, axis=1)
    o_ref[...] = _rms(att, ga_ref[...]).astype(o_ref.dtype)


def swa_prompt(u, sinks, g_att):
    blk = WINDOW
    nb = T_P // blk
    kcol = (ATT_W + 2 * CONV_C) // KV_W
    prev = lambda i: jnp.maximum(i - 1, 0)
    return pl.pallas_call(
        _swa_prompt_body,
        grid=(nb,),
        in_specs=[pl.BlockSpec(memory_space=pltpu.SMEM),
                  pl.BlockSpec((blk, ATT_W), lambda i: (i, 0)),
                  pl.BlockSpec((blk, KV_W), lambda i: (prev(i), kcol)),
                  pl.BlockSpec((blk, KV_W), lambda i: (i, kcol)),
                  pl.BlockSpec((blk, KV_W), lambda i: (prev(i), kcol + 1)),
                  pl.BlockSpec((blk, KV_W), lambda i: (i, kcol + 1)),
                  pl.BlockSpec((1, ATT_W), lambda i: (0, 0))],
        out_specs=pl.BlockSpec((blk, ATT_W), lambda i: (i, 0)),
        out_shape=jax.ShapeDtypeStruct((T_P, ATT_W), BF16),
        compiler_params=_params(("parallel",)),
        name="swa_prompt",
    )(sinks, u, u, u, u, u, g_att.reshape(1, ATT_W))


def _swa_sample_body(sink_ref, q_ref, kn_ref, vn_ref, kc_ref, vc_ref, bias_ref, ga_ref,
                     o_ref, ko_ref, vo_ref, kall_ref, vall_ref):
    bt = q_ref.shape[0]
    wb = kc_ref.shape[1]
    kall_ref[:, 0:wb, :] = kc_ref[...]
    kall_ref[:, wb:wb + DEC_SEQ, :] = kn_ref[...]
    kall_ref[:, wb + DEC_SEQ:, :] = jnp.zeros((bt, KEYS_PAD - wb - DEC_SEQ, KV_W), F32)
    vall_ref[:, 0:wb, :] = vc_ref[...]
    vall_ref[:, wb:wb + DEC_SEQ, :] = vn_ref[...]
    vall_ref[:, wb + DEC_SEQ:, :] = jnp.zeros((bt, KEYS_PAD - wb - DEC_SEQ, KV_W), F32)
    ko_ref[...] = kall_ref[:, DEC_SEQ:DEC_SEQ + wb, :]
    vo_ref[...] = vall_ref[:, DEC_SEQ:DEC_SEQ + wb, :]

    q = q_ref[...] * (HEAD_DIM ** -0.5)
    k = kall_ref[...].astype(BF16)
    v = vall_ref[...].astype(BF16)
    tiles = [None] * (N_HEADS // 2)
    for n in range(N_KV):
        half = n % 2
        ts = slice((n // 2) * LANES, (n // 2 + 1) * LANES)
        k_m = jnp.where(_half_mask((bt, KEYS_PAD, LANES), half), k[:, :, ts], jnp.zeros((), BF16))
        v_t = v[:, :, ts]
        q_al = jnp.concatenate([q[:, :, (2 * n) * LANES:(2 * n + 1) * LANES],
                                q[:, :, (2 * n + 1) * LANES:(2 * n + 2) * LANES]], axis=1)
        q_mis = pltpu.roll(q_al, HALF, 2)
        lhs = jnp.concatenate([q_al, q_mis], axis=1).astype(BF16)
        s = jnp.einsum("bqd,bkd->bqk", lhs, k_m, preferred_element_type=F32)
        bias = bias_ref[n]
        s = jnp.where(bias > 0.5 * NEG, s + bias, NEG)
        sink = sink_ref[n]
        o = _sink_softmax_pv(s, sink, v_t, True)
        keep = _half_mask((bt, DEC_SEQ, LANES), half)
        for j in range(2):
            o_al = o[:, j * DEC_SEQ:(j + 1) * DEC_SEQ]
            o_mis = pltpu.roll(o[:, (2 + j) * DEC_SEQ:(3 + j) * DEC_SEQ], HALF, 2)
            tiles[2 * n + j] = jnp.where(keep, o_al, o_mis)
    att = jnp.concatenate(tiles, axis=2)
    o_ref[...] = _rms(att, ga_ref[...]).astype(o_ref.dtype)


def _sample_bias_and_sinks(sinks):
    wb = WINDOW
    bias = np.full((N_KV, 4 * DEC_SEQ, KEYS_PAD), NEG, np.float32)
    head = np.zeros((N_KV, 4 * DEC_SEQ), np.int32)
    for n in range(N_KV):
        half = n % 2
        for a in range(2):
            for j in range(2):
                h = 4 * n + 2 * j + (half if a == 0 else 1 - half)
                for t in range(DEC_SEQ):
                    r = (2 * a + j) * DEC_SEQ + t
                    head[n, r] = h
                    for kk in range(wb + DEC_SEQ):
                        d = t + wb - kk
                        if 0 <= d <= WINDOW:
                            bias[n, r, kk] = -SLOPES[h] * d
    sink_rows = sinks[jnp.asarray(head)][..., None]
    return jnp.asarray(bias), sink_rows


def swa_sample(q_s, k_new, v_new, cache_k, cache_v, sinks, g_att, bt):
    wb = cache_k.shape[1]
    bias, sink_rows = _sample_bias_and_sinks(sinks)
    nq = 4 * DEC_SEQ
    b3 = lambda i: (i, 0, 0)
    z3 = lambda i: (0, 0, 0)
    return pl.pallas_call(
        _swa_sample_body,
        grid=(DEC_BATCH // bt,),
        in_specs=[pl.BlockSpec((N_KV, nq, 1), z3),
                  pl.BlockSpec((bt, DEC_SEQ, ATT_W), b3),
                  pl.BlockSpec((bt, DEC_SEQ, KV_W), b3),
                  pl.BlockSpec((bt, DEC_SEQ, KV_W), b3),
                  pl.BlockSpec((bt, wb, KV_W), b3),
                  pl.BlockSpec((bt, wb, KV_W), b3),
                  pl.BlockSpec((N_KV, nq, KEYS_PAD), z3),
                  pl.BlockSpec((1, 1, ATT_W), z3)],
        out_specs=[pl.BlockSpec((bt, DEC_SEQ, ATT_W), b3),
                   pl.BlockSpec((bt, wb, KV_W), b3),
                   pl.BlockSpec((bt, wb, KV_W), b3)],
        out_shape=[jax.ShapeDtypeStruct((DEC_BATCH, DEC_SEQ, ATT_W), BF16),
                   jax.ShapeDtypeStruct((DEC_BATCH, wb, KV_W), F32),
                   jax.ShapeDtypeStruct((DEC_BATCH, wb, KV_W), F32)],
        scratch_shapes=[pltpu.VMEM((bt, KEYS_PAD, KV_W), F32),
                        pltpu.VMEM((bt, KEYS_PAD, KV_W), F32)],
        compiler_params=_params(("parallel",)),
        name="swa_sample",
    )(sink_rows, q_s, k_new, v_new, cache_k, cache_v, bias, g_att.reshape(1, 1, ATT_W))


def _conv_post(y, cb, lg, lb, gc):
    y = y + cb
    mu = jnp.mean(y, axis=-1, keepdims=True)
    yc = y - mu
    z = yc * lax.rsqrt(jnp.mean(yc * yc, axis=-1, keepdims=True) + EPS) * lg + lb
    c = z * _sigmoid(z)
    return _rms(c, gc)


def _conv_prompt_body(ap_ref, gp_ref, a_ref, g_ref, cw_ref, cb_ref, lg_ref, lb_ref, gc_ref,
                      o_ref, st_ref, ext_ref, y_ref, *, tt, tc):
    i = pl.program_id(0)
    glu_prev = ap_ref[...] * _sigmoid(gp_ref[...])
    ext_ref[0:STATE_PAD, :] = jnp.where(i > 0, glu_prev, 0.0)
    glu = a_ref[...] * _sigmoid(g_ref[...])
    ext_ref[STATE_PAD:STATE_PAD + tt, :] = glu
    st_ref[...] = glu[tt - STATE_PAD:]
    first = STATE_PAD - STATE_ROWS
    for cblk in range(CONV_C // LANES):
        cs = slice(cblk * LANES, (cblk + 1) * LANES)
        for t0 in range(0, tt, tc):
            total = None
            for lo in range(SUBLANES):
                n_hi = (CONV_W - lo + SUBLANES - 1) // SUBLANES
                start = t0 + first + lo
                win = ext_ref[start:start + tc + SUBLANES * (n_hi - 1), cs]
                acc = None
                for hi in range(n_hi):
                    j = SUBLANES * hi + lo
                    term = win[SUBLANES * hi:SUBLANES * hi + tc] * cw_ref[j:j + 1, cs]
                    acc = term if acc is None else acc + term
                total = acc if total is None else total + acc
            y_ref[t0:t0 + tc, cs] = total
    o_ref[...] = _conv_post(y_ref[...], cb_ref[...], lg_ref[...], lb_ref[...],
                            gc_ref[...]).astype(o_ref.dtype)


def conv_prompt(u, conv_w, conv_b, ln_g, ln_b, g_conv, tt=128, tc=64):
    nt = T_P // tt
    per = tt // STATE_PAD
    prev = lambda i: jnp.maximum(i * per - 1, 0)
    acol, gcol = ATT_W // CONV_C, ATT_W // CONV_C + 1
    vec = lambda: pl.BlockSpec((1, CONV_C), lambda i: (0, 0))
    out, state = pl.pallas_call(
        functools.partial(_conv_prompt_body, tt=tt, tc=tc),
        grid=(nt,),
        in_specs=[pl.BlockSpec((STATE_PAD, CONV_C), lambda i: (prev(i), acol)),
                  pl.BlockSpec((STATE_PAD, CONV_C), lambda i: (prev(i), gcol)),
                  pl.BlockSpec((tt, CONV_C), lambda i: (i, acol)),
                  pl.BlockSpec((tt, CONV_C), lambda i: (i, gcol)),
                  pl.BlockSpec((CONV_W, CONV_C), lambda i: (0, 0)),
                  vec(), vec(), vec(), vec()],
        out_specs=[pl.BlockSpec((tt, CONV_C), lambda i: (i, 0)),
                   pl.BlockSpec((STATE_PAD, CONV_C), lambda i: (0, 0))],
        out_shape=[jax.ShapeDtypeStruct((T_P, CONV_C), BF16),
                   jax.ShapeDtypeStruct((STATE_PAD, CONV_C), F32)],
        scratch_shapes=[pltpu.VMEM((STATE_PAD + tt, CONV_C), F32),
                        pltpu.VMEM((tt, CONV_C), F32)],
        compiler_params=_params(("arbitrary",)),
        name="conv_prompt",
    )(u, u, u, u, conv_w, conv_b.reshape(1, -1), ln_g.reshape(1, -1), ln_b.reshape(1, -1),
      g_conv.reshape(1, -1))
    return out, state[STATE_PAD - STATE_ROWS:]


def _conv_sample_body(st_ref, a_ref, g_ref, cw_ref, cb_ref, lg_ref, lb_ref, gc_ref,
                      o_ref, so_ref):
    glu = a_ref[...] * _sigmoid(g_ref[...])
    ext = jnp.concatenate([st_ref[...], glu], axis=1)
    so_ref[...] = ext[:, DEC_SEQ:]
    w = cw_ref[...]
    for t in range(DEC_SEQ):
        y = jnp.sum(ext[:, t:t + CONV_W] * w[None], axis=1)
        o_ref[t] = _conv_post(y, cb_ref[...], lg_ref[...], lb_ref[...],
                              gc_ref[...]).astype(o_ref.dtype)


def conv_sample(state, a_s, g_s, conv_w, conv_b, ln_g, ln_b, g_conv, bt):
    b3 = lambda i: (i, 0, 0)
    vec = lambda: pl.BlockSpec((1, CONV_C), lambda i: (0, 0))
    return pl.pallas_call(
        _conv_sample_body,
        grid=(DEC_BATCH // bt,),
        in_specs=[pl.BlockSpec((bt, STATE_ROWS, CONV_C), b3),
                  pl.BlockSpec((bt, DEC_SEQ, CONV_C), b3),
                  pl.BlockSpec((bt, DEC_SEQ, CONV_C), b3),
                  pl.BlockSpec((CONV_W, CONV_C), lambda i: (0, 0)),
                  vec(), vec(), vec(), vec()],
        out_specs=[pl.BlockSpec((DEC_SEQ, bt, CONV_C), lambda i: (0, i, 0)),
                   pl.BlockSpec((bt, STATE_ROWS, CONV_C), b3)],
        out_shape=[jax.ShapeDtypeStruct((DEC_SEQ, DEC_BATCH, CONV_C), BF16),
                   jax.ShapeDtypeStruct((DEC_BATCH, STATE_ROWS, CONV_C), F32)],
        compiler_params=_params(("parallel",)),
        name="conv_sample",
    )(state, a_s, g_s, conv_w, conv_b.reshape(1, -1), ln_g.reshape(1, -1), ln_b.reshape(1, -1),
      g_conv.reshape(1, -1))


def _cross_prompt_body(q_ref, k_ref, v_ref, o_ref):
    q = q_ref[...].astype(BF16)
    k = k_ref[...].astype(BF16)
    v = v_ref[...].astype(BF16)
    outs = []
    for h in range(X_HEADS):
        hs = slice(h * X_HEAD_DIM, (h + 1) * X_HEAD_DIM)
        s = lax.dot_general(q[:, hs], k[:, hs], (((1,), (1,)), ((), ())),
                            preferred_element_type=F32) * (X_HEAD_DIM ** -0.5)
        p = jnp.exp(s - jnp.max(s, axis=-1, keepdims=True))
        den = jnp.sum(p, axis=-1, keepdims=True)
        outs.append(jnp.dot(p.astype(BF16), v[:, hs], preferred_element_type=F32) / den)
    o_ref[...] = jnp.concatenate(outs, axis=1).astype(o_ref.dtype)


def cross_prompt(q, mem_k, mem_v, tq=512):
    return pl.pallas_call(
        _cross_prompt_body,
        grid=(T_P // tq,),
        in_specs=[pl.BlockSpec((tq, X_W), lambda i: (i, 0)),
                  pl.BlockSpec((N_MEM, X_W), lambda i: (0, 0)),
                  pl.BlockSpec((N_MEM, X_W), lambda i: (0, 0))],
        out_specs=pl.BlockSpec((tq, X_W), lambda i: (i, 0)),
        out_shape=jax.ShapeDtypeStruct((T_P, X_W), BF16),
        compiler_params=_params(("parallel",)),
        name="cross_prompt",
    )(q, mem_k, mem_v)


def _cross_sample_body(q_ref, k_ref, v_ref, o_ref):
    q = q_ref[...].astype(BF16)
    k = k_ref[...].astype(BF16)
    v = v_ref[...].astype(BF16)
    s = jnp.einsum("bqd,bkd->bqk", q, k, preferred_element_type=F32) * (X_HEAD_DIM ** -0.5)
    row = lax.broadcasted_iota(jnp.int32, s.shape[1:], 0)
    col = lax.broadcasted_iota(jnp.int32, s.shape[1:], 1)
    same_head = (col % X_HEADS) == (row // DEC_SEQ)
    s = jnp.where(same_head, s, NEG)
    p = jnp.exp(s - jnp.max(s, axis=-1, keepdims=True))
    den = jnp.sum(p, axis=-1, keepdims=True)
    o = jnp.einsum("bqk,bkd->bqd", p.astype(BF16), v, preferred_element_type=F32) / den
    o_ref[...] = o.astype(o_ref.dtype)


def cross_sample(q_ht, mem_k, mem_v, layer, bt=8):
    rows = X_HEADS * DEC_SEQ
    b3 = lambda i: (i, 0, 0)
    b4 = lambda i: (layer, i, 0, 0)
    return pl.pallas_call(
        _cross_sample_body,
        grid=(DEC_BATCH // bt,),
        in_specs=[pl.BlockSpec((bt, rows, X_HEAD_DIM), b3),
                  pl.BlockSpec((None, bt, N_MEM * X_HEADS, X_HEAD_DIM), b4),
                  pl.BlockSpec((None, bt, N_MEM * X_HEADS, X_HEAD_DIM), b4)],
        out_specs=pl.BlockSpec((bt, rows, X_HEAD_DIM), b3),
        out_shape=jax.ShapeDtypeStruct((DEC_BATCH, rows, X_HEAD_DIM), BF16),
        compiler_params=_params(("parallel",)),
        name="cross_sample",
    )(q_ht, mem_k, mem_v)


def _swiglu_step(x, wg_ref, wu_ref, wd_ref):
    hg = jnp.dot(x, wg_ref[...].astype(BF16), preferred_element_type=F32)
    hu = jnp.dot(x, wu_ref[...].astype(BF16), preferred_element_type=F32)
    h = (hg * _sigmoid(hg) * hu).astype(BF16)
    return jnp.dot(h, wd_ref[...].astype(BF16), preferred_element_type=F32)


def _dense_ffn_body(x_ref, a_ref, wa_ref, g_ref, wg_ref, wu_ref, wd_ref, o_ref, xn_ref):
    @pl.when(pl.program_id(1) == 0)
    def _():
        x = x_ref[...] + jnp.dot(a_ref[...], wa_ref[...], preferred_element_type=F32)
        xn_ref[...] = _rms(x, g_ref[...]).astype(BF16)
        o_ref[...] = x

    o_ref[...] += _swiglu_step(xn_ref[...], wg_ref, wu_ref, wd_ref)


def dense_ffn(x, a, wa, g, wg, wu, wd, tm, tf):
    m, d = x.shape
    ka = a.shape[1]
    dff = wg.shape[1]
    return pl.pallas_call(
        _dense_ffn_body,
        grid=(m // tm, dff // tf),
        in_specs=[pl.BlockSpec((tm, d), lambda i, f: (i, 0), pipeline_mode=pl.Buffered(1)),
                  pl.BlockSpec((tm, ka), lambda i, f: (i, 0)),
                  pl.BlockSpec((ka, d), lambda i, f: (0, 0)),
                  pl.BlockSpec((1, d), lambda i, f: (0, 0)),
                  pl.BlockSpec((d, tf), lambda i, f: (0, f)),
                  pl.BlockSpec((d, tf), lambda i, f: (0, f)),
                  pl.BlockSpec((tf, d), lambda i, f: (f, 0))],
        out_specs=pl.BlockSpec((tm, d), lambda i, f: (i, 0)),
        out_shape=jax.ShapeDtypeStruct((m, d), F32),
        scratch_shapes=[pltpu.VMEM((tm, d), BF16)],
        compiler_params=_params(("parallel", "arbitrary")),
        name="dense_ffn",
    )(x, a, wa, g.reshape(1, d), wg, wu, wd)


def _router_body(x_ref, a_ref, wa_ref, g_ref, wr_ref, x1_ref, idx_ref, gate_ref, cnt_ref,
                 carry_ref):
    @pl.when(pl.program_id(0) == 0)
    def _():
        carry_ref[...] = jnp.zeros(carry_ref.shape, F32)

    x1 = x_ref[...] + jnp.dot(a_ref[...], wa_ref[...], preferred_element_type=F32)
    x1_ref[...] = x1
    xn = _rms(x1, g_ref[...])
    logits = jnp.dot(xn, wr_ref[...], preferred_element_type=F32,
                     precision=lax.Precision.HIGHEST)
    tm = logits.shape[0]
    lane = lax.broadcasted_iota(jnp.int32, logits.shape, 1)
    lg = jnp.where(lane < N_EXPERTS, logits, -jnp.inf)
    m1 = jnp.max(lg, axis=-1, keepdims=True)
    i1 = jnp.min(jnp.where(lg == m1, lane, LANES), axis=-1, keepdims=True)
    lg2 = jnp.where(lane == i1, -jnp.inf, lg)
    m2 = jnp.max(lg2, axis=-1, keepdims=True)
    i2 = jnp.min(jnp.where(lg2 == m2, lane, LANES), axis=-1, keepdims=True)
    e = jnp.exp(m2 - m1)
    den = 1.0 + e
    gate_ref[...] = jnp.where(lane == 0, 1.0 / den, jnp.where(lane == 1, e / den, 0.0))

    chosen = (lane == i1) | (lane == i2)
    r = lax.broadcasted_iota(jnp.int32, (tm, tm), 0)
    c = lax.broadcasted_iota(jnp.int32, (tm, tm), 1)
    earlier = jnp.where(c < r, 1.0, 0.0).astype(BF16)
    before = jnp.dot(earlier, jnp.where(chosen, 1.0, 0.0).astype(BF16),
                     preferred_element_type=F32) + carry_ref[...]
    r1 = jnp.sum(jnp.where(lane == i1, before, 0.0), axis=-1, keepdims=True).astype(jnp.int32)
    r2 = jnp.sum(jnp.where(lane == i2, before, 0.0), axis=-1, keepdims=True).astype(jnp.int32)
    idx_ref[...] = jnp.where(lane == 0, i1, jnp.where(lane == 1, i2,
                             jnp.where(lane == 2, r1, jnp.where(lane == 3, r2, 0))))
    carry_ref[...] += jnp.sum(jnp.where(chosen, 1.0, 0.0), axis=0, keepdims=True)
    cnt_ref[...] = carry_ref[...]


def router(x, a, wa, g, w_router, tm=512):
    m, d = x.shape
    ka = a.shape[1]
    wr = jnp.zeros((d, LANES), F32).at[:, :N_EXPERTS].set(w_router)
    return pl.pallas_call(
        _router_body,
        grid=(m // tm,),
        in_specs=[pl.BlockSpec((tm, d), lambda i: (i, 0)),
                  pl.BlockSpec((tm, ka), lambda i: (i, 0)),
                  pl.BlockSpec((ka, d), lambda i: (0, 0)),
                  pl.BlockSpec((1, d), lambda i: (0, 0)),
                  pl.BlockSpec((d, LANES), lambda i: (0, 0))],
        out_specs=[pl.BlockSpec((tm, d), lambda i: (i, 0)),
                   pl.BlockSpec((tm, LANES), lambda i: (i, 0)),
                   pl.BlockSpec((tm, LANES), lambda i: (i, 0)),
                   pl.BlockSpec((1, LANES), lambda i: (0, 0))],
        out_shape=[jax.ShapeDtypeStruct((m, d), F32),
                   jax.ShapeDtypeStruct((m, LANES), jnp.int32),
                   jax.ShapeDtypeStruct((m, LANES), F32),
                   jax.ShapeDtypeStruct((1, LANES), F32)],
        scratch_shapes=[pltpu.VMEM((1, LANES), F32)],
        compiler_params=_params(("arbitrary",)),
        name="router",
    )(x, a, wa, g.reshape(1, d), wr)


def _row_copy(src_hbm, row, dst_ref, r, sem):
    return pltpu.make_async_copy(src_hbm.at[pl.ds(row, 1)], dst_ref.at[pl.ds(r, 1)], sem)


def _gather_norm_body(src_ref, x_hbm, g_ref, o_ref, buf_ref, sem, *, rows):
    base = pl.program_id(0) * rows

    def start(r, c):
        _row_copy(x_hbm, src_ref[base + r], buf_ref, r, sem).start()
        return c

    def wait(r, c):
        _row_copy(x_hbm, 0, buf_ref, r, sem).wait()
        return c

    lax.fori_loop(0, rows, start, 0, unroll=8)
    lax.fori_loop(0, rows, wait, 0, unroll=8)
    o_ref[...] = _rms(buf_ref[...], g_ref[...]).astype(o_ref.dtype)


def gather_norm(x, g, src_rows, rows=512):
    d = x.shape[1]
    n = src_rows.shape[0]
    grid_spec = pltpu.PrefetchScalarGridSpec(
        num_scalar_prefetch=1,
        grid=(n // rows,),
        in_specs=[pl.BlockSpec(memory_space=pl.ANY),
                  pl.BlockSpec((1, d), lambda i, s: (0, 0))],
        out_specs=pl.BlockSpec((rows, d), lambda i, s: (i, 0)),
        scratch_shapes=[pltpu.VMEM((rows, d), F32), pltpu.SemaphoreType.DMA(())],
    )
    return pl.pallas_call(
        functools.partial(_gather_norm_body, rows=rows),
        grid_spec=grid_spec,
        out_shape=jax.ShapeDtypeStruct((n, d), BF16),
        compiler_params=_params(("arbitrary",)),
        name="gather_norm",
    )(src_rows, x, g.reshape(1, d))


def _moe_ffn_body(vt_ref, ve_ref, vlive_ref, vfirst_ref, vlo_ref, vcnt_ref, x_ref, wg_ref, wu_ref,
                  wd_ref, o_ref, wgb_ref, wub_ref, wdb_ref, *, sub, n_sub):
    v = pl.program_id(0)

    @pl.when((pl.program_id(1) == 0) & (vfirst_ref[v] > 0))
    def _():
        o_ref[...] = jnp.zeros(o_ref.shape, o_ref.dtype)

    @pl.when(vlive_ref[v] > 0)
    def _():
        wgb_ref[...] = wg_ref[...].astype(BF16)
        wub_ref[...] = wu_ref[...].astype(BF16)
        wdb_ref[...] = wd_ref[...].astype(BF16)

    lo = pl.multiple_of(vlo_ref[v] * sub, sub)
    for n in range(1, n_sub + 1):
        @pl.when(vcnt_ref[v] == n)
        def _(n=n):
            rows = pl.ds(lo, n * sub)
            o_ref[rows, :] += _swiglu_step(x_ref[rows, :], wgb_ref, wub_ref, wdb_ref)


def moe_ffn(xs, vis_tile, vis_expert, vis_live, vis_first, vis_lo, vis_cnt, wg, wu, wd, tm, sub, tf):
    n, d = xs.shape
    dff = wg.shape[2]
    nf = dff // tf
    n_vis = vis_tile.shape[0]
    fi = lambda v, f, live: jnp.where(live[v] > 0, f, nf - 1)
    grid_spec = pltpu.PrefetchScalarGridSpec(
        num_scalar_prefetch=6,
        grid=(n_vis, nf),
        in_specs=[pl.BlockSpec((tm, d), lambda v, f, vt, ve, lv, *_: (vt[v], 0)),
                  pl.BlockSpec((None, d, tf), lambda v, f, vt, ve, lv, *_: (ve[v], 0, fi(v, f, lv))),
                  pl.BlockSpec((None, d, tf), lambda v, f, vt, ve, lv, *_: (ve[v], 0, fi(v, f, lv))),
                  pl.BlockSpec((None, tf, d), lambda v, f, vt, ve, lv, *_: (ve[v], fi(v, f, lv), 0))],
        out_specs=pl.BlockSpec((tm, d), lambda v, f, vt, ve, lv, *_: (vt[v], 0)),
        scratch_shapes=[pltpu.VMEM((d, tf), BF16), pltpu.VMEM((d, tf), BF16),
                        pltpu.VMEM((tf, d), BF16)],
    )
    return pl.pallas_call(
        functools.partial(_moe_ffn_body, sub=sub, n_sub=tm // sub),
        grid_spec=grid_spec,
        out_shape=jax.ShapeDtypeStruct((n, d), F32),
        compiler_params=_params(("arbitrary", "arbitrary")),
        name="moe_ffn",
    )(vis_tile, vis_expert, vis_live, vis_first, vis_lo, vis_cnt, xs, wg, wu, wd)


def _combine_rows(pos_ref, x_ref, gate_ref, y_hbm, buf_ref, sem, rows, n_tok):
    base = pl.program_id(0) * rows

    def start(r, c):
        _row_copy(y_hbm, pos_ref[base + r], buf_ref.at[0], r, sem).start()
        _row_copy(y_hbm, pos_ref[n_tok + base + r], buf_ref.at[1], r, sem).start()
        return c

    def wait(r, c):
        _row_copy(y_hbm, 0, buf_ref.at[0], r, sem).wait()
        _row_copy(y_hbm, 0, buf_ref.at[1], r, sem).wait()
        return c

    lax.fori_loop(0, rows, start, 0, unroll=8)
    lax.fori_loop(0, rows, wait, 0, unroll=8)
    gates = gate_ref[...]
    return x_ref[...] + (gates[:, 0:1] * buf_ref[0] + gates[:, 1:2] * buf_ref[1])


def _combine_body(pos_ref, x_ref, gate_ref, y_hbm, o_ref, buf_ref, sem, *, rows, n_tok):
    o_ref[...] = _combine_rows(pos_ref, x_ref, gate_ref, y_hbm, buf_ref, sem, rows, n_tok)


def _combine_norm_body(pos_ref, x_ref, gate_ref, g_ref, y_hbm, op_ref, os_ref, buf_ref, sem, *,
                       rows, n_tok, n_p):
    out = _rms(_combine_rows(pos_ref, x_ref, gate_ref, y_hbm, buf_ref, sem, rows, n_tok),
               g_ref[...])

    @pl.when(pl.program_id(0) < n_p)
    def _():
        op_ref[...] = out

    @pl.when(pl.program_id(0) >= n_p)
    def _():
        os_ref[...] = out


def moe_combine(x, gates, y, pos, out_norm=None, rows=256):
    n_tok, d = x.shape
    in_specs = [pl.BlockSpec((rows, d), lambda i, p: (i, 0)),
                pl.BlockSpec((rows, LANES), lambda i, p: (i, 0))]
    scratch = [pltpu.VMEM((2, rows, d), F32), pltpu.SemaphoreType.DMA(())]
    any_spec = pl.BlockSpec(memory_space=pl.ANY)
    if out_norm is None:
        return pl.pallas_call(
            functools.partial(_combine_body, rows=rows, n_tok=n_tok),
            grid_spec=pltpu.PrefetchScalarGridSpec(
                num_scalar_prefetch=1, grid=(n_tok // rows,),
                in_specs=in_specs + [any_spec],
                out_specs=pl.BlockSpec((rows, d), lambda i, p: (i, 0)),
                scratch_shapes=scratch),
            out_shape=jax.ShapeDtypeStruct((n_tok, d), F32),
            compiler_params=_params(("arbitrary",)),
            name="moe_combine",
        )(pos, x, gates, y)
    n_p = T_P // rows
    return pl.pallas_call(
        functools.partial(_combine_norm_body, rows=rows, n_tok=n_tok, n_p=n_p),
        grid_spec=pltpu.PrefetchScalarGridSpec(
            num_scalar_prefetch=1, grid=(n_tok // rows,),
            in_specs=in_specs + [pl.BlockSpec((1, d), lambda i, p: (0, 0)), any_spec],
            out_specs=[pl.BlockSpec((rows, d), lambda i, p: (jnp.minimum(i, n_p - 1), 0)),
                       pl.BlockSpec((rows, d), lambda i, p: (jnp.maximum(i - n_p, 0), 0))],
            scratch_shapes=scratch),
        out_shape=[jax.ShapeDtypeStruct((T_P, d), F32), jax.ShapeDtypeStruct((n_tok - T_P, d), F32)],
        compiler_params=_params(("arbitrary",)),
        name="moe_combine_norm",
    )(pos, x, gates, out_norm.reshape(1, d), y)


def moe_layer(x, a, wa, g, w_router, wg, wu, wd, out_norm=None, tm=1024, sub=256, tf=256):
    n_tok = x.shape[0]
    i32 = jnp.int32
    x1, idx_full, gate_full, cnt = router(x, a, wa, g, w_router)
    experts = idx_full[:, 0:2]
    ranks = idx_full[:, 2:4]
    counts = cnt[0, :N_EXPERTS].astype(i32)
    n_assign = 2 * n_tok
    n_sub = tm // sub
    n_sb = n_assign // sub + N_EXPERTS
    n_tiles = n_sb // n_sub
    n_vis = n_tiles + N_EXPERTS - 1

    e_ids = jnp.arange(N_EXPERTS, dtype=i32)
    sb_per = (counts + sub - 1) // sub
    sb_end = jnp.cumsum(sb_per)
    sb_start = sb_end - sb_per
    sb_used = sb_end[-1]
    sb_ids = jnp.arange(n_sb, dtype=i32)
    sb_valid = sb_ids < sb_used
    sb_expert = jnp.minimum(jnp.sum((sb_ids[:, None] >= sb_end[None, :]).astype(i32), axis=1),
                            N_EXPERTS - 1)
    row_start = sb_start * sub

    start_of = jnp.sum(jnp.where(experts[:, :, None] == e_ids, row_start, 0), axis=-1)
    pos = (start_of + ranks).astype(i32).T.reshape(-1)

    n_pad = n_sb * sub - n_assign
    pad_end = jnp.cumsum(sb_per * sub - counts)
    pad_ids = jnp.arange(n_pad, dtype=i32)
    pad_e = jnp.sum((pad_ids[:, None] >= pad_end[None, :]).astype(i32), axis=1)
    keys = jnp.concatenate([experts.reshape(-1) * 2, pad_e * 2 + 1])
    toks = jnp.concatenate([jnp.arange(n_assign, dtype=i32) // 2, (pad_ids * 8) % n_tok])
    _, src_rows = lax.sort((keys, toks), num_keys=1, is_stable=True)

    prev_e = jnp.concatenate([jnp.full((1,), -1, i32), sb_expert[:-1]])
    new_vis = sb_valid & ((sb_ids % n_sub == 0) | (sb_expert != prev_e))
    vis_cum = jnp.cumsum(new_vis.astype(i32))
    vis_used = vis_cum[-1]
    tiles_used = (sb_used + n_sub - 1) // n_sub
    v_ids = jnp.arange(n_vis, dtype=i32)
    live = v_ids < vis_used
    v_eff = jnp.minimum(v_ids, vis_used - 1)
    first_sb = jnp.sum((vis_cum[None, :] <= v_eff[:, None]).astype(i32), axis=1)
    is_first = sb_ids[None, :] == first_sb[:, None]
    vis_expert = jnp.sum(jnp.where(is_first, sb_expert[None, :], 0), axis=1)
    spare_tile = tiles_used + (v_ids - vis_used)
    vis_tile = jnp.where(live, first_sb // n_sub, jnp.minimum(spare_tile, n_tiles - 1)).astype(i32)
    vis_first = jnp.where(live, first_sb % n_sub == 0, spare_tile < n_tiles).astype(i32)
    in_run = ((sb_ids[None, :] // n_sub == vis_tile[:, None]) & sb_valid[None, :]
              & (sb_expert[None, :] == vis_expert[:, None]) & live[:, None])
    vis_cnt = jnp.sum(in_run.astype(i32), axis=1)
    vis_lo = first_sb % n_sub

    xs = gather_norm(x1, g, src_rows.astype(i32))
    y = moe_ffn(xs, vis_tile, vis_expert.astype(i32), live.astype(i32), vis_first,
                vis_lo.astype(i32), vis_cnt, wg, wu, wd, tm, sub, tf)
    return moe_combine(x1, gate_full, y, pos, out_norm)


def _final_norm_body(x_ref, g_ref, o_ref):
    o_ref[...] = _rms(x_ref[...], g_ref[...])


def final_norm(x, g, first_block, n_blocks, tm=512):
    d = x.shape[1]
    return pl.pallas_call(
        _final_norm_body,
        grid=(n_blocks,),
        in_specs=[pl.BlockSpec((tm, d), lambda i: (first_block + i, 0)),
                  pl.BlockSpec((1, d), lambda i: (0, 0))],
        out_specs=pl.BlockSpec((tm, d), lambda i: (i, 0)),
        out_shape=jax.ShapeDtypeStruct((n_blocks * tm, d), F32),
        compiler_params=_params(("parallel",)),
        name="final_norm",
    )(x, g.reshape(1, d))


def kernel(x_prompt, x_sample, cache_win_k, cache_win_v, state_conv, cache_mem_k, cache_mem_v,
           mem_prompt, norm_mix, w_in, attn_sinks, att_out_norm, conv_w, conv_b, conv_ln_g,
           conv_ln_b, conv_out_norm, w_out, norm_cross, norm_mem, w_xq, w_xk, w_xv, w_xo,
           norm_ffn, w_gate, w_up, w_down, w_router, we_gate, we_up, we_down, final_norm_g):
    x = (x_prompt.reshape(T_P, D_MODEL), x_sample.reshape(T_S, D_MODEL))
    mem = mem_prompt.reshape(N_MEM, D_MODEL)
    o1 = ATT_W
    o2 = o1 + KV_W
    o3 = o2 + KV_W
    o4 = o3 + CONV_C
    wb = cache_win_k.shape[2]
    mem_k_rows = cache_mem_k.reshape(DEPTH, DEC_BATCH, N_MEM * X_HEADS, X_HEAD_DIM)
    mem_v_rows = cache_mem_v.reshape(DEPTH, DEC_BATCH, N_MEM * X_HEADS, X_HEAD_DIM)
    pk, pv, pc, pmk, pmv, sk, sv, sc = [], [], [], [], [], [], [], []
    for l in range(DEPTH):
        wl = w_in[l]
        w_u = jnp.concatenate([wl[:, :o1], wl[:, o3:o4], wl[:, o4:], wl[:, o1:o2], wl[:, o2:o3]],
                              axis=1).astype(BF16)
        u = rms_matmul(x, norm_mix[l], w_u, tm=512, tn=U_W)
        ka, kg, kk, kv = ATT_W, ATT_W + CONV_C, ATT_W + 2 * CONV_C, ATT_W + 2 * CONV_C + KV_W

        att_p = swa_prompt(u, attn_sinks[l], att_out_norm[l])
        conv_p, state_p = conv_prompt(u, conv_w[l], conv_b[l], conv_ln_g[l], conv_ln_b[l],
                                      conv_out_norm[l])
        us = u[T_P:]
        q_s = us[:, :ATT_W].reshape(DEC_BATCH, DEC_SEQ, ATT_W)
        a_s = us[:, ka:kg].reshape(DEC_BATCH, DEC_SEQ, CONV_C)
        g_s = us[:, kg:kk].reshape(DEC_BATCH, DEC_SEQ, CONV_C)
        k_s = us[:, kk:kv].reshape(DEC_BATCH, DEC_SEQ, KV_W)
        v_s = us[:, kv:].reshape(DEC_BATCH, DEC_SEQ, KV_W)
        att_s, k_win, v_win = swa_sample(q_s, k_s, v_s,
                                         cache_win_k[l].reshape(DEC_BATCH, wb, KV_W),
                                         cache_win_v[l].reshape(DEC_BATCH, wb, KV_W),
                                         attn_sinks[l], att_out_norm[l], bt=16)
        conv_s, state_s = conv_sample(state_conv[l], a_s, g_s, conv_w[l], conv_b[l],
                                      conv_ln_g[l], conv_ln_b[l], conv_out_norm[l], bt=16)
        conv_s = conv_s.transpose(1, 0, 2).reshape(T_S, CONV_C)
        x = matmul_add(x, [(att_p, att_s.reshape(T_S, ATT_W)), (conv_p, conv_s)],
                       w_out[l].astype(BF16), tm=512, tn=D_MODEL)

        keep = min(WINDOW, SEQ)
        pk.append(u[T_P - keep:T_P, kk:kv].reshape(1, keep, N_KV, HEAD_DIM))
        pv.append(u[T_P - keep:T_P, kv:].reshape(1, keep, N_KV, HEAD_DIM))
        pc.append(state_p.reshape(1, STATE_ROWS, CONV_C))
        sk.append(k_win.reshape(DEC_BATCH, wb, N_KV, HEAD_DIM))
        sv.append(v_win.reshape(DEC_BATCH, wb, N_KV, HEAD_DIM))
        sc.append(state_s)

        mk = rms_matmul(mem, norm_mem[l], w_xk[l].astype(BF16), tm=N_MEM, tn=X_W)
        mv = rms_matmul(mem, norm_mem[l], w_xv[l].astype(BF16), tm=N_MEM, tn=X_W)
        pmk.append(mk.reshape(1, N_MEM, X_HEADS, X_HEAD_DIM))
        pmv.append(mv.reshape(1, N_MEM, X_HEADS, X_HEAD_DIM))
        qx = rms_matmul(x, norm_cross[l], w_xq[l].astype(BF16), tm=512, tn=X_W)
        o_p = cross_prompt(qx, mk, mv)
        q_ht = qx[T_P:].reshape(DEC_BATCH, DEC_SEQ, X_HEADS, X_HEAD_DIM).transpose(0, 2, 1, 3)
        o_s = cross_sample(q_ht.reshape(DEC_BATCH, X_HEADS * DEC_SEQ, X_HEAD_DIM), mem_k_rows,
                           mem_v_rows, l)
        o_s = o_s.reshape(DEC_BATCH, X_HEADS, DEC_SEQ, X_HEAD_DIM).transpose(0, 2, 1, 3)
        o_all = jnp.concatenate([o_p, o_s.reshape(T_S, X_W)], axis=0)
        wxo = w_xo[l].astype(BF16)

        if l % 2 == 0:
            d = l // 2
            x = dense_ffn(x, o_all, wxo, norm_ffn[l], w_gate[d], w_up[d], w_down[d], tm=1088, tf=256)
        else:
            m = l // 2
            last = l == DEPTH - 1
            x = moe_layer(x, o_all, wxo, norm_ffn[l], w_router[m], we_gate[m], we_up[m], we_down[m],
                          out_norm=final_norm_g if last else None)

    if isinstance(x, tuple):
        y_prompt, y_sample = x
    else:
        y_prompt = final_norm(x, final_norm_g, 0, T_P // 512)
        y_sample = final_norm(x, final_norm_g, T_P // 512, T_S // 512)
    y_prompt = y_prompt.reshape(1, SEQ, D_MODEL)
    y_sample = y_sample.reshape(DEC_BATCH, DEC_SEQ, D_MODEL)
    return (y_prompt, y_sample, jnp.stack(pk), jnp.stack(pv), jnp.stack(pc), jnp.stack(pmk),
            jnp.stack(pmv), jnp.stack(sk), jnp.stack(sv), jnp.stack(sc))
```

```python
import functools

import jax
import jax.numpy as jnp
import numpy as np
from jax import lax
from jax.experimental import pallas as pl
from jax.experimental.pallas import tpu as pltpu

F32 = jnp.float32
BF16 = jnp.bfloat16

D_MODEL = 2048
SEQ = 8192
DEPTH = 2
DEC_BATCH = 128
DEC_SEQ = 4
HEAD_DIM = 64
ATT_W = 1024
N_HEADS = 16
N_KV = 4
KV_W = N_KV * HEAD_DIM
WINDOW = 128
CONV_C = 1024
CONV_W = 31
N_MEM = 256
X_HEADS = 4
X_HEAD_DIM = 128
X_W = X_HEADS * X_HEAD_DIM
D_FF = 5632
N_EXPERTS = 8
D_FF_E = 7168
EPS = 1e-6
NEG = -1e30

T_P = SEQ
T_S = DEC_BATCH * DEC_SEQ
T_ALL = T_P + T_S

LANES = 128
SUBLANES = 8
HALF = HEAD_DIM
STATE_ROWS = CONV_W - 1
STATE_PAD = 32
KEYS_PAD = 256
VMEM_LIMIT = 56 * 1024 * 1024

SLOPES = [float(2.0 ** (-8.0 * (h + 1) / N_HEADS)) for h in range(N_HEADS)]

U_W = ATT_W + 2 * CONV_C + 2 * KV_W


def _params(sem):
    return pltpu.CompilerParams(dimension_semantics=sem, vmem_limit_bytes=VMEM_LIMIT)


def _rms(x, g):
    r = lax.rsqrt(jnp.mean(x * x, axis=-1, keepdims=True) + EPS)
    return x * r * g


def _sigmoid(x):
    return 1.0 / (1.0 + jnp.exp(-x))


def _parts(x):
    return x if isinstance(x, tuple) else (x,)


def _n_rows(parts):
    return sum(p.shape[0] for p in parts)


def _row_specs(parts, tm, width, col):
    if len(parts) == 1:
        return [pl.BlockSpec((tm, width), lambda i, j: (i, col(j)))]
    n_p = parts[0].shape[0] // tm
    return [pl.BlockSpec((tm, width), lambda i, j: (jnp.minimum(i, n_p - 1), col(j))),
            pl.BlockSpec((tm, width), lambda i, j: (jnp.maximum(i - n_p, 0), col(j)))]


def _row_value(refs, n_p):
    if len(refs) == 1:
        return refs[0][...]
    return jnp.where(pl.program_id(0) < n_p, refs[0][...], refs[1][...])


def _rms_matmul_body(*refs, n_x, n_p):
    x_refs = refs[:n_x]
    g_ref, w_ref, o_ref, xn_ref = refs[n_x:]

    @pl.when(pl.program_id(1) == 0)
    def _():
        xn_ref[...] = _rms(_row_value(x_refs, n_p), g_ref[...]).astype(BF16)

    o_ref[...] = jnp.dot(xn_ref[...], w_ref[...], preferred_element_type=F32).astype(o_ref.dtype)


def rms_matmul(x, g, w, tm, tn, out_dtype=F32):
    parts = _parts(x)
    m, k = _n_rows(parts), parts[0].shape[1]
    n = w.shape[1]
    return pl.pallas_call(
        functools.partial(_rms_matmul_body, n_x=len(parts), n_p=parts[0].shape[0] // tm),
        grid=(m // tm, n // tn),
        in_specs=_row_specs(parts, tm, k, lambda j: 0) + [
            pl.BlockSpec((1, k), lambda i, j: (0, 0)),
            pl.BlockSpec((k, tn), lambda i, j: (0, j))],
        out_specs=pl.BlockSpec((tm, tn), lambda i, j: (i, j)),
        out_shape=jax.ShapeDtypeStruct((m, n), out_dtype),
        scratch_shapes=[pltpu.VMEM((tm, k), BF16)],
        compiler_params=_params(("parallel", "arbitrary")),
        name="rms_matmul",
    )(*parts, g.reshape(1, k), w)


def _matmul_add_body(*refs, counts, n_p):
    o_ref = refs[-1]
    n_terms = len(counts) - 1
    w_refs = refs[len(refs) - 1 - n_terms:len(refs) - 1]
    at = counts[0]
    acc = _row_value(refs[:at], n_p)
    for t in range(n_terms):
        a = _row_value(refs[at:at + counts[1 + t]], n_p)
        at += counts[1 + t]
        acc = acc + jnp.dot(a, w_refs[t][...], preferred_element_type=F32)
    o_ref[...] = acc


def matmul_add(x, a_list, w, tm, tn):
    x_parts = _parts(x)
    a_parts = [_parts(a) for a in a_list]
    m, n = _n_rows(x_parts), x_parts[0].shape[1]
    in_specs = _row_specs(x_parts, tm, tn, lambda j: j)
    operands = list(x_parts)
    for parts in a_parts:
        in_specs += _row_specs(parts, tm, parts[0].shape[1], lambda j: 0)
        operands += list(parts)
    row = 0
    for parts in a_parts:
        ka = parts[0].shape[1]
        in_specs.append(pl.BlockSpec((ka, tn), lambda i, j, blk=row // ka: (blk, j)))
        operands.append(w)
        row += ka
    counts = (len(x_parts),) + tuple(len(p) for p in a_parts)
    return pl.pallas_call(
        functools.partial(_matmul_add_body, counts=counts, n_p=T_P // tm),
        grid=(m // tm, n // tn),
        in_specs=in_specs,
        out_specs=pl.BlockSpec((tm, tn), lambda i, j: (i, j)),
        out_shape=jax.ShapeDtypeStruct((m, n), F32),
        input_output_aliases={0: 0} if len(x_parts) == 1 else {},
        compiler_params=_params(("parallel", "parallel")),
        name="matmul_add",
    )(*operands)


def _half_mask(shape, half):
    lane = lax.broadcasted_iota(jnp.int32, shape, len(shape) - 1)
    return (lane >= HALF) if half else (lane < HALF)


def _sink_softmax_pv(s, sink, v_tile, batched):
    m = jnp.maximum(jnp.max(s, axis=-1, keepdims=True), sink)
    p = jnp.exp(s - m)
    den = jnp.sum(p, axis=-1, keepdims=True) + jnp.exp(sink - m)
    if batched:
        o = jnp.einsum("bqk,bkd->bqd", p.astype(BF16), v_tile, preferred_element_type=F32)
    else:
        o = jnp.dot(p.astype(BF16), v_tile, preferred_element_type=F32)
    return o / den


def _swa_prompt_body(sink_ref, q_ref, kp_ref, kc_ref, vp_ref, vc_ref, ga_ref, o_ref):
    i = pl.program_id(0)
    blk = WINDOW
    q = q_ref[...] * (HEAD_DIM ** -0.5)
    k = jnp.concatenate([kp_ref[...], kc_ref[...]], axis=0).astype(BF16)
    v = jnp.concatenate([vp_ref[...], vc_ref[...]], axis=0).astype(BF16)
    row = lax.broadcasted_iota(jnp.int32, (blk, 2 * blk), 0)
    col = lax.broadcasted_iota(jnp.int32, (blk, 2 * blk), 1)
    dist = blk + row - col
    valid = (dist >= 0) & (dist <= WINDOW) & ((col >= blk) | (i > 0))
    distf = dist.astype(F32)
    tiles = [None] * (N_HEADS // 2)
    for n in range(N_KV):
        half = n % 2
        ts = slice((n // 2) * LANES, (n // 2 + 1) * LANES)
        k_m = jnp.where(_half_mask((2 * blk, LANES), half), k[:, ts], jnp.zeros((), BF16))
        v_t = v[:, ts]
        q_al = jnp.concatenate([q[:, (2 * n) * LANES:(2 * n + 1) * LANES],
                                q[:, (2 * n + 1) * LANES:(2 * n + 2) * LANES]], axis=0)
        q_mis = pltpu.roll(q_al, HALF, 1)
        lhs = jnp.concatenate([q_al, q_mis], axis=0).astype(BF16)
        s = lax.dot_general(lhs, k_m, (((1,), (1,)), ((), ())), preferred_element_type=F32)
        outs = []
        for b4 in range(4):
            a, j = divmod(b4, 2)
            h = 4 * n + 2 * j + (half if a == 0 else 1 - half)
            sb = s[b4 * blk:(b4 + 1) * blk]
            sb = jnp.where(valid, sb - SLOPES[h] * distf, NEG)
            outs.append(_sink_softmax_pv(sb, sink_ref[h], v_t, False))
        keep = _half_mask((blk, LANES), half)
        for j in range(2):
            tiles[2 * n + j] = jnp.where(keep, outs[j], pltpu.roll(outs[2 + j], HALF, 1))
    att = jnp.concatenate(tiles, axis=1)
    o_ref[...] = _rms(att, ga_ref[...]).astype(o_ref.dtype)


def swa_prompt(u, sinks, g_att):
    blk = WINDOW
    nb = T_P // blk
    kcol = (ATT_W + 2 * CONV_C) // KV_W
    prev = lambda i: jnp.maximum(i - 1, 0)
    return pl.pallas_call(
        _swa_prompt_body,
        grid=(nb,),
        in_specs=[pl.BlockSpec(memory_space=pltpu.SMEM),
                  pl.BlockSpec((blk, ATT_W), lambda i: (i, 0)),
                  pl.BlockSpec((blk, KV_W), lambda i: (prev(i), kcol)),
                  pl.BlockSpec((blk, KV_W), lambda i: (i, kcol)),
                  pl.BlockSpec((blk, KV_W), lambda i: (prev(i), kcol + 1)),
                  pl.BlockSpec((blk, KV_W), lambda i: (i, kcol + 1)),
                  pl.BlockSpec((1, ATT_W), lambda i: (0, 0))],
        out_specs=pl.BlockSpec((blk, ATT_W), lambda i: (i, 0)),
        out_shape=jax.ShapeDtypeStruct((T_P, ATT_W), BF16),
        compiler_params=_params(("parallel",)),
        name="swa_prompt",
    )(sinks, u, u, u, u, u, g_att.reshape(1, ATT_W))


def _swa_sample_body(sink_ref, q_ref, kn_ref, vn_ref, kc_ref, vc_ref, bias_ref, ga_ref,
                     o_ref, ko_ref, vo_ref, kall_ref, vall_ref):
    bt = q_ref.shape[0]
    wb = kc_ref.shape[1]
    kall_ref[:, 0:wb, :] = kc_ref[...]
    kall_ref[:, wb:wb + DEC_SEQ, :] = kn_ref[...]
    kall_ref[:, wb + DEC_SEQ:, :] = jnp.zeros((bt, KEYS_PAD - wb - DEC_SEQ, KV_W), F32)
    vall_ref[:, 0:wb, :] = vc_ref[...]
    vall_ref[:, wb:wb + DEC_SEQ, :] = vn_ref[...]
    vall_ref[:, wb + DEC_SEQ:, :] = jnp.zeros((bt, KEYS_PAD - wb - DEC_SEQ, KV_W), F32)
    ko_ref[...] = kall_ref[:, DEC_SEQ:DEC_SEQ + wb, :]
    vo_ref[...] = vall_ref[:, DEC_SEQ:DEC_SEQ + wb, :]

    q = q_ref[...] * (HEAD_DIM ** -0.5)
    k = kall_ref[...].astype(BF16)
    v = vall_ref[...].astype(BF16)
    tiles = [None] * (N_HEADS // 2)
    for n in range(N_KV):
        half = n % 2
        ts = slice((n // 2) * LANES, (n // 2 + 1) * LANES)
        k_m = jnp.where(_half_mask((bt, KEYS_PAD, LANES), half), k[:, :, ts], jnp.zeros((), BF16))
        v_t = v[:, :, ts]
        q_al = jnp.concatenate([q[:, :, (2 * n) * LANES:(2 * n + 1) * LANES],
                                q[:, :, (2 * n + 1) * LANES:(2 * n + 2) * LANES]], axis=1)
        q_mis = pltpu.roll(q_al, HALF, 2)
        lhs = jnp.concatenate([q_al, q_mis], axis=1).astype(BF16)
        s = jnp.einsum("bqd,bkd->bqk", lhs, k_m, preferred_element_type=F32)
        bias = bias_ref[n]
        s = jnp.where(bias > 0.5 * NEG, s + bias, NEG)
        sink = sink_ref[n]
        o = _sink_softmax_pv(s, sink, v_t, True)
        keep = _half_mask((bt, DEC_SEQ, LANES), half)
        for j in range(2):
            o_al = o[:, j * DEC_SEQ:(j + 1) * DEC_SEQ]
            o_mis = pltpu.roll(o[:, (2 + j) * DEC_SEQ:(3 + j) * DEC_SEQ], HALF, 2)
            tiles[2 * n + j] = jnp.where(keep, o_al, o_mis)
    att = jnp.concatenate(tiles, axis=2)
    o_ref[...] = _rms(att, ga_ref[...]).astype(o_ref.dtype)


def _sample_bias_and_sinks(sinks):
    wb = WINDOW
    bias = np.full((N_KV, 4 * DEC_SEQ, KEYS_PAD), NEG, np.float32)
    head = np.zeros((N_KV, 4 * DEC_SEQ), np.int32)
    for n in range(N_KV):
        half = n % 2
        for a in range(2):
            for j in range(2):
                h = 4 * n + 2 * j + (half if a == 0 else 1 - half)
                for t in range(DEC_SEQ):
                    r = (2 * a + j) * DEC_SEQ + t
                    head[n, r] = h
                    for kk in range(wb + DEC_SEQ):
                        d = t + wb - kk
                        if 0 <= d <= WINDOW:
                            bias[n, r, kk] = -SLOPES[h] * d
    sink_rows = sinks[jnp.asarray(head)][..., None]
    return jnp.asarray(bias), sink_rows


def swa_sample(q_s, k_new, v_new, cache_k, cache_v, sinks, g_att, bt):
    wb = cache_k.shape[1]
    bias, sink_rows = _sample_bias_and_sinks(sinks)
    nq = 4 * DEC_SEQ
    b3 = lambda i: (i, 0, 0)
    z3 = lambda i: (0, 0, 0)
    return pl.pallas_call(
        _swa_sample_body,
        grid=(DEC_BATCH // bt,),
        in_specs=[pl.BlockSpec((N_KV, nq, 1), z3),
                  pl.BlockSpec((bt, DEC_SEQ, ATT_W), b3),
                  pl.BlockSpec((bt, DEC_SEQ, KV_W), b3),
                  pl.BlockSpec((bt, DEC_SEQ, KV_W), b3),
                  pl.BlockSpec((bt, wb, KV_W), b3),
                  pl.BlockSpec((bt, wb, KV_W), b3),
                  pl.BlockSpec((N_KV, nq, KEYS_PAD), z3),
                  pl.BlockSpec((1, 1, ATT_W), z3)],
        out_specs=[pl.BlockSpec((bt, DEC_SEQ, ATT_W), b3),
                   pl.BlockSpec((bt, wb, KV_W), b3),
                   pl.BlockSpec((bt, wb, KV_W), b3)],
        out_shape=[jax.ShapeDtypeStruct((DEC_BATCH, DEC_SEQ, ATT_W), BF16),
                   jax.ShapeDtypeStruct((DEC_BATCH, wb, KV_W), F32),
                   jax.ShapeDtypeStruct((DEC_BATCH, wb, KV_W), F32)],
        scratch_shapes=[pltpu.VMEM((bt, KEYS_PAD, KV_W), F32),
                        pltpu.VMEM((bt, KEYS_PAD, KV_W), F32)],
        compiler_params=_params(("parallel",)),
        name="swa_sample",
    )(sink_rows, q_s, k_new, v_new, cache_k, cache_v, bias, g_att.reshape(1, 1, ATT_W))


def _conv_post(y, cb, lg, lb, gc):
    y = y + cb
    mu = jnp.mean(y, axis=-1, keepdims=True)
    yc = y - mu
    z = yc * lax.rsqrt(jnp.mean(yc * yc, axis=-1, keepdims=True) + EPS) * lg + lb
    c = z * _sigmoid(z)
    return _rms(c, gc)


def _conv_prompt_body(ap_ref, gp_ref, a_ref, g_ref, cw_ref, cb_ref, lg_ref, lb_ref, gc_ref,
                      o_ref, st_ref, ext_ref, y_ref, *, tt, tc):
    i = pl.program_id(0)
    glu_prev = ap_ref[...] * _sigmoid(gp_ref[...])
    ext_ref[0:STATE_PAD, :] = jnp.where(i > 0, glu_prev, 0.0)
    glu = a_ref[...] * _sigmoid(g_ref[...])
    ext_ref[STATE_PAD:STATE_PAD + tt, :] = glu
    st_ref[...] = glu[tt - STATE_PAD:]
    first = STATE_PAD - STATE_ROWS
    for cblk in range(CONV_C // LANES):
        cs = slice(cblk * LANES, (cblk + 1) * LANES)
        for t0 in range(0, tt, tc):
            total = None
            for lo in range(SUBLANES):
                n_hi = (CONV_W - lo + SUBLANES - 1) // SUBLANES
                start = t0 + first + lo
                win = ext_ref[start:start + tc + SUBLANES * (n_hi - 1), cs]
                acc = None
                for hi in range(n_hi):
                    j = SUBLANES * hi + lo
                    term = win[SUBLANES * hi:SUBLANES * hi + tc] * cw_ref[j:j + 1, cs]
                    acc = term if acc is None else acc + term
                total = acc if total is None else total + acc
            y_ref[t0:t0 + tc, cs] = total
    o_ref[...] = _conv_post(y_ref[...], cb_ref[...], lg_ref[...], lb_ref[...],
                            gc_ref[...]).astype(o_ref.dtype)


def conv_prompt(u, conv_w, conv_b, ln_g, ln_b, g_conv, tt=128, tc=64):
    nt = T_P // tt
    per = tt // STATE_PAD
    prev = lambda i: jnp.maximum(i * per - 1, 0)
    acol, gcol = ATT_W // CONV_C, ATT_W // CONV_C + 1
    vec = lambda: pl.BlockSpec((1, CONV_C), lambda i: (0, 0))
    out, state = pl.pallas_call(
        functools.partial(_conv_prompt_body, tt=tt, tc=tc),
        grid=(nt,),
        in_specs=[pl.BlockSpec((STATE_PAD, CONV_C), lambda i: (prev(i), acol)),
                  pl.BlockSpec((STATE_PAD, CONV_C), lambda i: (prev(i), gcol)),
                  pl.BlockSpec((tt, CONV_C), lambda i: (i, acol)),
                  pl.BlockSpec((tt, CONV_C), lambda i: (i, gcol)),
                  pl.BlockSpec((CONV_W, CONV_C), lambda i: (0, 0)),
                  vec(), vec(), vec(), vec()],
        out_specs=[pl.BlockSpec((tt, CONV_C), lambda i: (i, 0)),
                   pl.BlockSpec((STATE_PAD, CONV_C), lambda i: (0, 0))],
        out_shape=[jax.ShapeDtypeStruct((T_P, CONV_C), BF16),
                   jax.ShapeDtypeStruct((STATE_PAD, CONV_C), F32)],
        scratch_shapes=[pltpu.VMEM((STATE_PAD + tt, CONV_C), F32),
                        pltpu.VMEM((tt, CONV_C), F32)],
        compiler_params=_params(("arbitrary",)),
        name="conv_prompt",
    )(u, u, u, u, conv_w, conv_b.reshape(1, -1), ln_g.reshape(1, -1), ln_b.reshape(1, -1),
      g_conv.reshape(1, -1))
    return out, state[STATE_PAD - STATE_ROWS:]


def _conv_sample_body(st_ref, a_ref, g_ref, cw_ref, cb_ref, lg_ref, lb_ref, gc_ref,
                      o_ref, so_ref):
    glu = a_ref[...] * _sigmoid(g_ref[...])
    ext = jnp.concatenate([st_ref[...], glu], axis=1)
    so_ref[...] = ext[:, DEC_SEQ:]
    w = cw_ref[...]
    for t in range(DEC_SEQ):
        y = jnp.sum(ext[:, t:t + CONV_W] * w[None], axis=1)
        o_ref[t] = _conv_post(y, cb_ref[...], lg_ref[...], lb_ref[...],
                              gc_ref[...]).astype(o_ref.dtype)


def conv_sample(state, a_s, g_s, conv_w, conv_b, ln_g, ln_b, g_conv, bt):
    b3 = lambda i: (i, 0, 0)
    vec = lambda: pl.BlockSpec((1, CONV_C), lambda i: (0, 0))
    return pl.pallas_call(
        _conv_sample_body,
        grid=(DEC_BATCH // bt,),
        in_specs=[pl.BlockSpec((bt, STATE_ROWS, CONV_C), b3),
                  pl.BlockSpec((bt, DEC_SEQ, CONV_C), b3),
                  pl.BlockSpec((bt, DEC_SEQ, CONV_C), b3),
                  pl.BlockSpec((CONV_W, CONV_C), lambda i: (0, 0)),
                  vec(), vec(), vec(), vec()],
        out_specs=[pl.BlockSpec((DEC_SEQ, bt, CONV_C), lambda i: (0, i, 0)),
                   pl.BlockSpec((bt, STATE_ROWS, CONV_C), b3)],
        out_shape=[jax.ShapeDtypeStruct((DEC_SEQ, DEC_BATCH, CONV_C), BF16),
                   jax.ShapeDtypeStruct((DEC_BATCH, STATE_ROWS, CONV_C), F32)],
        compiler_params=_params(("parallel",)),
        name="conv_sample",
    )(state, a_s, g_s, conv_w, conv_b.reshape(1, -1), ln_g.reshape(1, -1), ln_b.reshape(1, -1),
      g_conv.reshape(1, -1))


def _cross_prompt_body(q_ref, k_ref, v_ref, o_ref):
    q = q_ref[...].astype(BF16)
    k = k_ref[...].astype(BF16)
    v = v_ref[...].astype(BF16)
    outs = []
    for h in range(X_HEADS):
        hs = slice(h * X_HEAD_DIM, (h + 1) * X_HEAD_DIM)
        s = lax.dot_general(q[:, hs], k[:, hs], (((1,), (1,)), ((), ())),
                            preferred_element_type=F32) * (X_HEAD_DIM ** -0.5)
        p = jnp.exp(s - jnp.max(s, axis=-1, keepdims=True))
        den = jnp.sum(p, axis=-1, keepdims=True)
        outs.append(jnp.dot(p.astype(BF16), v[:, hs], preferred_element_type=F32) / den)
    o_ref[...] = jnp.concatenate(outs, axis=1).astype(o_ref.dtype)


def cross_prompt(q, mem_k, mem_v, tq=512):
    return pl.pallas_call(
        _cross_prompt_body,
        grid=(T_P // tq,),
        in_specs=[pl.BlockSpec((tq, X_W), lambda i: (i, 0)),
                  pl.BlockSpec((N_MEM, X_W), lambda i: (0, 0)),
                  pl.BlockSpec((N_MEM, X_W), lambda i: (0, 0))],
        out_specs=pl.BlockSpec((tq, X_W), lambda i: (i, 0)),
        out_shape=jax.ShapeDtypeStruct((T_P, X_W), BF16),
        compiler_params=_params(("parallel",)),
        name="cross_prompt",
    )(q, mem_k, mem_v)


def _cross_sample_body(q_ref, k_ref, v_ref, o_ref):
    q = q_ref[...].astype(BF16)
    k = k_ref[...].astype(BF16)
    v = v_ref[...].astype(BF16)
    s = jnp.einsum("bqd,bkd->bqk", q, k, preferred_element_type=F32) * (X_HEAD_DIM ** -0.5)
    row = lax.broadcasted_iota(jnp.int32, s.shape[1:], 0)
    col = lax.broadcasted_iota(jnp.int32, s.shape[1:], 1)
    same_head = (col % X_HEADS) == (row // DEC_SEQ)
    s = jnp.where(same_head, s, NEG)
    p = jnp.exp(s - jnp.max(s, axis=-1, keepdims=True))
    den = jnp.sum(p, axis=-1, keepdims=True)
    o = jnp.einsum("bqk,bkd->bqd", p.astype(BF16), v, preferred_element_type=F32) / den
    o_ref[...] = o.astype(o_ref.dtype)


def cross_sample(q_ht, mem_k, mem_v, layer, bt=8):
    rows = X_HEADS * DEC_SEQ
    b3 = lambda i: (i, 0, 0)
    b4 = lambda i: (layer, i, 0, 0)
    return pl.pallas_call(
        _cross_sample_body,
        grid=(DEC_BATCH // bt,),
        in_specs=[pl.BlockSpec((bt, rows, X_HEAD_DIM), b3),
                  pl.BlockSpec((None, bt, N_MEM * X_HEADS, X_HEAD_DIM), b4),
                  pl.BlockSpec((None, bt, N_MEM * X_HEADS, X_HEAD_DIM), b4)],
        out_specs=pl.BlockSpec((bt, rows, X_HEAD_DIM), b3),
        out_shape=jax.ShapeDtypeStruct((DEC_BATCH, rows, X_HEAD_DIM), BF16),
        compiler_params=_params(("parallel",)),
        name="cross_sample",
    )(q_ht, mem_k, mem_v)


def _swiglu_step(x, wg_ref, wu_ref, wd_ref):
    hg = jnp.dot(x, wg_ref[...].astype(BF16), preferred_element_type=F32)
    hu = jnp.dot(x, wu_ref[...].astype(BF16), preferred_element_type=F32)
    h = (hg * _sigmoid(hg) * hu).astype(BF16)
    return jnp.dot(h, wd_ref[...].astype(BF16), preferred_element_type=F32)


def _dense_ffn_body(x_ref, a_ref, wa_ref, g_ref, wg_ref, wu_ref, wd_ref, o_ref, xn_ref):
    @pl.when(pl.program_id(1) == 0)
    def _():
        x = x_ref[...] + jnp.dot(a_ref[...], wa_ref[...], preferred_element_type=F32)
        xn_ref[...] = _rms(x, g_ref[...]).astype(BF16)
        o_ref[...] = x

    o_ref[...] += _swiglu_step(xn_ref[...], wg_ref, wu_ref, wd_ref)


def dense_ffn(x, a, wa, g, wg, wu, wd, tm, tf):
    m, d = x.shape
    ka = a.shape[1]
    dff = wg.shape[1]
    return pl.pallas_call(
        _dense_ffn_body,
        grid=(m // tm, dff // tf),
        in_specs=[pl.BlockSpec((tm, d), lambda i, f: (i, 0), pipeline_mode=pl.Buffered(1)),
                  pl.BlockSpec((tm, ka), lambda i, f: (i, 0)),
                  pl.BlockSpec((ka, d), lambda i, f: (0, 0)),
                  pl.BlockSpec((1, d), lambda i, f: (0, 0)),
                  pl.BlockSpec((d, tf), lambda i, f: (0, f)),
                  pl.BlockSpec((d, tf), lambda i, f: (0, f)),
                  pl.BlockSpec((tf, d), lambda i, f: (f, 0))],
        out_specs=pl.BlockSpec((tm, d), lambda i, f: (i, 0)),
        out_shape=jax.ShapeDtypeStruct((m, d), F32),
        scratch_shapes=[pltpu.VMEM((tm, d), BF16)],
        compiler_params=_params(("parallel", "arbitrary")),
        name="dense_ffn",
    )(x, a, wa, g.reshape(1, d), wg, wu, wd)


def _router_body(x_ref, a_ref, wa_ref, g_ref, wr_ref, x1_ref, idx_ref, gate_ref, cnt_ref,
                 carry_ref):
    @pl.when(pl.program_id(0) == 0)
    def _():
        carry_ref[...] = jnp.zeros(carry_ref.shape, F32)

    x1 = x_ref[...] + jnp.dot(a_ref[...], wa_ref[...], preferred_element_type=F32)
    x1_ref[...] = x1
    xn = _rms(x1, g_ref[...])
    logits = jnp.dot(xn, wr_ref[...], preferred_element_type=F32,
                     precision=lax.Precision.HIGHEST)
    tm = logits.shape[0]
    lane = lax.broadcasted_iota(jnp.int32, logits.shape, 1)
    lg = jnp.where(lane < N_EXPERTS, logits, -jnp.inf)
    m1 = jnp.max(lg, axis=-1, keepdims=True)
    i1 = jnp.min(jnp.where(lg == m1, lane, LANES), axis=-1, keepdims=True)
    lg2 = jnp.where(lane == i1, -jnp.inf, lg)
    m2 = jnp.max(lg2, axis=-1, keepdims=True)
    i2 = jnp.min(jnp.where(lg2 == m2, lane, LANES), axis=-1, keepdims=True)
    e = jnp.exp(m2 - m1)
    den = 1.0 + e
    gate_ref[...] = jnp.where(lane == 0, 1.0 / den, jnp.where(lane == 1, e / den, 0.0))

    chosen = (lane == i1) | (lane == i2)
    r = lax.broadcasted_iota(jnp.int32, (tm, tm), 0)
    c = lax.broadcasted_iota(jnp.int32, (tm, tm), 1)
    earlier = jnp.where(c < r, 1.0, 0.0).astype(BF16)
    before = jnp.dot(earlier, jnp.where(chosen, 1.0, 0.0).astype(BF16),
                     preferred_element_type=F32) + carry_ref[...]
    r1 = jnp.sum(jnp.where(lane == i1, before, 0.0), axis=-1, keepdims=True).astype(jnp.int32)
    r2 = jnp.sum(jnp.where(lane == i2, before, 0.0), axis=-1, keepdims=True).astype(jnp.int32)
    idx_ref[...] = jnp.where(lane == 0, i1, jnp.where(lane == 1, i2,
                             jnp.where(lane == 2, r1, jnp.where(lane == 3, r2, 0))))
    carry_ref[...] += jnp.sum(jnp.where(chosen, 1.0, 0.0), axis=0, keepdims=True)
    cnt_ref[...] = carry_ref[...]


def router(x, a, wa, g, w_router, tm=512):
    m, d = x.shape
    ka = a.shape[1]
    wr = jnp.zeros((d, LANES), F32).at[:, :N_EXPERTS].set(w_router)
    return pl.pallas_call(
        _router_body,
        grid=(m // tm,),
        in_specs=[pl.BlockSpec((tm, d), lambda i: (i, 0)),
                  pl.BlockSpec((tm, ka), lambda i: (i, 0)),
                  pl.BlockSpec((ka, d), lambda i: (0, 0)),
                  pl.BlockSpec((1, d), lambda i: (0, 0)),
                  pl.BlockSpec((d, LANES), lambda i: (0, 0))],
        out_specs=[pl.BlockSpec((tm, d), lambda i: (i, 0)),
                   pl.BlockSpec((tm, LANES), lambda i: (i, 0)),
                   pl.BlockSpec((tm, LANES), lambda i: (i, 0)),
                   pl.BlockSpec((1, LANES), lambda i: (0, 0))],
        out_shape=[jax.ShapeDtypeStruct((m, d), F32),
                   jax.ShapeDtypeStruct((m, LANES), jnp.int32),
                   jax.ShapeDtypeStruct((m, LANES), F32),
                   jax.ShapeDtypeStruct((1, LANES), F32)],
        scratch_shapes=[pltpu.VMEM((1, LANES), F32)],
        compiler_params=_params(("arbitrary",)),
        name="router",
    )(x, a, wa, g.reshape(1, d), wr)


def _row_copy(src_hbm, row, dst_ref, r, sem):
    return pltpu.make_async_copy(src_hbm.at[pl.ds(row, 1)], dst_ref.at[pl.ds(r, 1)], sem)


def _gather_norm_body(src_ref, x_hbm, g_ref, o_ref, buf_ref, sem, *, rows):
    base = pl.program_id(0) * rows

    def start(r, c):
        _row_copy(x_hbm, src_ref[base + r], buf_ref, r, sem).start()
        return c

    def wait(r, c):
        _row_copy(x_hbm, 0, buf_ref, r, sem).wait()
        return c

    lax.fori_loop(0, rows, start, 0, unroll=8)
    lax.fori_loop(0, rows, wait, 0, unroll=8)
    o_ref[...] = _rms(buf_ref[...], g_ref[...]).astype(o_ref.dtype)


def gather_norm(x, g, src_rows, rows=512):
    d = x.shape[1]
    n = src_rows.shape[0]
    grid_spec = pltpu.PrefetchScalarGridSpec(
        num_scalar_prefetch=1,
        grid=(n // rows,),
        in_specs=[pl.BlockSpec(memory_space=pl.ANY),
                  pl.BlockSpec((1, d), lambda i, s: (0, 0))],
        out_specs=pl.BlockSpec((rows, d), lambda i, s: (i, 0)),
        scratch_shapes=[pltpu.VMEM((rows, d), F32), pltpu.SemaphoreType.DMA(())],
    )
    return pl.pallas_call(
        functools.partial(_gather_norm_body, rows=rows),
        grid_spec=grid_spec,
        out_shape=jax.ShapeDtypeStruct((n, d), BF16),
        compiler_params=_params(("arbitrary",)),
        name="gather_norm",
    )(src_rows, x, g.reshape(1, d))


def _moe_ffn_body(ve_ref, vs_ref, vc_ref, vlive_ref, x_hbm, wg_ref, wu_ref, wd_ref, y_hbm,
                  xbuf_ref, acc_ref, wgb_ref, wub_ref, wdb_ref, sem_in, sem_out, *, sub, c_max, nf):
    v = pl.program_id(0)
    f = pl.program_id(1)
    count = vc_ref[v]
    first = vs_ref[v]
    live = vlive_ref[v] > 0

    def rows_of(j):
        return pl.ds(j * sub, sub)

    def copy_in(j):
        src = x_hbm.at[pl.ds(pl.multiple_of((first + j) * sub, sub), sub)]
        return pltpu.make_async_copy(src, xbuf_ref.at[rows_of(j)], sem_in)

    def copy_out(j):
        dst = y_hbm.at[pl.ds(pl.multiple_of((first + j) * sub, sub), sub)]
        return pltpu.make_async_copy(acc_ref.at[rows_of(j)], dst, sem_out)

    @pl.when(f == 0)
    def _():
        for j in range(c_max):
            @pl.when((j < count) & live)
            def _(j=j):
                copy_in(j).start()
        for j in range(c_max):
            @pl.when(j < count)
            def _(j=j):
                acc_ref[rows_of(j), :] = jnp.zeros((sub, acc_ref.shape[1]), F32)
        for j in range(c_max):
            @pl.when((j < count) & live)
            def _(j=j):
                copy_in(j).wait()

    @pl.when(live)
    def _():
        wgb_ref[...] = wg_ref[...].astype(BF16)
        wub_ref[...] = wu_ref[...].astype(BF16)
        wdb_ref[...] = wd_ref[...].astype(BF16)

        def chain(start, n_rows):
            rows = pl.ds(pl.multiple_of(start, sub), n_rows)
            acc_ref[rows, :] += _swiglu_step(xbuf_ref[rows, :], wgb_ref, wub_ref, wdb_ref)

        quads = count // 4

        def quad(i, carry):
            chain(i * (4 * sub), 4 * sub)
            return carry

        lax.fori_loop(0, quads, quad, 0)
        rest = count - 4 * quads

        @pl.when(rest >= 2)
        def _():
            chain(quads * (4 * sub), 2 * sub)

        @pl.when(rest % 2 == 1)
        def _():
            chain((count - 1) * sub, sub)

    @pl.when(f == nf - 1)
    def _():
        for j in range(c_max):
            @pl.when(j < count)
            def _(j=j):
                copy_out(j).start()
        for j in range(c_max):
            @pl.when(j < count)
            def _(j=j):
                copy_out(j).wait()


def moe_ffn(xs, vis_expert, vis_start, vis_count, vis_live, wg, wu, wd, sub, c_max, tf):
    n, d = xs.shape
    dff = wg.shape[2]
    nf = dff // tf
    n_vis = vis_expert.shape[0]
    fi = lambda v, f, live: jnp.where(live[v] > 0, f, nf - 1)
    grid_spec = pltpu.PrefetchScalarGridSpec(
        num_scalar_prefetch=4,
        grid=(n_vis, nf),
        in_specs=[pl.BlockSpec(memory_space=pl.ANY),
                  pl.BlockSpec((None, d, tf), lambda v, f, ve, vs, vc, lv: (ve[v], 0, fi(v, f, lv))),
                  pl.BlockSpec((None, d, tf), lambda v, f, ve, vs, vc, lv: (ve[v], 0, fi(v, f, lv))),
                  pl.BlockSpec((None, tf, d), lambda v, f, ve, vs, vc, lv: (ve[v], fi(v, f, lv), 0))],
        out_specs=pl.BlockSpec(memory_space=pl.ANY),
        scratch_shapes=[pltpu.VMEM((c_max * sub, d), BF16), pltpu.VMEM((c_max * sub, d), F32),
                        pltpu.VMEM((d, tf), BF16), pltpu.VMEM((d, tf), BF16),
                        pltpu.VMEM((tf, d), BF16),
                        pltpu.SemaphoreType.DMA(()), pltpu.SemaphoreType.DMA(())],
    )
    return pl.pallas_call(
        functools.partial(_moe_ffn_body, sub=sub, c_max=c_max, nf=nf),
        grid_spec=grid_spec,
        out_shape=jax.ShapeDtypeStruct((n, d), F32),
        compiler_params=_params(("arbitrary", "arbitrary")),
        name="moe_ffn",
    )(vis_expert, vis_start, vis_count, vis_live, xs, wg, wu, wd)


def _combine_rows(pos_ref, x_ref, gate_ref, y_hbm, buf_ref, sem, rows, n_tok):
    base = pl.program_id(0) * rows

    def start(r, c):
        _row_copy(y_hbm, pos_ref[base + r], buf_ref.at[0], r, sem).start()
        _row_copy(y_hbm, pos_ref[n_tok + base + r], buf_ref.at[1], r, sem).start()
        return c

    def wait(r, c):
        _row_copy(y_hbm, 0, buf_ref.at[0], r, sem).wait()
        _row_copy(y_hbm, 0, buf_ref.at[1], r, sem).wait()
        return c

    lax.fori_loop(0, rows, start, 0, unroll=8)
    lax.fori_loop(0, rows, wait, 0, unroll=8)
    gates = gate_ref[...]
    return x_ref[...] + (gates[:, 0:1] * buf_ref[0] + gates[:, 1:2] * buf_ref[1])


def _combine_body(pos_ref, x_ref, gate_ref, y_hbm, o_ref, buf_ref, sem, *, rows, n_tok):
    o_ref[...] = _combine_rows(pos_ref, x_ref, gate_ref, y_hbm, buf_ref, sem, rows, n_tok)


def _combine_norm_body(pos_ref, x_ref, gate_ref, g_ref, y_hbm, op_ref, os_ref, buf_ref, sem, *,
                       rows, n_tok, n_p):
    out = _rms(_combine_rows(pos_ref, x_ref, gate_ref, y_hbm, buf_ref, sem, rows, n_tok),
               g_ref[...])

    @pl.when(pl.program_id(0) < n_p)
    def _():
        op_ref[...] = out

    @pl.when(pl.program_id(0) >= n_p)
    def _():
        os_ref[...] = out


def moe_combine(x, gates, y, pos, out_norm=None, rows=256):
    n_tok, d = x.shape
    in_specs = [pl.BlockSpec((rows, d), lambda i, p: (i, 0)),
                pl.BlockSpec((rows, LANES), lambda i, p: (i, 0))]
    scratch = [pltpu.VMEM((2, rows, d), F32), pltpu.SemaphoreType.DMA(())]
    any_spec = pl.BlockSpec(memory_space=pl.ANY)
    if out_norm is None:
        return pl.pallas_call(
            functools.partial(_combine_body, rows=rows, n_tok=n_tok),
            grid_spec=pltpu.PrefetchScalarGridSpec(
                num_scalar_prefetch=1, grid=(n_tok // rows,),
                in_specs=in_specs + [any_spec],
                out_specs=pl.BlockSpec((rows, d), lambda i, p: (i, 0)),
                scratch_shapes=scratch),
            out_shape=jax.ShapeDtypeStruct((n_tok, d), F32),
            compiler_params=_params(("arbitrary",)),
            name="moe_combine",
        )(pos, x, gates, y)
    n_p = T_P // rows
    return pl.pallas_call(
        functools.partial(_combine_norm_body, rows=rows, n_tok=n_tok, n_p=n_p),
        grid_spec=pltpu.PrefetchScalarGridSpec(
            num_scalar_prefetch=1, grid=(n_tok // rows,),
            in_specs=in_specs + [pl.BlockSpec((1, d), lambda i, p: (0, 0)), any_spec],
            out_specs=[pl.BlockSpec((rows, d), lambda i, p: (jnp.minimum(i, n_p - 1), 0)),
                       pl.BlockSpec((rows, d), lambda i, p: (jnp.maximum(i - n_p, 0), 0))],
            scratch_shapes=scratch),
        out_shape=[jax.ShapeDtypeStruct((T_P, d), F32), jax.ShapeDtypeStruct((n_tok - T_P, d), F32)],
        compiler_params=_params(("arbitrary",)),
        name="moe_combine_norm",
    )(pos, x, gates, out_norm.reshape(1, d), y)


def moe_layer(x, a, wa, g, w_router, wg, wu, wd, out_norm=None, sub=256, c_max=10, tf=256):
    assert c_max >= N_EXPERTS
    n_tok = x.shape[0]
    i32 = jnp.int32
    x1, idx_full, gate_full, cnt = router(x, a, wa, g, w_router)
    experts = idx_full[:, 0:2]
    ranks = idx_full[:, 2:4]
    counts = cnt[0, :N_EXPERTS].astype(i32)
    n_assign = 2 * n_tok
    n_sb = n_assign // sub + N_EXPERTS
    n_vis = (n_sb + (c_max - 1) * N_EXPERTS) // c_max + 1

    e_ids = jnp.arange(N_EXPERTS, dtype=i32)
    sb_per = (counts + sub - 1) // sub
    sb_end = jnp.cumsum(sb_per)
    sb_start = sb_end - sb_per
    sb_used = sb_end[-1]
    row_start = sb_start * sub

    start_of = jnp.sum(jnp.where(experts[:, :, None] == e_ids, row_start, 0), axis=-1)
    pos = (start_of + ranks).astype(i32).T.reshape(-1)

    n_pad = n_sb * sub - n_assign
    pad_end = jnp.cumsum(sb_per * sub - counts)
    pad_ids = jnp.arange(n_pad, dtype=i32)
    pad_e = jnp.sum((pad_ids[:, None] >= pad_end[None, :]).astype(i32), axis=1)
    keys = jnp.concatenate([experts.reshape(-1) * 2, pad_e * 2 + 1])
    toks = jnp.concatenate([jnp.arange(n_assign, dtype=i32) // 2, (pad_ids * 8) % n_tok])
    _, src_rows = lax.sort((keys, toks), num_keys=1, is_stable=True)

    vis_per = (sb_per + c_max - 1) // c_max
    vis_end = jnp.cumsum(vis_per)
    vis_used = vis_end[-1]
    v_ids = jnp.arange(n_vis, dtype=i32)
    live = v_ids < vis_used
    v_eff = jnp.minimum(v_ids, vis_used - 1)
    vis_expert = jnp.sum((v_eff[:, None] >= vis_end[None, :]).astype(i32), axis=1)
    of_expert = vis_expert[:, None] == e_ids[None, :]
    pick = lambda table: jnp.sum(jnp.where(of_expert, table[None, :], 0), axis=1)
    nth = v_eff - (pick(vis_end) - pick(vis_per))
    vis_start = pick(sb_start) + c_max * nth
    vis_count = jnp.minimum(c_max, pick(sb_per) - c_max * nth)
    tail = v_ids == vis_used
    vis_start = jnp.where(live, vis_start, jnp.where(tail, sb_used, 0)).astype(i32)
    vis_count = jnp.where(live, vis_count, jnp.where(tail, n_sb - sb_used, 0)).astype(i32)

    xs = gather_norm(x1, g, src_rows.astype(i32))
    y = moe_ffn(xs, vis_expert.astype(i32), vis_start, vis_count, live.astype(i32), wg, wu, wd,
                sub, c_max, tf)
    return moe_combine(x1, gate_full, y, pos, out_norm)


def _final_norm_body(x_ref, g_ref, o_ref):
    o_ref[...] = _rms(x_ref[...], g_ref[...])


def final_norm(x, g, first_block, n_blocks, tm=512):
    d = x.shape[1]
    return pl.pallas_call(
        _final_norm_body,
        grid=(n_blocks,),
        in_specs=[pl.BlockSpec((tm, d), lambda i: (first_block + i, 0)),
                  pl.BlockSpec((1, d), lambda i: (0, 0))],
        out_specs=pl.BlockSpec((tm, d), lambda i: (i, 0)),
        out_shape=jax.ShapeDtypeStruct((n_blocks * tm, d), F32),
        compiler_params=_params(("parallel",)),
        name="final_norm",
    )(x, g.reshape(1, d))


def kernel(x_prompt, x_sample, cache_win_k, cache_win_v, state_conv, cache_mem_k, cache_mem_v,
           mem_prompt, norm_mix, w_in, attn_sinks, att_out_norm, conv_w, conv_b, conv_ln_g,
           conv_ln_b, conv_out_norm, w_out, norm_cross, norm_mem, w_xq, w_xk, w_xv, w_xo,
           norm_ffn, w_gate, w_up, w_down, w_router, we_gate, we_up, we_down, final_norm_g):
    x = (x_prompt.reshape(T_P, D_MODEL), x_sample.reshape(T_S, D_MODEL))
    mem = mem_prompt.reshape(N_MEM, D_MODEL)
    o1 = ATT_W
    o2 = o1 + KV_W
    o3 = o2 + KV_W
    o4 = o3 + CONV_C
    wb = cache_win_k.shape[2]
    mem_k_rows = cache_mem_k.reshape(DEPTH, DEC_BATCH, N_MEM * X_HEADS, X_HEAD_DIM)
    mem_v_rows = cache_mem_v.reshape(DEPTH, DEC_BATCH, N_MEM * X_HEADS, X_HEAD_DIM)
    pk, pv, pc, pmk, pmv, sk, sv, sc = [], [], [], [], [], [], [], []
    for l in range(DEPTH):
        wl = w_in[l]
        w_u = jnp.concatenate([wl[:, :o1], wl[:, o3:o4], wl[:, o4:], wl[:, o1:o2], wl[:, o2:o3]],
                              axis=1).astype(BF16)
        u = rms_matmul(x, norm_mix[l], w_u, tm=512, tn=U_W)
        ka, kg, kk, kv = ATT_W, ATT_W + CONV_C, ATT_W + 2 * CONV_C, ATT_W + 2 * CONV_C + KV_W

        att_p = swa_prompt(u, attn_sinks[l], att_out_norm[l])
        conv_p, state_p = conv_prompt(u, conv_w[l], conv_b[l], conv_ln_g[l], conv_ln_b[l],
                                      conv_out_norm[l])
        us = u[T_P:]
        q_s = us[:, :ATT_W].reshape(DEC_BATCH, DEC_SEQ, ATT_W)
        a_s = us[:, ka:kg].reshape(DEC_BATCH, DEC_SEQ, CONV_C)
        g_s = us[:, kg:kk].reshape(DEC_BATCH, DEC_SEQ, CONV_C)
        k_s = us[:, kk:kv].reshape(DEC_BATCH, DEC_SEQ, KV_W)
        v_s = us[:, kv:].reshape(DEC_BATCH, DEC_SEQ, KV_W)
        att_s, k_win, v_win = swa_sample(q_s, k_s, v_s,
                                         cache_win_k[l].reshape(DEC_BATCH, wb, KV_W),
                                         cache_win_v[l].reshape(DEC_BATCH, wb, KV_W),
                                         attn_sinks[l], att_out_norm[l], bt=16)
        conv_s, state_s = conv_sample(state_conv[l], a_s, g_s, conv_w[l], conv_b[l],
                                      conv_ln_g[l], conv_ln_b[l], conv_out_norm[l], bt=16)
        conv_s = conv_s.transpose(1, 0, 2).reshape(T_S, CONV_C)
        x = matmul_add(x, [(att_p, att_s.reshape(T_S, ATT_W)), (conv_p, conv_s)],
                       w_out[l].astype(BF16), tm=512, tn=D_MODEL)

        keep = min(WINDOW, SEQ)
        pk.append(u[T_P - keep:T_P, kk:kv].reshape(1, keep, N_KV, HEAD_DIM))
        pv.append(u[T_P - keep:T_P, kv:].reshape(1, keep, N_KV, HEAD_DIM))
        pc.append(state_p.reshape(1, STATE_ROWS, CONV_C))
        sk.append(k_win.reshape(DEC_BATCH, wb, N_KV, HEAD_DIM))
        sv.append(v_win.reshape(DEC_BATCH, wb, N_KV, HEAD_DIM))
        sc.append(state_s)

        mk = rms_matmul(mem, norm_mem[l], w_xk[l].astype(BF16), tm=N_MEM, tn=X_W)
        mv = rms_matmul(mem, norm_mem[l], w_xv[l].astype(BF16), tm=N_MEM, tn=X_W)
        pmk.append(mk.reshape(1, N_MEM, X_HEADS, X_HEAD_DIM))
        pmv.append(mv.reshape(1, N_MEM, X_HEADS, X_HEAD_DIM))
        qx = rms_matmul(x, norm_cross[l], w_xq[l].astype(BF16), tm=512, tn=X_W)
        o_p = cross_prompt(qx, mk, mv)
        q_ht = qx[T_P:].reshape(DEC_BATCH, DEC_SEQ, X_HEADS, X_HEAD_DIM).transpose(0, 2, 1, 3)
        o_s = cross_sample(q_ht.reshape(DEC_BATCH, X_HEADS * DEC_SEQ, X_HEAD_DIM), mem_k_rows,
                           mem_v_rows, l)
        o_s = o_s.reshape(DEC_BATCH, X_HEADS, DEC_SEQ, X_HEAD_DIM).transpose(0, 2, 1, 3)
        o_all = jnp.concatenate([o_p, o_s.reshape(T_S, X_W)], axis=0)
        wxo = w_xo[l].astype(BF16)

        if l % 2 == 0:
            d = l // 2
            x = dense_ffn(x, o_all, wxo, norm_ffn[l], w_gate[d], w_up[d], w_down[d], tm=1088, tf=256)
        else:
            m = l // 2
            last = l == DEPTH - 1
            x = moe_layer(x, o_all, wxo, norm_ffn[l], w_router[m], we_gate[m], we_up[m], we_down[m],
                          out_norm=final_norm_g if last else None)

    if isinstance(x, tuple):
        y_prompt, y_sample = x
    else:
        y_prompt = final_norm(x, final_norm_g, 0, T_P // 512)
        y_sample = final_norm(x, final_norm_g, T_P // 512, T_S // 512)
    y_prompt = y_prompt.reshape(1, SEQ, D_MODEL)
    y_sample = y_sample.reshape(DEC_BATCH, DEC_SEQ, D_MODEL)
    return (y_prompt, y_sample, jnp.stack(pk), jnp.stack(pv), jnp.stack(pc), jnp.stack(pmk),
            jnp.stack(pmv), jnp.stack(sk), jnp.stack(sv), jnp.stack(sc))
```

```python
import functools

import jax
import jax.numpy as jnp
import numpy as np
from jax import lax
from jax.experimental import pallas as pl
from jax.experimental.pallas import tpu as pltpu

F32 = jnp.float32
BF16 = jnp.bfloat16

D_MODEL = 2048
SEQ = 8192
DEPTH = 2
DEC_BATCH = 128
DEC_SEQ = 4
HEAD_DIM = 64
ATT_W = 1024
N_HEADS = 16
N_KV = 4
KV_W = N_KV * HEAD_DIM
WINDOW = 128
CONV_C = 1024
CONV_W = 31
N_MEM = 256
X_HEADS = 4
X_HEAD_DIM = 128
X_W = X_HEADS * X_HEAD_DIM
D_FF = 5632
N_EXPERTS = 8
D_FF_E = 7168
EPS = 1e-6
NEG = -1e30

T_P = SEQ
T_S = DEC_BATCH * DEC_SEQ
T_ALL = T_P + T_S

LANES = 128
SUBLANES = 8
HALF = HEAD_DIM
STATE_ROWS = CONV_W - 1
STATE_PAD = 32
KEYS_PAD = 256
VMEM_LIMIT = 56 * 1024 * 1024

SLOPES = [float(2.0 ** (-8.0 * (h + 1) / N_HEADS)) for h in range(N_HEADS)]

U_W = ATT_W + 2 * CONV_C + 2 * KV_W


def _params(sem):
    return pltpu.CompilerParams(dimension_semantics=sem, vmem_limit_bytes=VMEM_LIMIT)


def _rms(x, g):
    r = lax.rsqrt(jnp.mean(x * x, axis=-1, keepdims=True) + EPS)
    return x * r * g


def _sigmoid(x):
    return 1.0 / (1.0 + jnp.exp(-x))


def _parts(x):
    return x if isinstance(x, tuple) else (x,)


def _n_rows(parts):
    return sum(p.shape[0] for p in parts)


def _row_specs(parts, tm, width, col):
    if len(parts) == 1:
        return [pl.BlockSpec((tm, width), lambda i, j: (i, col(j)))]
    n_p = parts[0].shape[0] // tm
    return [pl.BlockSpec((tm, width), lambda i, j: (jnp.minimum(i, n_p - 1), col(j))),
            pl.BlockSpec((tm, width), lambda i, j: (jnp.maximum(i - n_p, 0), col(j)))]


def _row_value(refs, n_p):
    if len(refs) == 1:
        return refs[0][...]
    return jnp.where(pl.program_id(0) < n_p, refs[0][...], refs[1][...])


def _rms_matmul_body(*refs, n_x, n_p):
    x_refs = refs[:n_x]
    g_ref, w_ref, o_ref, xn_ref = refs[n_x:]

    @pl.when(pl.program_id(1) == 0)
    def _():
        xn_ref[...] = _rms(_row_value(x_refs, n_p), g_ref[...]).astype(BF16)

    o_ref[...] = jnp.dot(xn_ref[...], w_ref[...], preferred_element_type=F32).astype(o_ref.dtype)


def rms_matmul(x, g, w, tm, tn, out_dtype=F32):
    parts = _parts(x)
    m, k = _n_rows(parts), parts[0].shape[1]
    n = w.shape[1]
    return pl.pallas_call(
        functools.partial(_rms_matmul_body, n_x=len(parts), n_p=parts[0].shape[0] // tm),
        grid=(m // tm, n // tn),
        in_specs=_row_specs(parts, tm, k, lambda j: 0) + [
            pl.BlockSpec((1, k), lambda i, j: (0, 0)),
            pl.BlockSpec((k, tn), lambda i, j: (0, j))],
        out_specs=pl.BlockSpec((tm, tn), lambda i, j: (i, j)),
        out_shape=jax.ShapeDtypeStruct((m, n), out_dtype),
        scratch_shapes=[pltpu.VMEM((tm, k), BF16)],
        compiler_params=_params(("parallel", "arbitrary")),
        name="rms_matmul",
    )(*parts, g.reshape(1, k), w)


def _matmul_add_body(*refs, counts, n_p, n_next):
    n_out = 1 + (n_next > 0)
    n_terms = len(counts) - 1
    n_in = len(refs) - n_out
    w_refs = refs[n_in - n_next - n_terms:n_in - n_next]
    at = counts[0]
    acc = _row_value(refs[:at], n_p)
    for t in range(n_terms):
        a = _row_value(refs[at:at + counts[1 + t]], n_p)
        at += counts[1 + t]
        acc = acc + jnp.dot(a, w_refs[t][...], preferred_element_type=F32)
    refs[n_in][...] = acc
    if n_next:
        g_ref, w2_ref = refs[n_in - 2:n_in]
        refs[n_in + 1][...] = jnp.dot(_rms(acc, g_ref[...]).astype(BF16), w2_ref[...],
                                      preferred_element_type=F32)


def matmul_add(x, a_list, w, tm, tn, then=None):
    x_parts = _parts(x)
    a_parts = [_parts(a) for a in a_list]
    m, n = _n_rows(x_parts), x_parts[0].shape[1]
    in_specs = _row_specs(x_parts, tm, tn, lambda j: j)
    operands = list(x_parts)
    for parts in a_parts:
        in_specs += _row_specs(parts, tm, parts[0].shape[1], lambda j: 0)
        operands += list(parts)
    row = 0
    for parts in a_parts:
        ka = parts[0].shape[1]
        in_specs.append(pl.BlockSpec((ka, tn), lambda i, j, blk=row // ka: (blk, j)))
        operands.append(w)
        row += ka
    out_specs = [pl.BlockSpec((tm, tn), lambda i, j: (i, j))]
    out_shape = [jax.ShapeDtypeStruct((m, n), F32)]
    if then is not None:
        assert tn == n
        gain, w2 = then
        n2 = w2.shape[1]
        in_specs += [pl.BlockSpec((1, n), lambda i, j: (0, 0)),
                     pl.BlockSpec((n, n2), lambda i, j: (0, 0))]
        operands += [gain.reshape(1, n), w2]
        out_specs.append(pl.BlockSpec((tm, n2), lambda i, j: (i, 0)))
        out_shape.append(jax.ShapeDtypeStruct((m, n2), F32))
    counts = (len(x_parts),) + tuple(len(p) for p in a_parts)
    out = pl.pallas_call(
        functools.partial(_matmul_add_body, counts=counts, n_p=T_P // tm,
                          n_next=0 if then is None else 2),
        grid=(m // tm, n // tn),
        in_specs=in_specs,
        out_specs=out_specs,
        out_shape=out_shape,
        input_output_aliases={0: 0} if len(x_parts) == 1 else {},
        compiler_params=_params(("parallel", "parallel")),
        name="matmul_add",
    )(*operands)
    return out[0] if then is None else out


def _half_mask(shape, half):
    lane = lax.broadcasted_iota(jnp.int32, shape, len(shape) - 1)
    return (lane >= HALF) if half else (lane < HALF)


def _sink_softmax_pv(s, sink, v_tile, batched):
    m = jnp.maximum(jnp.max(s, axis=-1, keepdims=True), sink)
    p = jnp.exp(s - m)
    den = jnp.sum(p, axis=-1, keepdims=True) + jnp.exp(sink - m)
    if batched:
        o = jnp.einsum("bqk,bkd->bqd", p.astype(BF16), v_tile, preferred_element_type=F32)
    else:
        o = jnp.dot(p.astype(BF16), v_tile, preferred_element_type=F32)
    return o / den


def _swa_prompt_body(sink_ref, q_ref, kp_ref, kc_ref, vp_ref, vc_ref, ga_ref, o_ref):
    i = pl.program_id(0)
    blk = WINDOW
    q = q_ref[...] * (HEAD_DIM ** -0.5)
    k = jnp.concatenate([kp_ref[...], kc_ref[...]], axis=0).astype(BF16)
    v = jnp.concatenate([vp_ref[...], vc_ref[...]], axis=0).astype(BF16)
    row = lax.broadcasted_iota(jnp.int32, (blk, 2 * blk), 0)
    col = lax.broadcasted_iota(jnp.int32, (blk, 2 * blk), 1)
    dist = blk + row - col
    valid = (dist >= 0) & (dist <= WINDOW) & ((col >= blk) | (i > 0))
    distf = dist.astype(F32)
    tiles = [None] * (N_HEADS // 2)
    for n in range(N_KV):
        half = n % 2
        ts = slice((n // 2) * LANES, (n // 2 + 1) * LANES)
        k_m = jnp.where(_half_mask((2 * blk, LANES), half), k[:, ts], jnp.zeros((), BF16))
        v_t = v[:, ts]
        q_al = jnp.concatenate([q[:, (2 * n) * LANES:(2 * n + 1) * LANES],
                                q[:, (2 * n + 1) * LANES:(2 * n + 2) * LANES]], axis=0)
        q_mis = pltpu.roll(q_al, HALF, 1)
        lhs = jnp.concatenate([q_al, q_mis], axis=0).astype(BF16)
        s = lax.dot_general(lhs, k_m, (((1,), (1,)), ((), ())), preferred_element_type=F32)
        outs = []
        for b4 in range(4):
            a, j = divmod(b4, 2)
            h = 4 * n + 2 * j + (half if a == 0 else 1 - half)
            sb = s[b4 * blk:(b4 + 1) * blk]
            sb = jnp.where(valid, sb - SLOPES[h] * distf, NEG)
            outs.append(_sink_softmax_pv(sb, sink_ref[h], v_t, False))
        keep = _half_mask((blk, LANES), half)
        for j in range(2):
            tiles[2 * n + j] = jnp.where(keep, outs[j], pltpu.roll(outs[2 + j], HALF, 1))
    att = jnp.concatenate(tiles, axis=1)
    o_ref[...] = _rms(att, ga_ref[...]).astype(o_ref.dtype)


def swa_prompt(u, sinks, g_att):
    blk = WINDOW
    nb = T_P // blk
    kcol = (ATT_W + 2 * CONV_C) // KV_W
    prev = lambda i: jnp.maximum(i - 1, 0)
    return pl.pallas_call(
        _swa_prompt_body,
        grid=(nb,),
        in_specs=[pl.BlockSpec(memory_space=pltpu.SMEM),
                  pl.BlockSpec((blk, ATT_W), lambda i: (i, 0)),
                  pl.BlockSpec((blk, KV_W), lambda i: (prev(i), kcol)),
                  pl.BlockSpec((blk, KV_W), lambda i: (i, kcol)),
                  pl.BlockSpec((blk, KV_W), lambda i: (prev(i), kcol + 1)),
                  pl.BlockSpec((blk, KV_W), lambda i: (i, kcol + 1)),
                  pl.BlockSpec((1, ATT_W), lambda i: (0, 0))],
        out_specs=pl.BlockSpec((blk, ATT_W), lambda i: (i, 0)),
        out_shape=jax.ShapeDtypeStruct((T_P, ATT_W), BF16),
        compiler_params=_params(("parallel",)),
        name="swa_prompt",
    )(sinks, u, u, u, u, u, g_att.reshape(1, ATT_W))


def _layer_view(out_ref, layer):
    if len(out_ref.shape) == 3:
        return out_ref
    for other in range(out_ref.shape[0]):
        if other != layer:
            out_ref[other] = jnp.zeros(out_ref.shape[1:], out_ref.dtype)
    return out_ref.at[layer]


def _swa_sample_body(sink_ref, q_ref, kn_ref, vn_ref, kc_ref, vc_ref, bias_ref, ga_ref, *rest,
                     layer):
    o_ref, ko_ref, vo_ref, kall_ref, vall_ref = rest[-5:]
    bt = q_ref.shape[0]
    wb = kc_ref.shape[2]
    for new_ref, cache_ref, out_ref, all_ref in ((kn_ref, kc_ref, ko_ref, kall_ref),
                                                 (vn_ref, vc_ref, vo_ref, vall_ref)):
        out_ref = _layer_view(out_ref, layer)
        all_ref[:, wb:wb + DEC_SEQ, :] = new_ref[...]
        all_ref[:, wb + DEC_SEQ:, :] = jnp.zeros((bt, KEYS_PAD - wb - DEC_SEQ, KV_W), F32)
        for b in range(bt):
            all_ref[b, 0:wb, :] = cache_ref[b].T
        for b in range(bt):
            out_ref[b] = all_ref[b, DEC_SEQ:DEC_SEQ + wb, :].T

    q = q_ref[...] * (HEAD_DIM ** -0.5)
    k = kall_ref[...].astype(BF16)
    v = vall_ref[...].astype(BF16)
    tiles = [None] * (N_HEADS // 2)
    for n in range(N_KV):
        half = n % 2
        ts = slice((n // 2) * LANES, (n // 2 + 1) * LANES)
        k_m = jnp.where(_half_mask((bt, KEYS_PAD, LANES), half), k[:, :, ts], jnp.zeros((), BF16))
        v_t = v[:, :, ts]
        q_al = jnp.concatenate([q[:, :, (2 * n) * LANES:(2 * n + 1) * LANES],
                                q[:, :, (2 * n + 1) * LANES:(2 * n + 2) * LANES]], axis=1)
        q_mis = pltpu.roll(q_al, HALF, 2)
        lhs = jnp.concatenate([q_al, q_mis], axis=1).astype(BF16)
        s = jnp.einsum("bqd,bkd->bqk", lhs, k_m, preferred_element_type=F32)
        bias = bias_ref[n]
        s = jnp.where(bias > 0.5 * NEG, s + bias, NEG)
        sink = sink_ref[n]
        o = _sink_softmax_pv(s, sink, v_t, True)
        keep = _half_mask((bt, DEC_SEQ, LANES), half)
        for j in range(2):
            o_al = o[:, j * DEC_SEQ:(j + 1) * DEC_SEQ]
            o_mis = pltpu.roll(o[:, (2 + j) * DEC_SEQ:(3 + j) * DEC_SEQ], HALF, 2)
            tiles[2 * n + j] = jnp.where(keep, o_al, o_mis)
    att = jnp.concatenate(tiles, axis=2)
    o_ref[...] = _rms(att, ga_ref[...]).astype(o_ref.dtype)


def _sample_bias_and_sinks(sinks):
    wb = WINDOW
    bias = np.full((N_KV, 4 * DEC_SEQ, KEYS_PAD), NEG, np.float32)
    head = np.zeros((N_KV, 4 * DEC_SEQ), np.int32)
    for n in range(N_KV):
        half = n % 2
        for a in range(2):
            for j in range(2):
                h = 4 * n + 2 * j + (half if a == 0 else 1 - half)
                for t in range(DEC_SEQ):
                    r = (2 * a + j) * DEC_SEQ + t
                    head[n, r] = h
                    for kk in range(wb + DEC_SEQ):
                        d = t + wb - kk
                        if 0 <= d <= WINDOW:
                            bias[n, r, kk] = -SLOPES[h] * d
    sink_rows = sinks[jnp.asarray(head)][..., None]
    return jnp.asarray(bias), sink_rows


def swa_sample(q_s, k_new, v_new, cache_kt, cache_vt, layer, prev_out, sinks, g_att, bt):
    wb = cache_kt.shape[3]
    bias, sink_rows = _sample_bias_and_sinks(sinks)
    nq = 4 * DEC_SEQ
    b3 = lambda i: (i, 0, 0)
    z3 = lambda i: (0, 0, 0)
    cache_spec = pl.BlockSpec((None, bt, KV_W, wb), lambda i: (layer, i, 0, 0))
    in_specs = [pl.BlockSpec((N_KV, nq, 1), z3),
                pl.BlockSpec((bt, DEC_SEQ, ATT_W), b3),
                pl.BlockSpec((bt, DEC_SEQ, KV_W), b3),
                pl.BlockSpec((bt, DEC_SEQ, KV_W), b3),
                cache_spec, cache_spec,
                pl.BlockSpec((N_KV, nq, KEYS_PAD), z3),
                pl.BlockSpec((1, 1, ATT_W), z3)]
    operands = [sink_rows, q_s, k_new, v_new, cache_kt, cache_vt, bias, g_att.reshape(1, 1, ATT_W)]
    aliases = {}
    out_cache_spec = cache_spec
    if prev_out is None:
        out_cache_spec = pl.BlockSpec((DEPTH, bt, KV_W, wb), lambda i: (0, i, 0, 0))
    else:
        in_specs += [pl.BlockSpec(memory_space=pl.ANY)] * 2
        operands += list(prev_out)
        aliases = {len(operands) - 2: 1, len(operands) - 1: 2}
    return pl.pallas_call(
        functools.partial(_swa_sample_body, layer=layer),
        grid=(DEC_BATCH // bt,),
        in_specs=in_specs,
        out_specs=[pl.BlockSpec((bt, DEC_SEQ, ATT_W), b3), out_cache_spec, out_cache_spec],
        out_shape=[jax.ShapeDtypeStruct((DEC_BATCH, DEC_SEQ, ATT_W), BF16),
                   jax.ShapeDtypeStruct(cache_kt.shape, F32),
                   jax.ShapeDtypeStruct(cache_vt.shape, F32)],
        input_output_aliases=aliases,
        scratch_shapes=[pltpu.VMEM((bt, KEYS_PAD, KV_W), F32),
                        pltpu.VMEM((bt, KEYS_PAD, KV_W), F32)],
        compiler_params=_params(("parallel",)),
        name="swa_sample",
    )(*operands)


def _conv_post(y, cb, lg, lb, gc):
    y = y + cb
    mu = jnp.mean(y, axis=-1, keepdims=True)
    yc = y - mu
    z = yc * lax.rsqrt(jnp.mean(yc * yc, axis=-1, keepdims=True) + EPS) * lg + lb
    c = z * _sigmoid(z)
    return _rms(c, gc)


def _conv_prompt_body(ap_ref, gp_ref, a_ref, g_ref, cw_ref, cb_ref, lg_ref, lb_ref, gc_ref,
                      o_ref, st_ref, ext_ref, y_ref, *, tt, tc):
    i = pl.program_id(0)
    glu_prev = ap_ref[...] * _sigmoid(gp_ref[...])
    ext_ref[0:STATE_PAD, :] = jnp.where(i > 0, glu_prev, 0.0)
    glu = a_ref[...] * _sigmoid(g_ref[...])
    ext_ref[STATE_PAD:STATE_PAD + tt, :] = glu
    st_ref[...] = glu[tt - STATE_PAD:]
    first = STATE_PAD - STATE_ROWS
    for cblk in range(CONV_C // LANES):
        cs = slice(cblk * LANES, (cblk + 1) * LANES)
        for t0 in range(0, tt, tc):
            total = None
            for lo in range(SUBLANES):
                n_hi = (CONV_W - lo + SUBLANES - 1) // SUBLANES
                start = t0 + first + lo
                win = ext_ref[start:start + tc + SUBLANES * (n_hi - 1), cs]
                acc = None
                for hi in range(n_hi):
                    j = SUBLANES * hi + lo
                    term = win[SUBLANES * hi:SUBLANES * hi + tc] * cw_ref[j:j + 1, cs]
                    acc = term if acc is None else acc + term
                total = acc if total is None else total + acc
            y_ref[t0:t0 + tc, cs] = total
    o_ref[...] = _conv_post(y_ref[...], cb_ref[...], lg_ref[...], lb_ref[...],
                            gc_ref[...]).astype(o_ref.dtype)


def conv_prompt(u, conv_w, conv_b, ln_g, ln_b, g_conv, tt=128, tc=64):
    nt = T_P // tt
    per = tt // STATE_PAD
    prev = lambda i: jnp.maximum(i * per - 1, 0)
    acol, gcol = ATT_W // CONV_C, ATT_W // CONV_C + 1
    vec = lambda: pl.BlockSpec((1, CONV_C), lambda i: (0, 0))
    out, state = pl.pallas_call(
        functools.partial(_conv_prompt_body, tt=tt, tc=tc),
        grid=(nt,),
        in_specs=[pl.BlockSpec((STATE_PAD, CONV_C), lambda i: (prev(i), acol)),
                  pl.BlockSpec((STATE_PAD, CONV_C), lambda i: (prev(i), gcol)),
                  pl.BlockSpec((tt, CONV_C), lambda i: (i, acol)),
                  pl.BlockSpec((tt, CONV_C), lambda i: (i, gcol)),
                  pl.BlockSpec((CONV_W, CONV_C), lambda i: (0, 0)),
                  vec(), vec(), vec(), vec()],
        out_specs=[pl.BlockSpec((tt, CONV_C), lambda i: (i, 0)),
                   pl.BlockSpec((STATE_PAD, CONV_C), lambda i: (0, 0))],
        out_shape=[jax.ShapeDtypeStruct((T_P, CONV_C), BF16),
                   jax.ShapeDtypeStruct((STATE_PAD, CONV_C), F32)],
        scratch_shapes=[pltpu.VMEM((STATE_PAD + tt, CONV_C), F32),
                        pltpu.VMEM((tt, CONV_C), F32)],
        compiler_params=_params(("arbitrary",)),
        name="conv_prompt",
    )(u, u, u, u, conv_w, conv_b.reshape(1, -1), ln_g.reshape(1, -1), ln_b.reshape(1, -1),
      g_conv.reshape(1, -1))
    return out, state[STATE_PAD - STATE_ROWS:]


def _conv_sample_body(st_ref, *rest, layer):
    ag_refs = rest[:2 * DEC_SEQ]
    cw_ref, cb_ref, lg_ref, lb_ref, gc_ref = rest[2 * DEC_SEQ:2 * DEC_SEQ + 5]
    o_ref, so_ref = rest[-2:]
    so_ref = _layer_view(so_ref, layer)
    glu =[ag_refs[t][...] * _sigmoid(ag_refs[DEC_SEQ + t][...]) for t in range(DEC_SEQ)]

    def slab(r):
        return st_ref[r] if r < STATE_ROWS else glu[r - STATE_ROWS]

    for r in range(STATE_ROWS):
        so_ref[r] = slab(r + DEC_SEQ)
    for t in range(DEC_SEQ):
        y = None
        for j in range(CONV_W):
            term = slab(t + j) * cw_ref[j:j + 1, :]
            y = term if y is None else y + term
        o_ref[t] = _conv_post(y, cb_ref[...], lg_ref[...], lb_ref[...],
                              gc_ref[...]).astype(o_ref.dtype)


def conv_sample(state_rows, u, layer, prev_out, conv_w, conv_b, ln_g, ln_b, g_conv, bt):
    acol, gcol = ATT_W // CONV_C, ATT_W // CONV_C + 1
    vec = lambda: pl.BlockSpec((1, CONV_C), lambda i: (0, 0))
    st_spec = pl.BlockSpec((None, STATE_ROWS, bt, CONV_C), lambda i: (layer, 0, i, 0))
    step_spec = lambda t, col: pl.BlockSpec(
        (bt, CONV_C), lambda i: ((T_P + t * DEC_BATCH) // bt + i, col))
    in_specs = ([st_spec] + [step_spec(t, acol) for t in range(DEC_SEQ)]
                + [step_spec(t, gcol) for t in range(DEC_SEQ)]
                + [pl.BlockSpec((CONV_W, CONV_C), lambda i: (0, 0)), vec(), vec(), vec(), vec()])
    operands = [state_rows] + [u] * (2 * DEC_SEQ) + [
        conv_w, conv_b.reshape(1, -1), ln_g.reshape(1, -1), ln_b.reshape(1, -1),
        g_conv.reshape(1, -1)]
    aliases = {}
    out_st_spec = st_spec
    if prev_out is None:
        out_st_spec = pl.BlockSpec((DEPTH, STATE_ROWS, bt, CONV_C), lambda i: (0, 0, i, 0))
    else:
        in_specs.append(pl.BlockSpec(memory_space=pl.ANY))
        operands.append(prev_out)
        aliases = {len(operands) - 1: 1}
    return pl.pallas_call(
        functools.partial(_conv_sample_body, layer=layer),
        grid=(DEC_BATCH // bt,),
        in_specs=in_specs,
        out_specs=[pl.BlockSpec((DEC_SEQ, bt, CONV_C), lambda i: (0, i, 0)), out_st_spec],
        out_shape=[jax.ShapeDtypeStruct((DEC_SEQ, DEC_BATCH, CONV_C), BF16),
                   jax.ShapeDtypeStruct(state_rows.shape, F32)],
        input_output_aliases=aliases,
        compiler_params=_params(("parallel",)),
        name="conv_sample",
    )(*operands)


def _cross_prompt_body(q_ref, k_ref, v_ref, o_ref):
    q = q_ref[...].astype(BF16)
    k = k_ref[...].astype(BF16)
    v = v_ref[...].astype(BF16)
    outs = []
    for h in range(X_HEADS):
        hs = slice(h * X_HEAD_DIM, (h + 1) * X_HEAD_DIM)
        s = lax.dot_general(q[:, hs], k[:, hs], (((1,), (1,)), ((), ())),
                            preferred_element_type=F32) * (X_HEAD_DIM ** -0.5)
        p = jnp.exp(s - jnp.max(s, axis=-1, keepdims=True))
        den = jnp.sum(p, axis=-1, keepdims=True)
        outs.append(jnp.dot(p.astype(BF16), v[:, hs], preferred_element_type=F32) / den)
    o_ref[...] = jnp.concatenate(outs, axis=1).astype(o_ref.dtype)


def cross_prompt(q, mem_k, mem_v, tq=512):
    return pl.pallas_call(
        _cross_prompt_body,
        grid=(T_P // tq,),
        in_specs=[pl.BlockSpec((tq, X_W), lambda i: (i, 0)),
                  pl.BlockSpec((N_MEM, X_W), lambda i: (0, 0)),
                  pl.BlockSpec((N_MEM, X_W), lambda i: (0, 0))],
        out_specs=pl.BlockSpec((tq, X_W), lambda i: (i, 0)),
        out_shape=jax.ShapeDtypeStruct((T_P, X_W), BF16),
        compiler_params=_params(("parallel",)),
        name="cross_prompt",
    )(q, mem_k, mem_v)


def _cross_sample_body(q_ref, k_ref, v_ref, o_ref):
    q = q_ref[...].astype(BF16)
    k = k_ref[...].astype(BF16)
    v = v_ref[...].astype(BF16)
    s = jnp.einsum("bqd,bkd->bqk", q, k, preferred_element_type=F32) * (X_HEAD_DIM ** -0.5)
    row = lax.broadcasted_iota(jnp.int32, s.shape[1:], 0)
    col = lax.broadcasted_iota(jnp.int32, s.shape[1:], 1)
    same_head = (col % X_HEADS) == (row // DEC_SEQ)
    s = jnp.where(same_head, s, NEG)
    p = jnp.exp(s - jnp.max(s, axis=-1, keepdims=True))
    den = jnp.sum(p, axis=-1, keepdims=True)
    o = jnp.einsum("bqk,bkd->bqd", p.astype(BF16), v, preferred_element_type=F32) / den
    o_ref[...] = o.astype(o_ref.dtype)


def cross_sample(q_ht, mem_k, mem_v, layer, bt=8):
    rows = X_HEADS * DEC_SEQ
    b3 = lambda i: (i, 0, 0)
    b4 = lambda i: (layer, i, 0, 0)
    return pl.pallas_call(
        _cross_sample_body,
        grid=(DEC_BATCH // bt,),
        in_specs=[pl.BlockSpec((bt, rows, X_HEAD_DIM), b3),
                  pl.BlockSpec((None, bt, N_MEM * X_HEADS, X_HEAD_DIM), b4),
                  pl.BlockSpec((None, bt, N_MEM * X_HEADS, X_HEAD_DIM), b4)],
        out_specs=pl.BlockSpec((bt, rows, X_HEAD_DIM), b3),
        out_shape=jax.ShapeDtypeStruct((DEC_BATCH, rows, X_HEAD_DIM), BF16),
        compiler_params=_params(("parallel",)),
        name="cross_sample",
    )(q_ht, mem_k, mem_v)


def _swiglu_step(x, wg_ref, wu_ref, wd_ref):
    hg = jnp.dot(x, wg_ref[...].astype(BF16), preferred_element_type=F32)
    hu = jnp.dot(x, wu_ref[...].astype(BF16), preferred_element_type=F32)
    h = (hg * _sigmoid(hg) * hu).astype(BF16)
    return jnp.dot(h, wd_ref[...].astype(BF16), preferred_element_type=F32)


def _dense_ffn_body(x_ref, a_ref, wa_ref, g_ref, wg_ref, wu_ref, wd_ref, o_ref, xn_ref):
    @pl.when(pl.program_id(1) == 0)
    def _():
        x = x_ref[...] + jnp.dot(a_ref[...], wa_ref[...], preferred_element_type=F32)
        xn_ref[...] = _rms(x, g_ref[...]).astype(BF16)
        o_ref[...] = x

    o_ref[...] += _swiglu_step(xn_ref[...], wg_ref, wu_ref, wd_ref)


def dense_ffn(x, a, wa, g, wg, wu, wd, tm, tf):
    m, d = x.shape
    ka = a.shape[1]
    dff = wg.shape[1]
    return pl.pallas_call(
        _dense_ffn_body,
        grid=(m // tm, dff // tf),
        in_specs=[pl.BlockSpec((tm, d), lambda i, f: (i, 0), pipeline_mode=pl.Buffered(1)),
                  pl.BlockSpec((tm, ka), lambda i, f: (i, 0)),
                  pl.BlockSpec((ka, d), lambda i, f: (0, 0)),
                  pl.BlockSpec((1, d), lambda i, f: (0, 0)),
                  pl.BlockSpec((d, tf), lambda i, f: (0, f)),
                  pl.BlockSpec((d, tf), lambda i, f: (0, f)),
                  pl.BlockSpec((tf, d), lambda i, f: (f, 0))],
        out_specs=pl.BlockSpec((tm, d), lambda i, f: (i, 0)),
        out_shape=jax.ShapeDtypeStruct((m, d), F32),
        scratch_shapes=[pltpu.VMEM((tm, d), BF16)],
        compiler_params=_params(("parallel", "arbitrary")),
        name="dense_ffn",
    )(x, a, wa, g.reshape(1, d), wg, wu, wd)


def _router_body(x_ref, a_ref, wa_ref, g_ref, wr_ref, x1_ref, idx_ref, gate_ref, cnt_ref,
                 carry_ref):
    @pl.when(pl.program_id(0) == 0)
    def _():
        carry_ref[...] = jnp.zeros(carry_ref.shape, F32)

    x1 = x_ref[...] + jnp.dot(a_ref[...], wa_ref[...], preferred_element_type=F32)
    x1_ref[...] = x1
    xn = _rms(x1, g_ref[...])
    logits = jnp.dot(xn, wr_ref[...], preferred_element_type=F32,
                     precision=lax.Precision.HIGHEST)
    tm = logits.shape[0]
    lane = lax.broadcasted_iota(jnp.int32, logits.shape, 1)
    lg = jnp.where(lane < N_EXPERTS, logits, -jnp.inf)
    m1 = jnp.max(lg, axis=-1, keepdims=True)
    i1 = jnp.min(jnp.where(lg == m1, lane, LANES), axis=-1, keepdims=True)
    lg2 = jnp.where(lane == i1, -jnp.inf, lg)
    m2 = jnp.max(lg2, axis=-1, keepdims=True)
    i2 = jnp.min(jnp.where(lg2 == m2, lane, LANES), axis=-1, keepdims=True)
    e = jnp.exp(m2 - m1)
    den = 1.0 + e
    gate_ref[...] = jnp.where(lane == 0, 1.0 / den, jnp.where(lane == 1, e / den, 0.0))

    chosen = (lane == i1) | (lane == i2)
    r = lax.broadcasted_iota(jnp.int32, (tm, tm), 0)
    c = lax.broadcasted_iota(jnp.int32, (tm, tm), 1)
    earlier = jnp.where(c < r, 1.0, 0.0).astype(BF16)
    before = jnp.dot(earlier, jnp.where(chosen, 1.0, 0.0).astype(BF16),
                     preferred_element_type=F32) + carry_ref[...]
    r1 = jnp.sum(jnp.where(lane == i1, before, 0.0), axis=-1, keepdims=True).astype(jnp.int32)
    r2 = jnp.sum(jnp.where(lane == i2, before, 0.0), axis=-1, keepdims=True).astype(jnp.int32)
    idx_ref[...] = jnp.where(lane == 0, i1, jnp.where(lane == 1, i2,
                             jnp.where(lane == 2, r1, jnp.where(lane == 3, r2, 0))))
    carry_ref[...] += jnp.sum(jnp.where(chosen, 1.0, 0.0), axis=0, keepdims=True)
    cnt_ref[...] = carry_ref[...]


def router(x, a, wa, g, w_router, tm=512):
    m, d = x.shape
    ka = a.shape[1]
    wr = jnp.zeros((d, LANES), F32).at[:, :N_EXPERTS].set(w_router)
    return pl.pallas_call(
        _router_body,
        grid=(m // tm,),
        in_specs=[pl.BlockSpec((tm, d), lambda i: (i, 0)),
                  pl.BlockSpec((tm, ka), lambda i: (i, 0)),
                  pl.BlockSpec((ka, d), lambda i: (0, 0)),
                  pl.BlockSpec((1, d), lambda i: (0, 0)),
                  pl.BlockSpec((d, LANES), lambda i: (0, 0))],
        out_specs=[pl.BlockSpec((tm, d), lambda i: (i, 0)),
                   pl.BlockSpec((tm, LANES), lambda i: (i, 0)),
                   pl.BlockSpec((tm, LANES), lambda i: (i, 0)),
                   pl.BlockSpec((1, LANES), lambda i: (0, 0))],
        out_shape=[jax.ShapeDtypeStruct((m, d), F32),
                   jax.ShapeDtypeStruct((m, LANES), jnp.int32),
                   jax.ShapeDtypeStruct((m, LANES), F32),
                   jax.ShapeDtypeStruct((1, LANES), F32)],
        scratch_shapes=[pltpu.VMEM((1, LANES), F32)],
        compiler_params=_params(("arbitrary",)),
        name="router",
    )(x, a, wa, g.reshape(1, d), wr)


def _row_copy(src_hbm, row, dst_ref, r, sem):
    return pltpu.make_async_copy(src_hbm.at[pl.ds(row, 1)], dst_ref.at[pl.ds(r, 1)], sem)


def _gather_norm_body(src_ref, x_hbm, g_ref, o_ref, buf_ref, sem, *, rows):
    base = pl.program_id(0) * rows

    def start(r, c):
        _row_copy(x_hbm, src_ref[base + r], buf_ref, r, sem).start()
        return c

    def wait(r, c):
        _row_copy(x_hbm, 0, buf_ref, r, sem).wait()
        return c

    lax.fori_loop(0, rows, start, 0, unroll=8)
    lax.fori_loop(0, rows, wait, 0, unroll=8)
    o_ref[...] = _rms(buf_ref[...], g_ref[...]).astype(o_ref.dtype)


def gather_norm(x, g, src_rows, rows=1024):
    d = x.shape[1]
    n = src_rows.shape[0]
    grid_spec = pltpu.PrefetchScalarGridSpec(
        num_scalar_prefetch=1,
        grid=(n // rows,),
        in_specs=[pl.BlockSpec(memory_space=pl.ANY),
                  pl.BlockSpec((1, d), lambda i, s: (0, 0))],
        out_specs=pl.BlockSpec((rows, d), lambda i, s: (i, 0)),
        scratch_shapes=[pltpu.VMEM((rows, d), F32), pltpu.SemaphoreType.DMA(())],
    )
    return pl.pallas_call(
        functools.partial(_gather_norm_body, rows=rows),
        grid_spec=grid_spec,
        out_shape=jax.ShapeDtypeStruct((n, d), BF16),
        compiler_params=_params(("arbitrary",)),
        name="gather_norm",
    )(src_rows, x, g.reshape(1, d))


def _moe_ffn_body(ve_ref, vs_ref, vc_ref, vlive_ref, x_hbm, wg_ref, wu_ref, wd_ref, y_hbm,
                  xbuf_ref, acc_ref, wgb_ref, wub_ref, wdb_ref, sem_in, sem_out, *, sub, c_max, nf):
    v = pl.program_id(0)
    f = pl.program_id(1)
    count = vc_ref[v]
    first = vs_ref[v]
    live = vlive_ref[v] > 0

    def rows_of(j):
        return pl.ds(j * sub, sub)

    def copy_in(j):
        src = x_hbm.at[pl.ds(pl.multiple_of((first + j) * sub, sub), sub)]
        return pltpu.make_async_copy(src, xbuf_ref.at[rows_of(j)], sem_in)

    def copy_out(j):
        dst = y_hbm.at[pl.ds(pl.multiple_of((first + j) * sub, sub), sub)]
        return pltpu.make_async_copy(acc_ref.at[rows_of(j)], dst, sem_out)

    @pl.when(f == 0)
    def _():
        for j in range(c_max):
            @pl.when((j < count) & live)
            def _(j=j):
                copy_in(j).start()
        for j in range(c_max):
            @pl.when(j < count)
            def _(j=j):
                acc_ref[rows_of(j), :] = jnp.zeros((sub, acc_ref.shape[1]), F32)
        for j in range(c_max):
            @pl.when((j < count) & live)
            def _(j=j):
                copy_in(j).wait()

    @pl.when(live)
    def _():
        wgb_ref[...] = wg_ref[...].astype(BF16)
        wub_ref[...] = wu_ref[...].astype(BF16)
        wdb_ref[...] = wd_ref[...].astype(BF16)

        def chain(start, n_rows):
            rows = pl.ds(pl.multiple_of(start, sub), n_rows)
            acc_ref[rows, :] += _swiglu_step(xbuf_ref[rows, :], wgb_ref, wub_ref, wdb_ref)

        quads = count // 4

        def quad(i, carry):
            chain(i * (4 * sub), 4 * sub)
            return carry

        lax.fori_loop(0, quads, quad, 0)
        rest = count - 4 * quads

        @pl.when(rest >= 2)
        def _():
            chain(quads * (4 * sub), 2 * sub)

        @pl.when(rest % 2 == 1)
        def _():
            chain((count - 1) * sub, sub)

    @pl.when(f == nf - 1)
    def _():
        for j in range(c_max):
            @pl.when(j < count)
            def _(j=j):
                copy_out(j).start()
        for j in range(c_max):
            @pl.when(j < count)
            def _(j=j):
                copy_out(j).wait()


def moe_ffn(xs, vis_expert, vis_start, vis_count, vis_live, wg, wu, wd, sub, c_max, tf):
    n, d = xs.shape
    dff = wg.shape[2]
    nf = dff // tf
    n_vis = vis_expert.shape[0]
    fi = lambda v, f, live: jnp.where(live[v] > 0, f, nf - 1)
    grid_spec = pltpu.PrefetchScalarGridSpec(
        num_scalar_prefetch=4,
        grid=(n_vis, nf),
        in_specs=[pl.BlockSpec(memory_space=pl.ANY),
                  pl.BlockSpec((None, d, tf), lambda v, f, ve, vs, vc, lv: (ve[v], 0, fi(v, f, lv))),
                  pl.BlockSpec((None, d, tf), lambda v, f, ve, vs, vc, lv: (ve[v], 0, fi(v, f, lv))),
                  pl.BlockSpec((None, tf, d), lambda v, f, ve, vs, vc, lv: (ve[v], fi(v, f, lv), 0))],
        out_specs=pl.BlockSpec(memory_space=pl.ANY),
        scratch_shapes=[pltpu.VMEM((c_max * sub, d), BF16), pltpu.VMEM((c_max * sub, d), F32),
                        pltpu.VMEM((d, tf), BF16), pltpu.VMEM((d, tf), BF16),
                        pltpu.VMEM((tf, d), BF16),
                        pltpu.SemaphoreType.DMA(()), pltpu.SemaphoreType.DMA(())],
    )
    return pl.pallas_call(
        functools.partial(_moe_ffn_body, sub=sub, c_max=c_max, nf=nf),
        grid_spec=grid_spec,
        out_shape=jax.ShapeDtypeStruct((n, d), F32),
        compiler_params=_params(("arbitrary", "arbitrary")),
        name="moe_ffn",
    )(vis_expert, vis_start, vis_count, vis_live, xs, wg, wu, wd)


def _combine_rows(pos_ref, x_ref, gate_ref, y_hbm, buf_ref, sem, rows, n_tok):
    base = pl.program_id(0) * rows

    def start(r, c):
        _row_copy(y_hbm, pos_ref[base + r], buf_ref.at[0], r, sem).start()
        _row_copy(y_hbm, pos_ref[n_tok + base + r], buf_ref.at[1], r, sem).start()
        return c

    def wait(r, c):
        _row_copy(y_hbm, 0, buf_ref.at[0], r, sem).wait()
        _row_copy(y_hbm, 0, buf_ref.at[1], r, sem).wait()
        return c

    lax.fori_loop(0, rows, start, 0, unroll=8)
    lax.fori_loop(0, rows, wait, 0, unroll=8)
    gates = gate_ref[...]
    return x_ref[...] + (gates[:, 0:1] * buf_ref[0] + gates[:, 1:2] * buf_ref[1])


def _combine_body(pos_ref, x_ref, gate_ref, y_hbm, o_ref, buf_ref, sem, *, rows, n_tok):
    o_ref[...] = _combine_rows(pos_ref, x_ref, gate_ref, y_hbm, buf_ref, sem, rows, n_tok)


def _combine_norm_body(pos_ref, x_ref, gate_ref, g_ref, y_hbm, op_ref, os_ref, buf_ref, sem, *,
                       rows, n_tok, n_p):
    out = _rms(_combine_rows(pos_ref, x_ref, gate_ref, y_hbm, buf_ref, sem, rows, n_tok),
               g_ref[...])

    @pl.when(pl.program_id(0) < n_p)
    def _():
        op_ref[...] = out

    @pl.when(pl.program_id(0) >= n_p)
    def _():
        os_ref[...] = out


def moe_combine(x, gates, y, pos, out_norm=None, rows=512):
    n_tok, d = x.shape
    in_specs = [pl.BlockSpec((rows, d), lambda i, p: (i, 0)),
                pl.BlockSpec((rows, LANES), lambda i, p: (i, 0))]
    scratch = [pltpu.VMEM((2, rows, d), F32), pltpu.SemaphoreType.DMA(())]
    any_spec = pl.BlockSpec(memory_space=pl.ANY)
    if out_norm is None:
        return pl.pallas_call(
            functools.partial(_combine_body, rows=rows, n_tok=n_tok),
            grid_spec=pltpu.PrefetchScalarGridSpec(
                num_scalar_prefetch=1, grid=(n_tok // rows,),
                in_specs=in_specs + [any_spec],
                out_specs=pl.BlockSpec((rows, d), lambda i, p: (i, 0)),
                scratch_shapes=scratch),
            out_shape=jax.ShapeDtypeStruct((n_tok, d), F32),
            compiler_params=_params(("arbitrary",)),
            name="moe_combine",
        )(pos, x, gates, y)
    n_p = T_P // rows
    return pl.pallas_call(
        functools.partial(_combine_norm_body, rows=rows, n_tok=n_tok, n_p=n_p),
        grid_spec=pltpu.PrefetchScalarGridSpec(
            num_scalar_prefetch=1, grid=(n_tok // rows,),
            in_specs=in_specs + [pl.BlockSpec((1, d), lambda i, p: (0, 0)), any_spec],
            out_specs=[pl.BlockSpec((rows, d), lambda i, p: (jnp.minimum(i, n_p - 1), 0)),
                       pl.BlockSpec((rows, d), lambda i, p: (jnp.maximum(i - n_p, 0), 0))],
            scratch_shapes=scratch),
        out_shape=[jax.ShapeDtypeStruct((T_P, d), F32), jax.ShapeDtypeStruct((n_tok - T_P, d), F32)],
        compiler_params=_params(("arbitrary",)),
        name="moe_combine_norm",
    )(pos, x, gates, out_norm.reshape(1, d), y)


def moe_layer(x, a, wa, g, w_router, wg, wu, wd, out_norm=None, sub=256, c_max=10, tf=256):
    assert c_max >= N_EXPERTS
    n_tok = x.shape[0]
    i32 = jnp.int32
    x1, idx_full, gate_full, cnt = router(x, a, wa, g, w_router)
    experts = idx_full[:, 0:2]
    ranks = idx_full[:, 2:4]
    counts = cnt[0, :N_EXPERTS].astype(i32)
    n_assign = 2 * n_tok
    n_sb = n_assign // sub + N_EXPERTS
    n_vis = (n_sb + (c_max - 1) * N_EXPERTS) // c_max + 1

    e_ids = jnp.arange(N_EXPERTS, dtype=i32)
    sb_per = (counts + sub - 1) // sub
    sb_end = jnp.cumsum(sb_per)
    sb_start = sb_end - sb_per
    sb_used = sb_end[-1]
    row_start = sb_start * sub

    start_of = jnp.sum(jnp.where(experts[:, :, None] == e_ids, row_start, 0), axis=-1)
    pos = (start_of + ranks).astype(i32).T.reshape(-1)

    n_pad = n_sb * sub - n_assign
    pad_end = jnp.cumsum(sb_per * sub - counts)
    pad_ids = jnp.arange(n_pad, dtype=i32)
    pad_e = jnp.sum((pad_ids[:, None] >= pad_end[None, :]).astype(i32), axis=1)
    keys = jnp.concatenate([experts.reshape(-1) * 2, pad_e * 2 + 1])
    toks = jnp.concatenate([jnp.arange(n_assign, dtype=i32) // 2, (pad_ids * 8) % n_tok])
    _, src_rows = lax.sort((keys, toks), num_keys=1, is_stable=True)

    vis_per = (sb_per + c_max - 1) // c_max
    vis_end = jnp.cumsum(vis_per)
    vis_used = vis_end[-1]
    v_ids = jnp.arange(n_vis, dtype=i32)
    live = v_ids < vis_used
    v_eff = jnp.minimum(v_ids, vis_used - 1)
    vis_expert = jnp.sum((v_eff[:, None] >= vis_end[None, :]).astype(i32), axis=1)
    of_expert = vis_expert[:, None] == e_ids[None, :]
    pick = lambda table: jnp.sum(jnp.where(of_expert, table[None, :], 0), axis=1)
    nth = v_eff - (pick(vis_end) - pick(vis_per))
    vis_start = pick(sb_start) + c_max * nth
    vis_count = jnp.minimum(c_max, pick(sb_per) - c_max * nth)
    tail = v_ids == vis_used
    vis_start = jnp.where(live, vis_start, jnp.where(tail, sb_used, 0)).astype(i32)
    vis_count = jnp.where(live, vis_count, jnp.where(tail, n_sb - sb_used, 0)).astype(i32)

    xs = gather_norm(x1, g, src_rows.astype(i32))
    y = moe_ffn(xs, vis_expert.astype(i32), vis_start, vis_count, live.astype(i32), wg, wu, wd,
                sub, c_max, tf)
    return moe_combine(x1, gate_full, y, pos, out_norm)


def _final_norm_body(x_ref, g_ref, o_ref):
    o_ref[...] = _rms(x_ref[...], g_ref[...])


def final_norm(x, g, first_block, n_blocks, tm=512):
    d = x.shape[1]
    return pl.pallas_call(
        _final_norm_body,
        grid=(n_blocks,),
        in_specs=[pl.BlockSpec((tm, d), lambda i: (first_block + i, 0)),
                  pl.BlockSpec((1, d), lambda i: (0, 0))],
        out_specs=pl.BlockSpec((tm, d), lambda i: (i, 0)),
        out_shape=jax.ShapeDtypeStruct((n_blocks * tm, d), F32),
        compiler_params=_params(("parallel",)),
        name="final_norm",
    )(x, g.reshape(1, d))


def kernel(x_prompt, x_sample, cache_win_k, cache_win_v, state_conv, cache_mem_k, cache_mem_v,
           mem_prompt, norm_mix, w_in, attn_sinks, att_out_norm, conv_w, conv_b, conv_ln_g,
           conv_ln_b, conv_out_norm, w_out, norm_cross, norm_mem, w_xq, w_xk, w_xv, w_xo,
           norm_ffn, w_gate, w_up, w_down, w_router, we_gate, we_up, we_down, final_norm_g):
    x = (x_prompt.reshape(T_P, D_MODEL), x_sample.transpose(1, 0, 2).reshape(T_S, D_MODEL))
    mem = mem_prompt.reshape(N_MEM, D_MODEL)
    o1 = ATT_W
    o2 = o1 + KV_W
    o3 = o2 + KV_W
    o4 = o3 + CONV_C
    wb = cache_win_k.shape[2]
    mem_k_rows = cache_mem_k.reshape(DEPTH, DEC_BATCH, N_MEM * X_HEADS, X_HEAD_DIM)
    mem_v_rows = cache_mem_v.reshape(DEPTH, DEC_BATCH, N_MEM * X_HEADS, X_HEAD_DIM)
    cache_kt = cache_win_k.transpose(0, 1, 3, 4, 2).reshape(DEPTH, DEC_BATCH, KV_W, wb)
    cache_vt = cache_win_v.transpose(0, 1, 3, 4, 2).reshape(DEPTH, DEC_BATCH, KV_W, wb)
    state_rows = state_conv.transpose(0, 2, 1, 3)
    win_out = state_out = None
    pk, pv, pc, pmk, pmv = [], [], [], [], []
    for l in range(DEPTH):
        wl = w_in[l]
        w_u = jnp.concatenate([wl[:, :o1], wl[:, o3:o4], wl[:, o4:], wl[:, o1:o2], wl[:, o2:o3]],
                              axis=1).astype(BF16)
        u = rms_matmul(x, norm_mix[l], w_u, tm=512, tn=U_W)
        ka, kg, kk, kv = ATT_W, ATT_W + CONV_C, ATT_W + 2 * CONV_C, ATT_W + 2 * CONV_C + KV_W

        att_p = swa_prompt(u, attn_sinks[l], att_out_norm[l])
        conv_p, state_p = conv_prompt(u, conv_w[l], conv_b[l], conv_ln_g[l], conv_ln_b[l],
                                      conv_out_norm[l])
        us = u[T_P:].reshape(DEC_SEQ, DEC_BATCH, U_W).transpose(1, 0, 2)
        att_s, win_kt, win_vt = swa_sample(us[:, :, :ATT_W], us[:, :, kk:kv], us[:, :, kv:],
                                           cache_kt, cache_vt, l, win_out,
                                           attn_sinks[l], att_out_norm[l], bt=16)
        win_out = (win_kt, win_vt)
        att_s = att_s.transpose(1, 0, 2).reshape(T_S, ATT_W)
        conv_s, state_out = conv_sample(state_rows, u, l, state_out, conv_w[l], conv_b[l],
                                        conv_ln_g[l], conv_ln_b[l], conv_out_norm[l], bt=16)
        conv_s = conv_s.reshape(T_S, CONV_C)
        x, qx = matmul_add(x, [(att_p, att_s), (conv_p, conv_s)],
                           w_out[l].astype(BF16), tm=256, tn=D_MODEL,
                           then=(norm_cross[l], w_xq[l].astype(BF16)))

        keep = min(WINDOW, SEQ)
        pk.append(u[T_P - keep:T_P, kk:kv].reshape(1, keep, N_KV, HEAD_DIM))
        pv.append(u[T_P - keep:T_P, kv:].reshape(1, keep, N_KV, HEAD_DIM))
        pc.append(state_p.reshape(1, STATE_ROWS, CONV_C))

        mk = rms_matmul(mem, norm_mem[l], w_xk[l].astype(BF16), tm=N_MEM, tn=X_W)
        mv = rms_matmul(mem, norm_mem[l], w_xv[l].astype(BF16), tm=N_MEM, tn=X_W)
        pmk.append(mk.reshape(1, N_MEM, X_HEADS, X_HEAD_DIM))
        pmv.append(mv.reshape(1, N_MEM, X_HEADS, X_HEAD_DIM))
        o_p = cross_prompt(qx, mk, mv)
        q_ht = qx[T_P:].reshape(DEC_SEQ, DEC_BATCH, X_HEADS, X_HEAD_DIM).transpose(1, 2, 0, 3)
        o_s = cross_sample(q_ht.reshape(DEC_BATCH, X_HEADS * DEC_SEQ, X_HEAD_DIM), mem_k_rows,
                           mem_v_rows, l)
        o_s = o_s.reshape(DEC_BATCH, X_HEADS, DEC_SEQ, X_HEAD_DIM).transpose(2, 0, 1, 3)
        o_all = jnp.concatenate([o_p, o_s.reshape(T_S, X_W)], axis=0)
        wxo = w_xo[l].astype(BF16)

        if l % 2 == 0:
            d = l // 2
            x = dense_ffn(x, o_all, wxo, norm_ffn[l], w_gate[d], w_up[d], w_down[d], tm=1088, tf=256)
        else:
            m = l // 2
            last = l == DEPTH - 1
            x = moe_layer(x, o_all, wxo, norm_ffn[l], w_router[m], we_gate[m], we_up[m], we_down[m],
                          out_norm=final_norm_g if last else None)

    if isinstance(x, tuple):
        y_prompt, y_sample = x
    else:
        y_prompt = final_norm(x, final_norm_g, 0, T_P // 512)
        y_sample = final_norm(x, final_norm_g, T_P // 512, T_S // 512)
    y_prompt = y_prompt.reshape(1, SEQ, D_MODEL)
    y_sample = y_sample.reshape(DEC_SEQ, DEC_BATCH, D_MODEL).transpose(1, 0, 2)
    win_k_s, win_v_s = (w.reshape(DEPTH, DEC_BATCH, N_KV, HEAD_DIM, wb).transpose(0, 1, 4, 2, 3)
                        for w in win_out)
    conv_s_state = state_out.transpose(0, 2, 1, 3)
    return (y_prompt, y_sample, jnp.stack(pk), jnp.stack(pv), jnp.stack(pc), jnp.stack(pmk),
            jnp.stack(pmv), win_k_s, win_v_s, conv_s_state)
```

```python
import functools

import jax
import jax.numpy as jnp
import numpy as np
from jax import lax
from jax.experimental import pallas as pl
from jax.experimental.pallas import tpu as pltpu

F32 = jnp.float32
BF16 = jnp.bfloat16

D_MODEL = 2048
SEQ = 8192
DEPTH = 2
DEC_BATCH = 128
DEC_SEQ = 4
HEAD_DIM = 64
ATT_W = 1024
N_HEADS = 16
N_KV = 4
KV_W = N_KV * HEAD_DIM
WINDOW = 128
CONV_C = 1024
CONV_W = 31
N_MEM = 256
X_HEADS = 4
X_HEAD_DIM = 128
X_W = X_HEADS * X_HEAD_DIM
D_FF = 5632
N_EXPERTS = 8
D_FF_E = 7168
EPS = 1e-6
NEG = -1e30

T_P = SEQ
T_S = DEC_BATCH * DEC_SEQ
T_ALL = T_P + T_S

LANES = 128
SUBLANES = 8
HALF = HEAD_DIM
STATE_ROWS = CONV_W - 1
STATE_PAD = 32
KEYS_PAD = 256
VMEM_LIMIT = 56 * 1024 * 1024

SLOPES = [float(2.0 ** (-8.0 * (h + 1) / N_HEADS)) for h in range(N_HEADS)]

U_W = ATT_W + 2 * CONV_C + 2 * KV_W


def _params(sem):
    return pltpu.CompilerParams(dimension_semantics=sem, vmem_limit_bytes=VMEM_LIMIT)


def _rms(x, g):
    r = lax.rsqrt(jnp.mean(x * x, axis=-1, keepdims=True) + EPS)
    return x * r * g


def _sigmoid(x):
    return 1.0 / (1.0 + jnp.exp(-x))


def _parts(x):
    return x if isinstance(x, tuple) else (x,)


def _n_rows(parts):
    return sum(p.shape[0] for p in parts)


def _row_specs(parts, tm, width, col):
    if len(parts) == 1:
        return [pl.BlockSpec((tm, width), lambda i, j: (i, col(j)))]
    n_p = parts[0].shape[0] // tm
    return [pl.BlockSpec((tm, width), lambda i, j: (jnp.minimum(i, n_p - 1), col(j))),
            pl.BlockSpec((tm, width), lambda i, j: (jnp.maximum(i - n_p, 0), col(j)))]


def _row_value(refs, n_p):
    if len(refs) == 1:
        return refs[0][...]
    return jnp.where(pl.program_id(0) < n_p, refs[0][...], refs[1][...])


def _rms_matmul_body(*refs, n_x, n_p):
    x_refs = refs[:n_x]
    g_ref, w_ref, o_ref, xn_ref = refs[n_x:]

    @pl.when(pl.program_id(1) == 0)
    def _():
        xn_ref[...] = _rms(_row_value(x_refs, n_p), g_ref[...]).astype(BF16)

    o_ref[...] = jnp.dot(xn_ref[...], w_ref[...], preferred_element_type=F32).astype(o_ref.dtype)


def rms_matmul(x, g, w, tm, tn, out_dtype=F32):
    parts = _parts(x)
    m, k = _n_rows(parts), parts[0].shape[1]
    n = w.shape[1]
    return pl.pallas_call(
        functools.partial(_rms_matmul_body, n_x=len(parts), n_p=parts[0].shape[0] // tm),
        grid=(m // tm, n // tn),
        in_specs=_row_specs(parts, tm, k, lambda j: 0) + [
            pl.BlockSpec((1, k), lambda i, j: (0, 0)),
            pl.BlockSpec((k, tn), lambda i, j: (0, j))],
        out_specs=pl.BlockSpec((tm, tn), lambda i, j: (i, j)),
        out_shape=jax.ShapeDtypeStruct((m, n), out_dtype),
        scratch_shapes=[pltpu.VMEM((tm, k), BF16)],
        compiler_params=_params(("parallel", "arbitrary")),
        name="rms_matmul",
    )(*parts, g.reshape(1, k), w)


def _matmul_add_body(*refs, counts, n_p, n_next):
    n_out = 1 + (n_next > 0)
    n_terms = len(counts) - 1
    n_in = len(refs) - n_out
    w_refs = refs[n_in - n_next - n_terms:n_in - n_next]
    at = counts[0]
    acc = _row_value(refs[:at], n_p)
    for t in range(n_terms):
        a = _row_value(refs[at:at + counts[1 + t]], n_p)
        at += counts[1 + t]
        acc = acc + jnp.dot(a, w_refs[t][...], preferred_element_type=F32)
    refs[n_in][...] = acc
    if n_next:
        g_ref, w2_ref = refs[n_in - 2:n_in]
        refs[n_in + 1][...] = jnp.dot(_rms(acc, g_ref[...]).astype(BF16), w2_ref[...],
                                      preferred_element_type=F32)


def matmul_add(x, a_list, w, tm, tn, then=None):
    x_parts = _parts(x)
    a_parts = [_parts(a) for a in a_list]
    m, n = _n_rows(x_parts), x_parts[0].shape[1]
    in_specs = _row_specs(x_parts, tm, tn, lambda j: j)
    operands = list(x_parts)
    for parts in a_parts:
        in_specs += _row_specs(parts, tm, parts[0].shape[1], lambda j: 0)
        operands += list(parts)
    row = 0
    for parts in a_parts:
        ka = parts[0].shape[1]
        in_specs.append(pl.BlockSpec((ka, tn), lambda i, j, blk=row // ka: (blk, j)))
        operands.append(w)
        row += ka
    out_specs = [pl.BlockSpec((tm, tn), lambda i, j: (i, j))]
    out_shape = [jax.ShapeDtypeStruct((m, n), F32)]
    if then is not None:
        assert tn == n
        gain, w2 = then
        n2 = w2.shape[1]
        in_specs += [pl.BlockSpec((1, n), lambda i, j: (0, 0)),
                     pl.BlockSpec((n, n2), lambda i, j: (0, 0))]
        operands += [gain.reshape(1, n), w2]
        out_specs.append(pl.BlockSpec((tm, n2), lambda i, j: (i, 0)))
        out_shape.append(jax.ShapeDtypeStruct((m, n2), F32))
    counts = (len(x_parts),) + tuple(len(p) for p in a_parts)
    out = pl.pallas_call(
        functools.partial(_matmul_add_body, counts=counts, n_p=T_P // tm,
                          n_next=0 if then is None else 2),
        grid=(m // tm, n // tn),
        in_specs=in_specs,
        out_specs=out_specs,
        out_shape=out_shape,
        input_output_aliases={0: 0} if len(x_parts) == 1 else {},
        compiler_params=_params(("parallel", "parallel")),
        name="matmul_add",
    )(*operands)
    return out[0] if then is None else out


def _half_mask(shape, half):
    lane = lax.broadcasted_iota(jnp.int32, shape, len(shape) - 1)
    return (lane >= HALF) if half else (lane < HALF)


def _sink_softmax_pv(s, sink, v_tile, batched):
    m = jnp.maximum(jnp.max(s, axis=-1, keepdims=True), sink)
    p = jnp.exp(s - m)
    den = jnp.sum(p, axis=-1, keepdims=True) + jnp.exp(sink - m)
    if batched:
        o = jnp.einsum("bqk,bkd->bqd", p.astype(BF16), v_tile, preferred_element_type=F32)
    else:
        o = jnp.dot(p.astype(BF16), v_tile, preferred_element_type=F32)
    return o / den


def _swa_prompt_body(sink_ref, q_ref, kp_ref, kc_ref, vp_ref, vc_ref, ga_ref, o_ref):
    i = pl.program_id(0)
    blk = WINDOW
    q = q_ref[...] * (HEAD_DIM ** -0.5)
    k = jnp.concatenate([kp_ref[...], kc_ref[...]], axis=0).astype(BF16)
    v = jnp.concatenate([vp_ref[...], vc_ref[...]], axis=0).astype(BF16)
    row = lax.broadcasted_iota(jnp.int32, (blk, 2 * blk), 0)
    col = lax.broadcasted_iota(jnp.int32, (blk, 2 * blk), 1)
    dist = blk + row - col
    valid = (dist >= 0) & (dist <= WINDOW) & ((col >= blk) | (i > 0))
    distf = dist.astype(F32)
    tiles = [None] * (N_HEADS // 2)
    for n in range(N_KV):
        half = n % 2
        ts = slice((n // 2) * LANES, (n // 2 + 1) * LANES)
        k_m = jnp.where(_half_mask((2 * blk, LANES), half), k[:, ts], jnp.zeros((), BF16))
        v_t = v[:, ts]
        q_al = jnp.concatenate([q[:, (2 * n) * LANES:(2 * n + 1) * LANES],
                                q[:, (2 * n + 1) * LANES:(2 * n + 2) * LANES]], axis=0)
        q_mis = pltpu.roll(q_al, HALF, 1)
        lhs = jnp.concatenate([q_al, q_mis], axis=0).astype(BF16)
        s = lax.dot_general(lhs, k_m, (((1,), (1,)), ((), ())), preferred_element_type=F32)
        outs = []
        for b4 in range(4):
            a, j = divmod(b4, 2)
            h = 4 * n + 2 * j + (half if a == 0 else 1 - half)
            sb = s[b4 * blk:(b4 + 1) * blk]
            sb = jnp.where(valid, sb - SLOPES[h] * distf, NEG)
            outs.append(_sink_softmax_pv(sb, sink_ref[h], v_t, False))
        keep = _half_mask((blk, LANES), half)
        for j in range(2):
            tiles[2 * n + j] = jnp.where(keep, outs[j], pltpu.roll(outs[2 + j], HALF, 1))
    att = jnp.concatenate(tiles, axis=1)
    o_ref[...] = _rms(att, ga_ref[...]).astype(o_ref.dtype)


def swa_prompt(u, sinks, g_att):
    blk = WINDOW
    nb = T_P // blk
    kcol = (ATT_W + 2 * CONV_C) // KV_W
    prev = lambda i: jnp.maximum(i - 1, 0)
    return pl.pallas_call(
        _swa_prompt_body,
        grid=(nb,),
        in_specs=[pl.BlockSpec(memory_space=pltpu.SMEM),
                  pl.BlockSpec((blk, ATT_W), lambda i: (i, 0)),
                  pl.BlockSpec((blk, KV_W), lambda i: (prev(i), kcol)),
                  pl.BlockSpec((blk, KV_W), lambda i: (i, kcol)),
                  pl.BlockSpec((blk, KV_W), lambda i: (prev(i), kcol + 1)),
                  pl.BlockSpec((blk, KV_W), lambda i: (i, kcol + 1)),
                  pl.BlockSpec((1, ATT_W), lambda i: (0, 0))],
        out_specs=pl.BlockSpec((blk, ATT_W), lambda i: (i, 0)),
        out_shape=jax.ShapeDtypeStruct((T_P, ATT_W), BF16),
        compiler_params=_params(("parallel",)),
        name="swa_prompt",
    )(sinks, u, u, u, u, u, g_att.reshape(1, ATT_W))


def _layer_view(out_ref, layer):
    if len(out_ref.shape) == 3:
        return out_ref
    for other in range(out_ref.shape[0]):
        if other != layer:
            out_ref[other] = jnp.zeros(out_ref.shape[1:], out_ref.dtype)
    return out_ref.at[layer]


def _swa_sample_body(sink_ref, q_ref, kn_ref, vn_ref, kc_ref, vc_ref, bias_ref, ga_ref, *rest,
                     layer):
    o_ref, ko_ref, vo_ref, kall_ref, vall_ref = rest[-5:]
    bt = q_ref.shape[0]
    wb = kc_ref.shape[2]
    for new_ref, cache_ref, out_ref, all_ref in ((kn_ref, kc_ref, ko_ref, kall_ref),
                                                 (vn_ref, vc_ref, vo_ref, vall_ref)):
        out_ref = _layer_view(out_ref, layer)
        all_ref[:, wb:wb + DEC_SEQ, :] = new_ref[...]
        all_ref[:, wb + DEC_SEQ:, :] = jnp.zeros((bt, KEYS_PAD - wb - DEC_SEQ, KV_W), F32)
        for b in range(bt):
            all_ref[b, 0:wb, :] = cache_ref[b].T
        for b in range(bt):
            out_ref[b] = all_ref[b, DEC_SEQ:DEC_SEQ + wb, :].T

    q = q_ref[...] * (HEAD_DIM ** -0.5)
    k = kall_ref[...].astype(BF16)
    v = vall_ref[...].astype(BF16)
    tiles = [None] * (N_HEADS // 2)
    for n in range(N_KV):
        half = n % 2
        ts = slice((n // 2) * LANES, (n // 2 + 1) * LANES)
        k_m = jnp.where(_half_mask((bt, KEYS_PAD, LANES), half), k[:, :, ts], jnp.zeros((), BF16))
        v_t = v[:, :, ts]
        q_al = jnp.concatenate([q[:, :, (2 * n) * LANES:(2 * n + 1) * LANES],
                                q[:, :, (2 * n + 1) * LANES:(2 * n + 2) * LANES]], axis=1)
        q_mis = pltpu.roll(q_al, HALF, 2)
        lhs = jnp.concatenate([q_al, q_mis], axis=1).astype(BF16)
        s = jnp.einsum("bqd,bkd->bqk", lhs, k_m, preferred_element_type=F32)
        bias = bias_ref[n]
        s = jnp.where(bias > 0.5 * NEG, s + bias, NEG)
        sink = sink_ref[n]
        o = _sink_softmax_pv(s, sink, v_t, True)
        keep = _half_mask((bt, DEC_SEQ, LANES), half)
        for j in range(2):
            o_al = o[:, j * DEC_SEQ:(j + 1) * DEC_SEQ]
            o_mis = pltpu.roll(o[:, (2 + j) * DEC_SEQ:(3 + j) * DEC_SEQ], HALF, 2)
            tiles[2 * n + j] = jnp.where(keep, o_al, o_mis)
    att = jnp.concatenate(tiles, axis=2)
    o_ref[...] = _rms(att, ga_ref[...]).astype(o_ref.dtype)


def _sample_bias_and_sinks(sinks):
    wb = WINDOW
    bias = np.full((N_KV, 4 * DEC_SEQ, KEYS_PAD), NEG, np.float32)
    head = np.zeros((N_KV, 4 * DEC_SEQ), np.int32)
    for n in range(N_KV):
        half = n % 2
        for a in range(2):
            for j in range(2):
                h = 4 * n + 2 * j + (half if a == 0 else 1 - half)
                for t in range(DEC_SEQ):
                    r = (2 * a + j) * DEC_SEQ + t
                    head[n, r] = h
                    for kk in range(wb + DEC_SEQ):
                        d = t + wb - kk
                        if 0 <= d <= WINDOW:
                            bias[n, r, kk] = -SLOPES[h] * d
    sink_rows = sinks[jnp.asarray(head)][..., None]
    return jnp.asarray(bias), sink_rows


def swa_sample(q_s, k_new, v_new, cache_kt, cache_vt, layer, prev_out, sinks, g_att, bt):
    wb = cache_kt.shape[3]
    bias, sink_rows = _sample_bias_and_sinks(sinks)
    nq = 4 * DEC_SEQ
    b3 = lambda i: (i, 0, 0)
    z3 = lambda i: (0, 0, 0)
    cache_spec = pl.BlockSpec((None, bt, KV_W, wb), lambda i: (layer, i, 0, 0))
    in_specs = [pl.BlockSpec((N_KV, nq, 1), z3),
                pl.BlockSpec((bt, DEC_SEQ, ATT_W), b3),
                pl.BlockSpec((bt, DEC_SEQ, KV_W), b3),
                pl.BlockSpec((bt, DEC_SEQ, KV_W), b3),
                cache_spec, cache_spec,
                pl.BlockSpec((N_KV, nq, KEYS_PAD), z3),
                pl.BlockSpec((1, 1, ATT_W), z3)]
    operands = [sink_rows, q_s, k_new, v_new, cache_kt, cache_vt, bias, g_att.reshape(1, 1, ATT_W)]
    aliases = {}
    out_cache_spec = cache_spec
    if prev_out is None:
        out_cache_spec = pl.BlockSpec((DEPTH, bt, KV_W, wb), lambda i: (0, i, 0, 0))
    else:
        in_specs += [pl.BlockSpec(memory_space=pl.ANY)] * 2
        operands += list(prev_out)
        aliases = {len(operands) - 2: 1, len(operands) - 1: 2}
    return pl.pallas_call(
        functools.partial(_swa_sample_body, layer=layer),
        grid=(DEC_BATCH // bt,),
        in_specs=in_specs,
        out_specs=[pl.BlockSpec((bt, DEC_SEQ, ATT_W), b3), out_cache_spec, out_cache_spec],
        out_shape=[jax.ShapeDtypeStruct((DEC_BATCH, DEC_SEQ, ATT_W), BF16),
                   jax.ShapeDtypeStruct(cache_kt.shape, F32),
                   jax.ShapeDtypeStruct(cache_vt.shape, F32)],
        input_output_aliases=aliases,
        scratch_shapes=[pltpu.VMEM((bt, KEYS_PAD, KV_W), F32),
                        pltpu.VMEM((bt, KEYS_PAD, KV_W), F32)],
        compiler_params=_params(("parallel",)),
        name="swa_sample",
    )(*operands)


def _conv_post(y, cb, lg, lb, gc):
    y = y + cb
    mu = jnp.mean(y, axis=-1, keepdims=True)
    yc = y - mu
    z = yc * lax.rsqrt(jnp.mean(yc * yc, axis=-1, keepdims=True) + EPS) * lg + lb
    c = z * _sigmoid(z)
    return _rms(c, gc)


def _conv_prompt_body(ap_ref, gp_ref, a_ref, g_ref, cw_ref, cb_ref, lg_ref, lb_ref, gc_ref,
                      o_ref, st_ref, ext_ref, y_ref, *, tt, tc):
    i = pl.program_id(0)
    glu_prev = ap_ref[...] * _sigmoid(gp_ref[...])
    ext_ref[0:STATE_PAD, :] = jnp.where(i > 0, glu_prev, 0.0)
    glu = a_ref[...] * _sigmoid(g_ref[...])
    ext_ref[STATE_PAD:STATE_PAD + tt, :] = glu
    st_ref[...] = glu[tt - STATE_PAD:]
    first = STATE_PAD - STATE_ROWS
    for cblk in range(CONV_C // LANES):
        cs = slice(cblk * LANES, (cblk + 1) * LANES)
        for t0 in range(0, tt, tc):
            total = None
            for lo in range(SUBLANES):
                n_hi = (CONV_W - lo + SUBLANES - 1) // SUBLANES
                start = t0 + first + lo
                win = ext_ref[start:start + tc + SUBLANES * (n_hi - 1), cs]
                acc = None
                for hi in range(n_hi):
                    j = SUBLANES * hi + lo
                    term = win[SUBLANES * hi:SUBLANES * hi + tc] * cw_ref[j:j + 1, cs]
                    acc = term if acc is None else acc + term
                total = acc if total is None else total + acc
            y_ref[t0:t0 + tc, cs] = total
    o_ref[...] = _conv_post(y_ref[...], cb_ref[...], lg_ref[...], lb_ref[...],
                            gc_ref[...]).astype(o_ref.dtype)


def conv_prompt(u, conv_w, conv_b, ln_g, ln_b, g_conv, tt=128, tc=64):
    nt = T_P // tt
    per = tt // STATE_PAD
    prev = lambda i: jnp.maximum(i * per - 1, 0)
    acol, gcol = ATT_W // CONV_C, ATT_W // CONV_C + 1
    vec = lambda: pl.BlockSpec((1, CONV_C), lambda i: (0, 0))
    out, state = pl.pallas_call(
        functools.partial(_conv_prompt_body, tt=tt, tc=tc),
        grid=(nt,),
        in_specs=[pl.BlockSpec((STATE_PAD, CONV_C), lambda i: (prev(i), acol)),
                  pl.BlockSpec((STATE_PAD, CONV_C), lambda i: (prev(i), gcol)),
                  pl.BlockSpec((tt, CONV_C), lambda i: (i, acol)),
                  pl.BlockSpec((tt, CONV_C), lambda i: (i, gcol)),
                  pl.BlockSpec((CONV_W, CONV_C), lambda i: (0, 0)),
                  vec(), vec(), vec(), vec()],
        out_specs=[pl.BlockSpec((tt, CONV_C), lambda i: (i, 0)),
                   pl.BlockSpec((STATE_PAD, CONV_C), lambda i: (0, 0))],
        out_shape=[jax.ShapeDtypeStruct((T_P, CONV_C), BF16),
                   jax.ShapeDtypeStruct((STATE_PAD, CONV_C), F32)],
        scratch_shapes=[pltpu.VMEM((STATE_PAD + tt, CONV_C), F32),
                        pltpu.VMEM((tt, CONV_C), F32)],
        compiler_params=_params(("arbitrary",)),
        name="conv_prompt",
    )(u, u, u, u, conv_w, conv_b.reshape(1, -1), ln_g.reshape(1, -1), ln_b.reshape(1, -1),
      g_conv.reshape(1, -1))
    return out, state[STATE_PAD - STATE_ROWS:]


def _conv_sample_body(st_ref, *rest, layer):
    ag_refs = rest[:2 * DEC_SEQ]
    cw_ref, cb_ref, lg_ref, lb_ref, gc_ref = rest[2 * DEC_SEQ:2 * DEC_SEQ + 5]
    o_ref, so_ref = rest[-2:]
    so_ref = _layer_view(so_ref, layer)
    glu =[ag_refs[t][...] * _sigmoid(ag_refs[DEC_SEQ + t][...]) for t in range(DEC_SEQ)]

    def slab(r):
        return st_ref[r] if r < STATE_ROWS else glu[r - STATE_ROWS]

    for r in range(STATE_ROWS):
        so_ref[r] = slab(r + DEC_SEQ)
    for t in range(DEC_SEQ):
        y = None
        for j in range(CONV_W):
            term = slab(t + j) * cw_ref[j:j + 1, :]
            y = term if y is None else y + term
        o_ref[t] = _conv_post(y, cb_ref[...], lg_ref[...], lb_ref[...],
                              gc_ref[...]).astype(o_ref.dtype)


def conv_sample(state_rows, u, layer, prev_out, conv_w, conv_b, ln_g, ln_b, g_conv, bt):
    acol, gcol = ATT_W // CONV_C, ATT_W // CONV_C + 1
    vec = lambda: pl.BlockSpec((1, CONV_C), lambda i: (0, 0))
    st_spec = pl.BlockSpec((None, STATE_ROWS, bt, CONV_C), lambda i: (layer, 0, i, 0))
    step_spec = lambda t, col: pl.BlockSpec(
        (bt, CONV_C), lambda i: ((T_P + t * DEC_BATCH) // bt + i, col))
    in_specs = ([st_spec] + [step_spec(t, acol) for t in range(DEC_SEQ)]
                + [step_spec(t, gcol) for t in range(DEC_SEQ)]
                + [pl.BlockSpec((CONV_W, CONV_C), lambda i: (0, 0)), vec(), vec(), vec(), vec()])
    operands = [state_rows] + [u] * (2 * DEC_SEQ) + [
        conv_w, conv_b.reshape(1, -1), ln_g.reshape(1, -1), ln_b.reshape(1, -1),
        g_conv.reshape(1, -1)]
    aliases = {}
    out_st_spec = st_spec
    if prev_out is None:
        out_st_spec = pl.BlockSpec((DEPTH, STATE_ROWS, bt, CONV_C), lambda i: (0, 0, i, 0))
    else:
        in_specs.append(pl.BlockSpec(memory_space=pl.ANY))
        operands.append(prev_out)
        aliases = {len(operands) - 1: 1}
    return pl.pallas_call(
        functools.partial(_conv_sample_body, layer=layer),
        grid=(DEC_BATCH // bt,),
        in_specs=in_specs,
        out_specs=[pl.BlockSpec((DEC_SEQ, bt, CONV_C), lambda i: (0, i, 0)), out_st_spec],
        out_shape=[jax.ShapeDtypeStruct((DEC_SEQ, DEC_BATCH, CONV_C), BF16),
                   jax.ShapeDtypeStruct(state_rows.shape, F32)],
        input_output_aliases=aliases,
        compiler_params=_params(("parallel",)),
        name="conv_sample",
    )(*operands)


def _cross_prompt_body(q_ref, k_ref, v_ref, o_ref):
    q = q_ref[...].astype(BF16)
    k = k_ref[...].astype(BF16)
    v = v_ref[...].astype(BF16)
    outs = []
    for h in range(X_HEADS):
        hs = slice(h * X_HEAD_DIM, (h + 1) * X_HEAD_DIM)
        s = lax.dot_general(q[:, hs], k[:, hs], (((1,), (1,)), ((), ())),
                            preferred_element_type=F32) * (X_HEAD_DIM ** -0.5)
        p = jnp.exp(s - jnp.max(s, axis=-1, keepdims=True))
        den = jnp.sum(p, axis=-1, keepdims=True)
        outs.append(jnp.dot(p.astype(BF16), v[:, hs], preferred_element_type=F32) / den)
    o_ref[...] = jnp.concatenate(outs, axis=1).astype(o_ref.dtype)


def cross_prompt(q, mem_k, mem_v, tq=512):
    return pl.pallas_call(
        _cross_prompt_body,
        grid=(T_P // tq,),
        in_specs=[pl.BlockSpec((tq, X_W), lambda i: (i, 0)),
                  pl.BlockSpec((N_MEM, X_W), lambda i: (0, 0)),
                  pl.BlockSpec((N_MEM, X_W), lambda i: (0, 0))],
        out_specs=pl.BlockSpec((tq, X_W), lambda i: (i, 0)),
        out_shape=jax.ShapeDtypeStruct((T_P, X_W), BF16),
        compiler_params=_params(("parallel",)),
        name="cross_prompt",
    )(q, mem_k, mem_v)


def _cross_sample_body(q_ref, k_ref, v_ref, o_ref):
    q = q_ref[...].astype(BF16)
    k = k_ref[...].astype(BF16)
    v = v_ref[...].astype(BF16)
    s = jnp.einsum("bqd,bkd->bqk", q, k, preferred_element_type=F32) * (X_HEAD_DIM ** -0.5)
    row = lax.broadcasted_iota(jnp.int32, s.shape[1:], 0)
    col = lax.broadcasted_iota(jnp.int32, s.shape[1:], 1)
    same_head = (col % X_HEADS) == (row // DEC_SEQ)
    s = jnp.where(same_head, s, NEG)
    p = jnp.exp(s - jnp.max(s, axis=-1, keepdims=True))
    den = jnp.sum(p, axis=-1, keepdims=True)
    o = jnp.einsum("bqk,bkd->bqd", p.astype(BF16), v, preferred_element_type=F32) / den
    o_ref[...] = o.astype(o_ref.dtype)


def cross_sample(q_ht, mem_k, mem_v, layer, bt=8):
    rows = X_HEADS * DEC_SEQ
    b3 = lambda i: (i, 0, 0)
    b4 = lambda i: (layer, i, 0, 0)
    return pl.pallas_call(
        _cross_sample_body,
        grid=(DEC_BATCH // bt,),
        in_specs=[pl.BlockSpec((bt, rows, X_HEAD_DIM), b3),
                  pl.BlockSpec((None, bt, N_MEM * X_HEADS, X_HEAD_DIM), b4),
                  pl.BlockSpec((None, bt, N_MEM * X_HEADS, X_HEAD_DIM), b4)],
        out_specs=pl.BlockSpec((bt, rows, X_HEAD_DIM), b3),
        out_shape=jax.ShapeDtypeStruct((DEC_BATCH, rows, X_HEAD_DIM), BF16),
        compiler_params=_params(("parallel",)),
        name="cross_sample",
    )(q_ht, mem_k, mem_v)


def _swiglu_step(x, wg_ref, wu_ref, wd_ref):
    hg = jnp.dot(x, wg_ref[...].astype(BF16), preferred_element_type=F32)
    hu = jnp.dot(x, wu_ref[...].astype(BF16), preferred_element_type=F32)
    h = (hg * _sigmoid(hg) * hu).astype(BF16)
    return jnp.dot(h, wd_ref[...].astype(BF16), preferred_element_type=F32)


def _dense_ffn_body(x_ref, a_ref, wa_ref, g_ref, wg_ref, wu_ref, wd_ref, o_ref, xn_ref):
    @pl.when(pl.program_id(1) == 0)
    def _():
        x = x_ref[...] + jnp.dot(a_ref[...], wa_ref[...], preferred_element_type=F32)
        xn_ref[...] = _rms(x, g_ref[...]).astype(BF16)
        o_ref[...] = x

    o_ref[...] += _swiglu_step(xn_ref[...], wg_ref, wu_ref, wd_ref)


def dense_ffn(x, a, wa, g, wg, wu, wd, tm, tf):
    m, d = x.shape
    ka = a.shape[1]
    dff = wg.shape[1]
    return pl.pallas_call(
        _dense_ffn_body,
        grid=(m // tm, dff // tf),
        in_specs=[pl.BlockSpec((tm, d), lambda i, f: (i, 0), pipeline_mode=pl.Buffered(1)),
                  pl.BlockSpec((tm, ka), lambda i, f: (i, 0)),
                  pl.BlockSpec((ka, d), lambda i, f: (0, 0)),
                  pl.BlockSpec((1, d), lambda i, f: (0, 0)),
                  pl.BlockSpec((d, tf), lambda i, f: (0, f)),
                  pl.BlockSpec((d, tf), lambda i, f: (0, f)),
                  pl.BlockSpec((tf, d), lambda i, f: (f, 0))],
        out_specs=pl.BlockSpec((tm, d), lambda i, f: (i, 0)),
        out_shape=jax.ShapeDtypeStruct((m, d), F32),
        scratch_shapes=[pltpu.VMEM((tm, d), BF16)],
        compiler_params=_params(("parallel", "arbitrary")),
        name="dense_ffn",
    )(x, a, wa, g.reshape(1, d), wg, wu, wd)


def _router_body(x_ref, a_ref, wa_ref, g_ref, wr_ref, x1_ref, idx_ref, gate_ref, cnt_ref,
                 carry_ref):
    @pl.when(pl.program_id(0) == 0)
    def _():
        carry_ref[...] = jnp.zeros(carry_ref.shape, F32)

    x1 = x_ref[...] + jnp.dot(a_ref[...], wa_ref[...], preferred_element_type=F32)
    x1_ref[...] = x1
    xn = _rms(x1, g_ref[...])
    logits = jnp.dot(xn, wr_ref[...], preferred_element_type=F32,
                     precision=lax.Precision.HIGHEST)
    tm = logits.shape[0]
    lane = lax.broadcasted_iota(jnp.int32, logits.shape, 1)
    lg = jnp.where(lane < N_EXPERTS, logits, -jnp.inf)
    m1 = jnp.max(lg, axis=-1, keepdims=True)
    i1 = jnp.min(jnp.where(lg == m1, lane, LANES), axis=-1, keepdims=True)
    lg2 = jnp.where(lane == i1, -jnp.inf, lg)
    m2 = jnp.max(lg2, axis=-1, keepdims=True)
    i2 = jnp.min(jnp.where(lg2 == m2, lane, LANES), axis=-1, keepdims=True)
    e = jnp.exp(m2 - m1)
    den = 1.0 + e
    gate_ref[...] = jnp.where(lane == 0, 1.0 / den, jnp.where(lane == 1, e / den, 0.0))

    chosen = (lane == i1) | (lane == i2)
    r = lax.broadcasted_iota(jnp.int32, (tm, tm), 0)
    c = lax.broadcasted_iota(jnp.int32, (tm, tm), 1)
    earlier = jnp.where(c < r, 1.0, 0.0).astype(BF16)
    before = jnp.dot(earlier, jnp.where(chosen, 1.0, 0.0).astype(BF16),
                     preferred_element_type=F32) + carry_ref[...]
    r1 = jnp.sum(jnp.where(lane == i1, before, 0.0), axis=-1, keepdims=True).astype(jnp.int32)
    r2 = jnp.sum(jnp.where(lane == i2, before, 0.0), axis=-1, keepdims=True).astype(jnp.int32)
    idx_ref[...] = jnp.where(lane == 0, i1, jnp.where(lane == 1, i2,
                             jnp.where(lane == 2, r1, jnp.where(lane == 3, r2, 0))))
    carry_ref[...] += jnp.sum(jnp.where(chosen, 1.0, 0.0), axis=0, keepdims=True)
    cnt_ref[...] = carry_ref[...]


def router(x, a, wa, g, w_router, tm=512):
    m, d = x.shape
    ka = a.shape[1]
    wr = jnp.zeros((d, LANES), F32).at[:, :N_EXPERTS].set(w_router)
    return pl.pallas_call(
        _router_body,
        grid=(m // tm,),
        in_specs=[pl.BlockSpec((tm, d), lambda i: (i, 0)),
                  pl.BlockSpec((tm, ka), lambda i: (i, 0)),
                  pl.BlockSpec((ka, d), lambda i: (0, 0)),
                  pl.BlockSpec((1, d), lambda i: (0, 0)),
                  pl.BlockSpec((d, LANES), lambda i: (0, 0))],
        out_specs=[pl.BlockSpec((tm, d), lambda i: (i, 0)),
                   pl.BlockSpec((tm, LANES), lambda i: (i, 0)),
                   pl.BlockSpec((tm, LANES), lambda i: (i, 0)),
                   pl.BlockSpec((1, LANES), lambda i: (0, 0))],
        out_shape=[jax.ShapeDtypeStruct((m, d), F32),
                   jax.ShapeDtypeStruct((m, LANES), jnp.int32),
                   jax.ShapeDtypeStruct((m, LANES), F32),
                   jax.ShapeDtypeStruct((1, LANES), F32)],
        scratch_shapes=[pltpu.VMEM((1, LANES), F32)],
        compiler_params=_params(("arbitrary",)),
        name="router",
    )(x, a, wa, g.reshape(1, d), wr)


def _row_copy(src_hbm, row, dst_ref, r, sem):
    return pltpu.make_async_copy(src_hbm.at[pl.ds(row, 1)], dst_ref.at[pl.ds(r, 1)], sem)


def _gather_norm_body(src_ref, x_hbm, g_ref, o_ref, buf_ref, sem, *, rows):
    base = pl.program_id(0) * rows

    def start(r, c):
        _row_copy(x_hbm, src_ref[base + r], buf_ref, r, sem).start()
        return c

    lax.fori_loop(0, rows, start, 0, unroll=8)
    pltpu.make_async_copy(x_hbm.at[pl.ds(0, rows)], buf_ref, sem).wait()
    o_ref[...] = _rms(buf_ref[...], g_ref[...]).astype(o_ref.dtype)


def gather_norm(x, g, src_rows, rows=1024):
    d = x.shape[1]
    n = src_rows.shape[0]
    grid_spec = pltpu.PrefetchScalarGridSpec(
        num_scalar_prefetch=1,
        grid=(n // rows,),
        in_specs=[pl.BlockSpec(memory_space=pl.ANY),
                  pl.BlockSpec((1, d), lambda i, s: (0, 0))],
        out_specs=pl.BlockSpec((rows, d), lambda i, s: (i, 0)),
        scratch_shapes=[pltpu.VMEM((rows, d), F32), pltpu.SemaphoreType.DMA(())],
    )
    return pl.pallas_call(
        functools.partial(_gather_norm_body, rows=rows),
        grid_spec=grid_spec,
        out_shape=jax.ShapeDtypeStruct((n, d), BF16),
        compiler_params=_params(("arbitrary",)),
        name="gather_norm",
    )(src_rows, x, g.reshape(1, d))


def _moe_ffn_body(ve_ref, vs_ref, vc_ref, vlive_ref, x_hbm, wg_ref, wu_ref, wd_ref, y_hbm,
                  xbuf_ref, acc_ref, wgb_ref, wub_ref, wdb_ref, sem_in, sem_out, *, sub, c_max, nf):
    v = pl.program_id(0)
    f = pl.program_id(1)
    count = vc_ref[v]
    first = vs_ref[v]
    live = vlive_ref[v] > 0

    def rows_of(j):
        return pl.ds(j * sub, sub)

    def copy_in(j):
        src = x_hbm.at[pl.ds(pl.multiple_of((first + j) * sub, sub), sub)]
        return pltpu.make_async_copy(src, xbuf_ref.at[rows_of(j)], sem_in)

    def copy_out(j):
        dst = y_hbm.at[pl.ds(pl.multiple_of((first + j) * sub, sub), sub)]
        return pltpu.make_async_copy(acc_ref.at[rows_of(j)], dst, sem_out)

    @pl.when(f == 0)
    def _():
        for j in range(c_max):
            @pl.when((j < count) & live)
            def _(j=j):
                copy_in(j).start()
        for j in range(c_max):
            @pl.when(j < count)
            def _(j=j):
                acc_ref[rows_of(j), :] = jnp.zeros((sub, acc_ref.shape[1]), F32)
        for j in range(c_max):
            @pl.when((j < count) & live)
            def _(j=j):
                copy_in(j).wait()

    @pl.when(live)
    def _():
        wgb_ref[...] = wg_ref[...].astype(BF16)
        wub_ref[...] = wu_ref[...].astype(BF16)
        wdb_ref[...] = wd_ref[...].astype(BF16)

        def chain(start, n_rows):
            rows = pl.ds(pl.multiple_of(start, sub), n_rows)
            acc_ref[rows, :] += _swiglu_step(xbuf_ref[rows, :], wgb_ref, wub_ref, wdb_ref)

        quads = count // 4

        def quad(i, carry):
            chain(i * (4 * sub), 4 * sub)
            return carry

        lax.fori_loop(0, quads, quad, 0)
        rest = count - 4 * quads

        @pl.when(rest >= 2)
        def _():
            chain(quads * (4 * sub), 2 * sub)

        @pl.when(rest % 2 == 1)
        def _():
            chain((count - 1) * sub, sub)

    @pl.when(f == nf - 1)
    def _():
        for j in range(c_max):
            @pl.when(j < count)
            def _(j=j):
                copy_out(j).start()
        for j in range(c_max):
            @pl.when(j < count)
            def _(j=j):
                copy_out(j).wait()


def moe_ffn(xs, vis_expert, vis_start, vis_count, vis_live, wg, wu, wd, sub, c_max, tf):
    n, d = xs.shape
    dff = wg.shape[2]
    nf = dff // tf
    n_vis = vis_expert.shape[0]
    fi = lambda v, f, live: jnp.where(live[v] > 0, f, nf - 1)
    grid_spec = pltpu.PrefetchScalarGridSpec(
        num_scalar_prefetch=4,
        grid=(n_vis, nf),
        in_specs=[pl.BlockSpec(memory_space=pl.ANY),
                  pl.BlockSpec((None, d, tf), lambda v, f, ve, vs, vc, lv: (ve[v], 0, fi(v, f, lv))),
                  pl.BlockSpec((None, d, tf), lambda v, f, ve, vs, vc, lv: (ve[v], 0, fi(v, f, lv))),
                  pl.BlockSpec((None, tf, d), lambda v, f, ve, vs, vc, lv: (ve[v], fi(v, f, lv), 0))],
        out_specs=pl.BlockSpec(memory_space=pl.ANY),
        scratch_shapes=[pltpu.VMEM((c_max * sub, d), BF16), pltpu.VMEM((c_max * sub, d), F32),
                        pltpu.VMEM((d, tf), BF16), pltpu.VMEM((d, tf), BF16),
                        pltpu.VMEM((tf, d), BF16),
                        pltpu.SemaphoreType.DMA(()), pltpu.SemaphoreType.DMA(())],
    )
    return pl.pallas_call(
        functools.partial(_moe_ffn_body, sub=sub, c_max=c_max, nf=nf),
        grid_spec=grid_spec,
        out_shape=jax.ShapeDtypeStruct((n, d), F32),
        compiler_params=_params(("arbitrary", "arbitrary")),
        name="moe_ffn",
    )(vis_expert, vis_start, vis_count, vis_live, xs, wg, wu, wd)


def _combine_rows(pos_ref, x_ref, gate_ref, y_hbm, buf_ref, sem, rows, n_tok):
    base = pl.program_id(0) * rows

    def start(r, c):
        _row_copy(y_hbm, pos_ref[base + r], buf_ref.at[0], r, sem).start()
        _row_copy(y_hbm, pos_ref[n_tok + base + r], buf_ref.at[1], r, sem).start()
        return c

    lax.fori_loop(0, rows, start, 0, unroll=8)
    for slot in range(2):
        pltpu.make_async_copy(y_hbm.at[pl.ds(0, rows)], buf_ref.at[slot], sem).wait()
    gates = gate_ref[...]
    return x_ref[...] + (gates[:, 0:1] * buf_ref[0] + gates[:, 1:2] * buf_ref[1])


def _combine_body(pos_ref, x_ref, gate_ref, y_hbm, o_ref, buf_ref, sem, *, rows, n_tok):
    o_ref[...] = _combine_rows(pos_ref, x_ref, gate_ref, y_hbm, buf_ref, sem, rows, n_tok)


def _combine_norm_body(pos_ref, x_ref, gate_ref, g_ref, y_hbm, op_ref, os_ref, buf_ref, sem, *,
                       rows, n_tok, n_p):
    out = _rms(_combine_rows(pos_ref, x_ref, gate_ref, y_hbm, buf_ref, sem, rows, n_tok),
               g_ref[...])

    @pl.when(pl.program_id(0) < n_p)
    def _():
        op_ref[...] = out

    @pl.when(pl.program_id(0) >= n_p)
    def _():
        os_ref[...] = out


def moe_combine(x, gates, y, pos, out_norm=None, rows=512):
    n_tok, d = x.shape
    in_specs = [pl.BlockSpec((rows, d), lambda i, p: (i, 0)),
                pl.BlockSpec((rows, LANES), lambda i, p: (i, 0))]
    scratch = [pltpu.VMEM((2, rows, d), F32), pltpu.SemaphoreType.DMA(())]
    any_spec = pl.BlockSpec(memory_space=pl.ANY)
    if out_norm is None:
        return pl.pallas_call(
            functools.partial(_combine_body, rows=rows, n_tok=n_tok),
            grid_spec=pltpu.PrefetchScalarGridSpec(
                num_scalar_prefetch=1, grid=(n_tok // rows,),
                in_specs=in_specs + [any_spec],
                out_specs=pl.BlockSpec((rows, d), lambda i, p: (i, 0)),
                scratch_shapes=scratch),
            out_shape=jax.ShapeDtypeStruct((n_tok, d), F32),
            compiler_params=_params(("arbitrary",)),
            name="moe_combine",
        )(pos, x, gates, y)
    n_p = T_P // rows
    return pl.pallas_call(
        functools.partial(_combine_norm_body, rows=rows, n_tok=n_tok, n_p=n_p),
        grid_spec=pltpu.PrefetchScalarGridSpec(
            num_scalar_prefetch=1, grid=(n_tok // rows,),
            in_specs=in_specs + [pl.BlockSpec((1, d), lambda i, p: (0, 0)), any_spec],
            out_specs=[pl.BlockSpec((rows, d), lambda i, p: (jnp.minimum(i, n_p - 1), 0)),
                       pl.BlockSpec((rows, d), lambda i, p: (jnp.maximum(i - n_p, 0), 0))],
            scratch_shapes=scratch),
        out_shape=[jax.ShapeDtypeStruct((T_P, d), F32), jax.ShapeDtypeStruct((n_tok - T_P, d), F32)],
        compiler_params=_params(("arbitrary",)),
        name="moe_combine_norm",
    )(pos, x, gates, out_norm.reshape(1, d), y)


def moe_layer(x, a, wa, g, w_router, wg, wu, wd, out_norm=None, sub=256, c_max=10, tf=256):
    assert c_max >= N_EXPERTS
    n_tok = x.shape[0]
    i32 = jnp.int32
    x1, idx_full, gate_full, cnt = router(x, a, wa, g, w_router)
    experts = idx_full[:, 0:2]
    ranks = idx_full[:, 2:4]
    counts = cnt[0, :N_EXPERTS].astype(i32)
    n_assign = 2 * n_tok
    n_sb = n_assign // sub + N_EXPERTS
    n_vis = (n_sb + (c_max - 1) * N_EXPERTS) // c_max + 1

    e_ids = jnp.arange(N_EXPERTS, dtype=i32)
    sb_per = (counts + sub - 1) // sub
    sb_end = jnp.cumsum(sb_per)
    sb_start = sb_end - sb_per
    sb_used = sb_end[-1]
    row_start = sb_start * sub

    start_of = jnp.sum(jnp.where(experts[:, :, None] == e_ids, row_start, 0), axis=-1)
    pos = (start_of + ranks).astype(i32).T.reshape(-1)

    n_pad = n_sb * sub - n_assign
    pad_end = jnp.cumsum(sb_per * sub - counts)
    pad_ids = jnp.arange(n_pad, dtype=i32)
    pad_e = jnp.sum((pad_ids[:, None] >= pad_end[None, :]).astype(i32), axis=1)
    keys = jnp.concatenate([experts.reshape(-1) * 2, pad_e * 2 + 1])
    toks = jnp.concatenate([jnp.arange(n_assign, dtype=i32) // 2, (pad_ids * 8) % n_tok])
    _, src_rows = lax.sort((keys, toks), num_keys=1, is_stable=True)

    vis_per = (sb_per + c_max - 1) // c_max
    vis_end = jnp.cumsum(vis_per)
    vis_used = vis_end[-1]
    v_ids = jnp.arange(n_vis, dtype=i32)
    live = v_ids < vis_used
    v_eff = jnp.minimum(v_ids, vis_used - 1)
    vis_expert = jnp.sum((v_eff[:, None] >= vis_end[None, :]).astype(i32), axis=1)
    of_expert = vis_expert[:, None] == e_ids[None, :]
    pick = lambda table: jnp.sum(jnp.where(of_expert, table[None, :], 0), axis=1)
    nth = v_eff - (pick(vis_end) - pick(vis_per))
    vis_start = pick(sb_start) + c_max * nth
    vis_count = jnp.minimum(c_max, pick(sb_per) - c_max * nth)
    tail = v_ids == vis_used
    vis_start = jnp.where(live, vis_start, jnp.where(tail, sb_used, 0)).astype(i32)
    vis_count = jnp.where(live, vis_count, jnp.where(tail, n_sb - sb_used, 0)).astype(i32)

    xs = gather_norm(x1, g, src_rows.astype(i32))
    y = moe_ffn(xs, vis_expert.astype(i32), vis_start, vis_count, live.astype(i32), wg, wu, wd,
                sub, c_max, tf)
    return moe_combine(x1, gate_full, y, pos, out_norm)


def _final_norm_body(x_ref, g_ref, o_ref):
    o_ref[...] = _rms(x_ref[...], g_ref[...])


def final_norm(x, g, first_block, n_blocks, tm=512):
    d = x.shape[1]
    return pl.pallas_call(
        _final_norm_body,
        grid=(n_blocks,),
        in_specs=[pl.BlockSpec((tm, d), lambda i: (first_block + i, 0)),
                  pl.BlockSpec((1, d), lambda i: (0, 0))],
        out_specs=pl.BlockSpec((tm, d), lambda i: (i, 0)),
        out_shape=jax.ShapeDtypeStruct((n_blocks * tm, d), F32),
        compiler_params=_params(("parallel",)),
        name="final_norm",
    )(x, g.reshape(1, d))


def kernel(x_prompt, x_sample, cache_win_k, cache_win_v, state_conv, cache_mem_k, cache_mem_v,
           mem_prompt, norm_mix, w_in, attn_sinks, att_out_norm, conv_w, conv_b, conv_ln_g,
           conv_ln_b, conv_out_norm, w_out, norm_cross, norm_mem, w_xq, w_xk, w_xv, w_xo,
           norm_ffn, w_gate, w_up, w_down, w_router, we_gate, we_up, we_down, final_norm_g):
    x = (x_prompt.reshape(T_P, D_MODEL), x_sample.transpose(1, 0, 2).reshape(T_S, D_MODEL))
    mem = mem_prompt.reshape(N_MEM, D_MODEL)
    o1 = ATT_W
    o2 = o1 + KV_W
    o3 = o2 + KV_W
    o4 = o3 + CONV_C
    wb = cache_win_k.shape[2]
    mem_k_rows = cache_mem_k.reshape(DEPTH, DEC_BATCH, N_MEM * X_HEADS, X_HEAD_DIM)
    mem_v_rows = cache_mem_v.reshape(DEPTH, DEC_BATCH, N_MEM * X_HEADS, X_HEAD_DIM)
    cache_kt = cache_win_k.transpose(0, 1, 3, 4, 2).reshape(DEPTH, DEC_BATCH, KV_W, wb)
    cache_vt = cache_win_v.transpose(0, 1, 3, 4, 2).reshape(DEPTH, DEC_BATCH, KV_W, wb)
    state_rows = state_conv.transpose(0, 2, 1, 3)
    win_out = state_out = None
    pk, pv, pc, pmk, pmv = [], [], [], [], []
    for l in range(DEPTH):
        wl = w_in[l]
        w_u = jnp.concatenate([wl[:, :o1], wl[:, o3:o4], wl[:, o4:], wl[:, o1:o2], wl[:, o2:o3]],
                              axis=1).astype(BF16)
        u = rms_matmul(x, norm_mix[l], w_u, tm=512, tn=U_W)
        ka, kg, kk, kv = ATT_W, ATT_W + CONV_C, ATT_W + 2 * CONV_C, ATT_W + 2 * CONV_C + KV_W

        att_p = swa_prompt(u, attn_sinks[l], att_out_norm[l])
        conv_p, state_p = conv_prompt(u, conv_w[l], conv_b[l], conv_ln_g[l], conv_ln_b[l],
                                      conv_out_norm[l])
        us = u[T_P:].reshape(DEC_SEQ, DEC_BATCH, U_W).transpose(1, 0, 2)
        att_s, win_kt, win_vt = swa_sample(us[:, :, :ATT_W], us[:, :, kk:kv], us[:, :, kv:],
                                           cache_kt, cache_vt, l, win_out,
                                           attn_sinks[l], att_out_norm[l], bt=16)
        win_out = (win_kt, win_vt)
        att_s = att_s.transpose(1, 0, 2).reshape(T_S, ATT_W)
        conv_s, state_out = conv_sample(state_rows, u, l, state_out, conv_w[l], conv_b[l],
                                        conv_ln_g[l], conv_ln_b[l], conv_out_norm[l], bt=16)
        conv_s = conv_s.reshape(T_S, CONV_C)
        x, qx = matmul_add(x, [(att_p, att_s), (conv_p, conv_s)],
                           w_out[l].astype(BF16), tm=256, tn=D_MODEL,
                           then=(norm_cross[l], w_xq[l].astype(BF16)))

        keep = min(WINDOW, SEQ)
        pk.append(u[T_P - keep:T_P, kk:kv].reshape(1, keep, N_KV, HEAD_DIM))
        pv.append(u[T_P - keep:T_P, kv:].reshape(1, keep, N_KV, HEAD_DIM))
        pc.append(state_p.reshape(1, STATE_ROWS, CONV_C))

        mk = rms_matmul(mem, norm_mem[l], w_xk[l].astype(BF16), tm=N_MEM, tn=X_W)
        mv = rms_matmul(mem, norm_mem[l], w_xv[l].astype(BF16), tm=N_MEM, tn=X_W)
        pmk.append(mk.reshape(1, N_MEM, X_HEADS, X_HEAD_DIM))
        pmv.append(mv.reshape(1, N_MEM, X_HEADS, X_HEAD_DIM))
        o_p = cross_prompt(qx, mk, mv)
        q_ht = qx[T_P:].reshape(DEC_SEQ, DEC_BATCH, X_HEADS, X_HEAD_DIM).transpose(1, 2, 0, 3)
        o_s = cross_sample(q_ht.reshape(DEC_BATCH, X_HEADS * DEC_SEQ, X_HEAD_DIM), mem_k_rows,
                           mem_v_rows, l)
        o_s = o_s.reshape(DEC_BATCH, X_HEADS, DEC_SEQ, X_HEAD_DIM).transpose(2, 0, 1, 3)
        o_all = jnp.concatenate([o_p, o_s.reshape(T_S, X_W)], axis=0)
        wxo = w_xo[l].astype(BF16)

        if l % 2 == 0:
            d = l // 2
            x = dense_ffn(x, o_all, wxo, norm_ffn[l], w_gate[d].astype(BF16), w_up[d].astype(BF16),
                          w_down[d].astype(BF16), tm=1088, tf=512)
        else:
            m = l // 2
            last = l == DEPTH - 1
            x = moe_layer(x, o_all, wxo, norm_ffn[l], w_router[m], we_gate[m], we_up[m], we_down[m],
                          out_norm=final_norm_g if last else None)

    if isinstance(x, tuple):
        y_prompt, y_sample = x
    else:
        y_prompt = final_norm(x, final_norm_g, 0, T_P // 512)
        y_sample = final_norm(x, final_norm_g, T_P // 512, T_S // 512)
    y_prompt = y_prompt.reshape(1, SEQ, D_MODEL)
    y_sample = y_sample.reshape(DEC_SEQ, DEC_BATCH, D_MODEL).transpose(1, 0, 2)
    win_k_s, win_v_s = (w.reshape(DEPTH, DEC_BATCH, N_KV, HEAD_DIM, wb).transpose(0, 1, 4, 2, 3)
                        for w in win_out)
    conv_s_state = state_out.transpose(0, 2, 1, 3)
    return (y_prompt, y_sample, jnp.stack(pk), jnp.stack(pv), jnp.stack(pc), jnp.stack(pmk),
            jnp.stack(pmv), win_k_s, win_v_s, conv_s_state)
```

```python
import functools

import jax
import jax.numpy as jnp
import numpy as np
from jax import lax
from jax.experimental import pallas as pl
from jax.experimental.pallas import tpu as pltpu

F32 = jnp.float32
BF16 = jnp.bfloat16

D_MODEL = 2048
SEQ = 8192
DEPTH = 2
DEC_BATCH = 128
DEC_SEQ = 4
HEAD_DIM = 64
ATT_W = 1024
N_HEADS = 16
N_KV = 4
KV_W = N_KV * HEAD_DIM
WINDOW = 128
CONV_C = 1024
CONV_W = 31
N_MEM = 256
X_HEADS = 4
X_HEAD_DIM = 128
X_W = X_HEADS * X_HEAD_DIM
D_FF = 5632
N_EXPERTS = 8
D_FF_E = 7168
EPS = 1e-6
NEG = -1e30

T_P = SEQ
T_S = DEC_BATCH * DEC_SEQ
T_ALL = T_P + T_S

LANES = 128
SUBLANES = 8
HALF = HEAD_DIM
STATE_ROWS = CONV_W - 1
STATE_PAD = 32
KEYS_PAD = 256
VMEM_LIMIT = 56 * 1024 * 1024

SLOPES = [float(2.0 ** (-8.0 * (h + 1) / N_HEADS)) for h in range(N_HEADS)]

U_W = ATT_W + 2 * CONV_C + 2 * KV_W


def _params(sem):
    return pltpu.CompilerParams(dimension_semantics=sem, vmem_limit_bytes=VMEM_LIMIT)


def _rms(x, g):
    r = lax.rsqrt(jnp.mean(x * x, axis=-1, keepdims=True) + EPS)
    return x * r * g


def _sigmoid(x):
    return 1.0 / (1.0 + jnp.exp(-x))


def _split_bf16(x):
    hi = x.astype(BF16)
    return hi, (x - hi.astype(F32)).astype(BF16)


def _parts(x):
    return x if isinstance(x, tuple) else (x,)


def _n_rows(parts):
    return sum(p.shape[0] for p in parts)


def _row_specs(parts, tm, width, col):
    if len(parts) == 1:
        return [pl.BlockSpec((tm, width), lambda i, j: (i, col(j)))]
    n_p = parts[0].shape[0] // tm
    return [pl.BlockSpec((tm, width), lambda i, j: (jnp.minimum(i, n_p - 1), col(j))),
            pl.BlockSpec((tm, width), lambda i, j: (jnp.maximum(i - n_p, 0), col(j)))]


def _row_value(refs, n_p):
    if len(refs) == 1:
        return refs[0][...]
    return jnp.where(pl.program_id(0) < n_p, refs[0][...], refs[1][...])


def _rms_matmul_body(*refs, n_x, n_p):
    x_refs = refs[:n_x]
    g_ref, w_ref, o_ref, xn_ref = refs[n_x:]

    @pl.when(pl.program_id(1) == 0)
    def _():
        xn_ref[...] = _rms(_row_value(x_refs, n_p), g_ref[...]).astype(BF16)

    o_ref[...] = jnp.dot(xn_ref[...], w_ref[...], preferred_element_type=F32).astype(o_ref.dtype)


def rms_matmul(x, g, w, tm, tn, out_dtype=F32):
    parts = _parts(x)
    m, k = _n_rows(parts), parts[0].shape[1]
    n = w.shape[1]
    return pl.pallas_call(
        functools.partial(_rms_matmul_body, n_x=len(parts), n_p=parts[0].shape[0] // tm),
        grid=(m // tm, n // tn),
        in_specs=_row_specs(parts, tm, k, lambda j: 0) + [
            pl.BlockSpec((1, k), lambda i, j: (0, 0)),
            pl.BlockSpec((k, tn), lambda i, j: (0, j))],
        out_specs=pl.BlockSpec((tm, tn), lambda i, j: (i, j)),
        out_shape=jax.ShapeDtypeStruct((m, n), out_dtype),
        scratch_shapes=[pltpu.VMEM((tm, k), BF16)],
        compiler_params=_params(("parallel", "arbitrary")),
        name="rms_matmul",
    )(*parts, g.reshape(1, k), w)


def _matmul_add_body(*refs, counts, n_p, n_next):
    n_out = 1 + (n_next > 0)
    n_terms = len(counts) - 1
    n_in = len(refs) - n_out
    w_refs = refs[n_in - n_next - n_terms:n_in - n_next]
    at = counts[0]
    acc = _row_value(refs[:at], n_p)
    for t in range(n_terms):
        a = _row_value(refs[at:at + counts[1 + t]], n_p)
        at += counts[1 + t]
        acc = acc + jnp.dot(a, w_refs[t][...], preferred_element_type=F32)
    refs[n_in][...] = acc
    if n_next:
        g_ref, w2_ref = refs[n_in - 2:n_in]
        refs[n_in + 1][...] = jnp.dot(_rms(acc, g_ref[...]).astype(BF16), w2_ref[...],
                                      preferred_element_type=F32)


def matmul_add(x, a_list, w, tm, tn, then=None):
    x_parts = _parts(x)
    a_parts = [_parts(a) for a in a_list]
    m, n = _n_rows(x_parts), x_parts[0].shape[1]
    in_specs = _row_specs(x_parts, tm, tn, lambda j: j)
    operands = list(x_parts)
    for parts in a_parts:
        in_specs += _row_specs(parts, tm, parts[0].shape[1], lambda j: 0)
        operands += list(parts)
    row = 0
    for parts in a_parts:
        ka = parts[0].shape[1]
        in_specs.append(pl.BlockSpec((ka, tn), lambda i, j, blk=row // ka: (blk, j)))
        operands.append(w)
        row += ka
    out_specs = [pl.BlockSpec((tm, tn), lambda i, j: (i, j))]
    out_shape = [jax.ShapeDtypeStruct((m, n), F32)]
    if then is not None:
        assert tn == n
        gain, w2 = then
        n2 = w2.shape[1]
        in_specs += [pl.BlockSpec((1, n), lambda i, j: (0, 0)),
                     pl.BlockSpec((n, n2), lambda i, j: (0, 0))]
        operands += [gain.reshape(1, n), w2]
        out_specs.append(pl.BlockSpec((tm, n2), lambda i, j: (i, 0)))
        out_shape.append(jax.ShapeDtypeStruct((m, n2), F32))
    counts = (len(x_parts),) + tuple(len(p) for p in a_parts)
    out = pl.pallas_call(
        functools.partial(_matmul_add_body, counts=counts, n_p=T_P // tm,
                          n_next=0 if then is None else 2),
        grid=(m // tm, n // tn),
        in_specs=in_specs,
        out_specs=out_specs,
        out_shape=out_shape,
        input_output_aliases={0: 0} if len(x_parts) == 1 else {},
        compiler_params=_params(("parallel", "parallel")),
        name="matmul_add",
    )(*operands)
    return out[0] if then is None else out


def _half_mask(shape, half):
    lane = lax.broadcasted_iota(jnp.int32, shape, len(shape) - 1)
    return (lane >= HALF) if half else (lane < HALF)


def _sink_softmax_pv(s, sink, v_tile, batched):
    m = jnp.maximum(jnp.max(s, axis=-1, keepdims=True), sink)
    p = jnp.exp(s - m)
    den = jnp.sum(p, axis=-1, keepdims=True) + jnp.exp(sink - m)
    if batched:
        o = jnp.einsum("bqk,bkd->bqd", p.astype(BF16), v_tile, preferred_element_type=F32)
    else:
        o = jnp.dot(p.astype(BF16), v_tile, preferred_element_type=F32)
    return o / den


def _swa_prompt_body(sink_ref, q_ref, kp_ref, kc_ref, vp_ref, vc_ref, ga_ref, o_ref):
    i = pl.program_id(0)
    blk = WINDOW
    q = q_ref[...] * (HEAD_DIM ** -0.5)
    k = jnp.concatenate([kp_ref[...], kc_ref[...]], axis=0).astype(BF16)
    v = jnp.concatenate([vp_ref[...], vc_ref[...]], axis=0).astype(BF16)
    row = lax.broadcasted_iota(jnp.int32, (blk, 2 * blk), 0)
    col = lax.broadcasted_iota(jnp.int32, (blk, 2 * blk), 1)
    dist = blk + row - col
    valid = (dist >= 0) & (dist <= WINDOW) & ((col >= blk) | (i > 0))
    distf = dist.astype(F32)
    tiles = [None] * (N_HEADS // 2)
    for n in range(N_KV):
        half = n % 2
        ts = slice((n // 2) * LANES, (n // 2 + 1) * LANES)
        k_m = jnp.where(_half_mask((2 * blk, LANES), half), k[:, ts], jnp.zeros((), BF16))
        v_t = v[:, ts]
        q_al = jnp.concatenate([q[:, (2 * n) * LANES:(2 * n + 1) * LANES],
                                q[:, (2 * n + 1) * LANES:(2 * n + 2) * LANES]], axis=0)
        q_mis = pltpu.roll(q_al, HALF, 1)
        lhs = jnp.concatenate([q_al, q_mis], axis=0).astype(BF16)
        s = lax.dot_general(lhs, k_m, (((1,), (1,)), ((), ())), preferred_element_type=F32)
        outs = []
        for b4 in range(4):
            a, j = divmod(b4, 2)
            h = 4 * n + 2 * j + (half if a == 0 else 1 - half)
            sb = s[b4 * blk:(b4 + 1) * blk]
            sb = jnp.where(valid, sb - SLOPES[h] * distf, NEG)
            outs.append(_sink_softmax_pv(sb, sink_ref[h], v_t, False))
        keep = _half_mask((blk, LANES), half)
        for j in range(2):
            tiles[2 * n + j] = jnp.where(keep, outs[j], pltpu.roll(outs[2 + j], HALF, 1))
    att = jnp.concatenate(tiles, axis=1)
    o_ref[...] = _rms(att, ga_ref[...]).astype(o_ref.dtype)


def swa_prompt(u, sinks, g_att):
    blk = WINDOW
    nb = T_P // blk
    kcol = (ATT_W + 2 * CONV_C) // KV_W
    prev = lambda i: jnp.maximum(i - 1, 0)
    return pl.pallas_call(
        _swa_prompt_body,
        grid=(nb,),
        in_specs=[pl.BlockSpec(memory_space=pltpu.SMEM),
                  pl.BlockSpec((blk, ATT_W), lambda i: (i, 0)),
                  pl.BlockSpec((blk, KV_W), lambda i: (prev(i), kcol)),
                  pl.BlockSpec((blk, KV_W), lambda i: (i, kcol)),
                  pl.BlockSpec((blk, KV_W), lambda i: (prev(i), kcol + 1)),
                  pl.BlockSpec((blk, KV_W), lambda i: (i, kcol + 1)),
                  pl.BlockSpec((1, ATT_W), lambda i: (0, 0))],
        out_specs=pl.BlockSpec((blk, ATT_W), lambda i: (i, 0)),
        out_shape=jax.ShapeDtypeStruct((T_P, ATT_W), BF16),
        compiler_params=_params(("parallel",)),
        name="swa_prompt",
    )(sinks, u, u, u, u, u, g_att.reshape(1, ATT_W))


def _layer_view(out_ref, layer):
    if len(out_ref.shape) == 3:
        return out_ref
    for other in range(out_ref.shape[0]):
        if other != layer:
            out_ref[other] = jnp.zeros(out_ref.shape[1:], out_ref.dtype)
    return out_ref.at[layer]


def _swa_sample_body(sink_ref, q_ref, kn_ref, vn_ref, kc_ref, vc_ref, bias_ref, ga_ref, *rest,
                     layer):
    o_ref, ko_ref, vo_ref, kall_ref, vall_ref = rest[-5:]
    bt = q_ref.shape[0]
    wb = kc_ref.shape[2]
    for new_ref, cache_ref, out_ref, all_ref in ((kn_ref, kc_ref, ko_ref, kall_ref),
                                                 (vn_ref, vc_ref, vo_ref, vall_ref)):
        out_ref = _layer_view(out_ref, layer)
        all_ref[:, wb:wb + DEC_SEQ, :] = new_ref[...]
        all_ref[:, wb + DEC_SEQ:, :] = jnp.zeros((bt, KEYS_PAD - wb - DEC_SEQ, KV_W), F32)
        for b in range(bt):
            all_ref[b, 0:wb, :] = cache_ref[b].T
        for b in range(bt):
            out_ref[b] = all_ref[b, DEC_SEQ:DEC_SEQ + wb, :].T

    q = q_ref[...] * (HEAD_DIM ** -0.5)
    k = kall_ref[...].astype(BF16)
    v = vall_ref[...].astype(BF16)
    tiles = [None] * (N_HEADS // 2)
    for n in range(N_KV):
        half = n % 2
        ts = slice((n // 2) * LANES, (n // 2 + 1) * LANES)
        k_m = jnp.where(_half_mask((bt, KEYS_PAD, LANES), half), k[:, :, ts], jnp.zeros((), BF16))
        v_t = v[:, :, ts]
        q_al = jnp.concatenate([q[:, :, (2 * n) * LANES:(2 * n + 1) * LANES],
                                q[:, :, (2 * n + 1) * LANES:(2 * n + 2) * LANES]], axis=1)
        q_mis = pltpu.roll(q_al, HALF, 2)
        lhs = jnp.concatenate([q_al, q_mis], axis=1).astype(BF16)
        s = jnp.einsum("bqd,bkd->bqk", lhs, k_m, preferred_element_type=F32)
        bias = bias_ref[n]
        s = jnp.where(bias > 0.5 * NEG, s + bias, NEG)
        sink = sink_ref[n]
        o = _sink_softmax_pv(s, sink, v_t, True)
        keep = _half_mask((bt, DEC_SEQ, LANES), half)
        for j in range(2):
            o_al = o[:, j * DEC_SEQ:(j + 1) * DEC_SEQ]
            o_mis = pltpu.roll(o[:, (2 + j) * DEC_SEQ:(3 + j) * DEC_SEQ], HALF, 2)
            tiles[2 * n + j] = jnp.where(keep, o_al, o_mis)
    att = jnp.concatenate(tiles, axis=2)
    o_ref[...] = _rms(att, ga_ref[...]).astype(o_ref.dtype)


def _sample_bias_and_sinks(sinks):
    wb = WINDOW
    bias = np.full((N_KV, 4 * DEC_SEQ, KEYS_PAD), NEG, np.float32)
    head = np.zeros((N_KV, 4 * DEC_SEQ), np.int32)
    for n in range(N_KV):
        half = n % 2
        for a in range(2):
            for j in range(2):
                h = 4 * n + 2 * j + (half if a == 0 else 1 - half)
                for t in range(DEC_SEQ):
                    r = (2 * a + j) * DEC_SEQ + t
                    head[n, r] = h
                    for kk in range(wb + DEC_SEQ):
                        d = t + wb - kk
                        if 0 <= d <= WINDOW:
                            bias[n, r, kk] = -SLOPES[h] * d
    sink_rows = sinks[jnp.asarray(head)][..., None]
    return jnp.asarray(bias), sink_rows


def swa_sample(q_s, k_new, v_new, cache_kt, cache_vt, layer, prev_out, sinks, g_att, bt):
    wb = cache_kt.shape[3]
    bias, sink_rows = _sample_bias_and_sinks(sinks)
    nq = 4 * DEC_SEQ
    b3 = lambda i: (i, 0, 0)
    z3 = lambda i: (0, 0, 0)
    cache_spec = pl.BlockSpec((None, bt, KV_W, wb), lambda i: (layer, i, 0, 0))
    in_specs = [pl.BlockSpec((N_KV, nq, 1), z3),
                pl.BlockSpec((bt, DEC_SEQ, ATT_W), b3),
                pl.BlockSpec((bt, DEC_SEQ, KV_W), b3),
                pl.BlockSpec((bt, DEC_SEQ, KV_W), b3),
                cache_spec, cache_spec,
                pl.BlockSpec((N_KV, nq, KEYS_PAD), z3),
                pl.BlockSpec((1, 1, ATT_W), z3)]
    operands = [sink_rows, q_s, k_new, v_new, cache_kt, cache_vt, bias, g_att.reshape(1, 1, ATT_W)]
    aliases = {}
    out_cache_spec = cache_spec
    if prev_out is None:
        out_cache_spec = pl.BlockSpec((DEPTH, bt, KV_W, wb), lambda i: (0, i, 0, 0))
    else:
        in_specs += [pl.BlockSpec(memory_space=pl.ANY)] * 2
        operands += list(prev_out)
        aliases = {len(operands) - 2: 1, len(operands) - 1: 2}
    return pl.pallas_call(
        functools.partial(_swa_sample_body, layer=layer),
        grid=(DEC_BATCH // bt,),
        in_specs=in_specs,
        out_specs=[pl.BlockSpec((bt, DEC_SEQ, ATT_W), b3), out_cache_spec, out_cache_spec],
        out_shape=[jax.ShapeDtypeStruct((DEC_BATCH, DEC_SEQ, ATT_W), BF16),
                   jax.ShapeDtypeStruct(cache_kt.shape, F32),
                   jax.ShapeDtypeStruct(cache_vt.shape, F32)],
        input_output_aliases=aliases,
        scratch_shapes=[pltpu.VMEM((bt, KEYS_PAD, KV_W), F32),
                        pltpu.VMEM((bt, KEYS_PAD, KV_W), F32)],
        compiler_params=_params(("parallel",)),
        name="swa_sample",
    )(*operands)


def _conv_post(y, cb, lg, lb, gc):
    y = y + cb
    mu = jnp.mean(y, axis=-1, keepdims=True)
    yc = y - mu
    z = yc * lax.rsqrt(jnp.mean(yc * yc, axis=-1, keepdims=True) + EPS) * lg + lb
    c = z * _sigmoid(z)
    return _rms(c, gc)


def _conv_prompt_body(ap_ref, gp_ref, a_ref, g_ref, cw_ref, cb_ref, lg_ref, lb_ref, gc_ref,
                      o_ref, st_ref, ext_ref, y_ref, *, tt, tc):
    i = pl.program_id(0)
    glu_prev = ap_ref[...] * _sigmoid(gp_ref[...])
    ext_ref[0:STATE_PAD, :] = jnp.where(i > 0, glu_prev, 0.0)
    glu = a_ref[...] * _sigmoid(g_ref[...])
    ext_ref[STATE_PAD:STATE_PAD + tt, :] = glu
    st_ref[...] = glu[tt - STATE_PAD:]
    first = STATE_PAD - STATE_ROWS
    for cblk in range(CONV_C // LANES):
        cs = slice(cblk * LANES, (cblk + 1) * LANES)
        for t0 in range(0, tt, tc):
            total = None
            for lo in range(SUBLANES):
                n_hi = (CONV_W - lo + SUBLANES - 1) // SUBLANES
                start = t0 + first + lo
                win = ext_ref[start:start + tc + SUBLANES * (n_hi - 1), cs]
                acc = None
                for hi in range(n_hi):
                    j = SUBLANES * hi + lo
                    term = win[SUBLANES * hi:SUBLANES * hi + tc] * cw_ref[j:j + 1, cs]
                    acc = term if acc is None else acc + term
                total = acc if total is None else total + acc
            y_ref[t0:t0 + tc, cs] = total
    o_ref[...] = _conv_post(y_ref[...], cb_ref[...], lg_ref[...], lb_ref[...],
                            gc_ref[...]).astype(o_ref.dtype)


def conv_prompt(u, conv_w, conv_b, ln_g, ln_b, g_conv, tt=128, tc=64):
    nt = T_P // tt
    per = tt // STATE_PAD
    prev = lambda i: jnp.maximum(i * per - 1, 0)
    acol, gcol = ATT_W // CONV_C, ATT_W // CONV_C + 1
    vec = lambda: pl.BlockSpec((1, CONV_C), lambda i: (0, 0))
    out, state = pl.pallas_call(
        functools.partial(_conv_prompt_body, tt=tt, tc=tc),
        grid=(nt,),
        in_specs=[pl.BlockSpec((STATE_PAD, CONV_C), lambda i: (prev(i), acol)),
                  pl.BlockSpec((STATE_PAD, CONV_C), lambda i: (prev(i), gcol)),
                  pl.BlockSpec((tt, CONV_C), lambda i: (i, acol)),
                  pl.BlockSpec((tt, CONV_C), lambda i: (i, gcol)),
                  pl.BlockSpec((CONV_W, CONV_C), lambda i: (0, 0)),
                  vec(), vec(), vec(), vec()],
        out_specs=[pl.BlockSpec((tt, CONV_C), lambda i: (i, 0)),
                   pl.BlockSpec((STATE_PAD, CONV_C), lambda i: (0, 0))],
        out_shape=[jax.ShapeDtypeStruct((T_P, CONV_C), BF16),
                   jax.ShapeDtypeStruct((STATE_PAD, CONV_C), F32)],
        scratch_shapes=[pltpu.VMEM((STATE_PAD + tt, CONV_C), F32),
                        pltpu.VMEM((tt, CONV_C), F32)],
        compiler_params=_params(("arbitrary",)),
        name="conv_prompt",
    )(u, u, u, u, conv_w, conv_b.reshape(1, -1), ln_g.reshape(1, -1), ln_b.reshape(1, -1),
      g_conv.reshape(1, -1))
    return out, state[STATE_PAD - STATE_ROWS:]


def _conv_sample_body(st_ref, *rest, layer):
    ag_refs = rest[:2 * DEC_SEQ]
    cw_ref, cb_ref, lg_ref, lb_ref, gc_ref = rest[2 * DEC_SEQ:2 * DEC_SEQ + 5]
    o_ref, so_ref = rest[-2:]
    so_ref = _layer_view(so_ref, layer)
    glu =[ag_refs[t][...] * _sigmoid(ag_refs[DEC_SEQ + t][...]) for t in range(DEC_SEQ)]

    def slab(r):
        return st_ref[r] if r < STATE_ROWS else glu[r - STATE_ROWS]

    for r in range(STATE_ROWS):
        so_ref[r] = slab(r + DEC_SEQ)
    for t in range(DEC_SEQ):
        y = None
        for j in range(CONV_W):
            term = slab(t + j) * cw_ref[j:j + 1, :]
            y = term if y is None else y + term
        o_ref[t] = _conv_post(y, cb_ref[...], lg_ref[...], lb_ref[...],
                              gc_ref[...]).astype(o_ref.dtype)


def conv_sample(state_rows, u, layer, prev_out, conv_w, conv_b, ln_g, ln_b, g_conv, bt):
    acol, gcol = ATT_W // CONV_C, ATT_W // CONV_C + 1
    vec = lambda: pl.BlockSpec((1, CONV_C), lambda i: (0, 0))
    st_spec = pl.BlockSpec((None, STATE_ROWS, bt, CONV_C), lambda i: (layer, 0, i, 0))
    step_spec = lambda t, col: pl.BlockSpec(
        (bt, CONV_C), lambda i: ((T_P + t * DEC_BATCH) // bt + i, col))
    in_specs = ([st_spec] + [step_spec(t, acol) for t in range(DEC_SEQ)]
                + [step_spec(t, gcol) for t in range(DEC_SEQ)]
                + [pl.BlockSpec((CONV_W, CONV_C), lambda i: (0, 0)), vec(), vec(), vec(), vec()])
    operands = [state_rows] + [u] * (2 * DEC_SEQ) + [
        conv_w, conv_b.reshape(1, -1), ln_g.reshape(1, -1), ln_b.reshape(1, -1),
        g_conv.reshape(1, -1)]
    aliases = {}
    out_st_spec = st_spec
    if prev_out is None:
        out_st_spec = pl.BlockSpec((DEPTH, STATE_ROWS, bt, CONV_C), lambda i: (0, 0, i, 0))
    else:
        in_specs.append(pl.BlockSpec(memory_space=pl.ANY))
        operands.append(prev_out)
        aliases = {len(operands) - 1: 1}
    return pl.pallas_call(
        functools.partial(_conv_sample_body, layer=layer),
        grid=(DEC_BATCH // bt,),
        in_specs=in_specs,
        out_specs=[pl.BlockSpec((DEC_SEQ, bt, CONV_C), lambda i: (0, i, 0)), out_st_spec],
        out_shape=[jax.ShapeDtypeStruct((DEC_SEQ, DEC_BATCH, CONV_C), BF16),
                   jax.ShapeDtypeStruct(state_rows.shape, F32)],
        input_output_aliases=aliases,
        compiler_params=_params(("parallel",)),
        name="conv_sample",
    )(*operands)


def _cross_prompt_body(q_ref, k_ref, v_ref, o_ref):
    q = q_ref[...].astype(BF16)
    k = k_ref[...].astype(BF16)
    v = v_ref[...].astype(BF16)
    outs = []
    for h in range(X_HEADS):
        hs = slice(h * X_HEAD_DIM, (h + 1) * X_HEAD_DIM)
        s = lax.dot_general(q[:, hs], k[:, hs], (((1,), (1,)), ((), ())),
                            preferred_element_type=F32) * (X_HEAD_DIM ** -0.5)
        p = jnp.exp(s - jnp.max(s, axis=-1, keepdims=True))
        den = jnp.sum(p, axis=-1, keepdims=True)
        outs.append(jnp.dot(p.astype(BF16), v[:, hs], preferred_element_type=F32) / den)
    o_ref[...] = jnp.concatenate(outs, axis=1).astype(o_ref.dtype)


def cross_prompt(q, mem_k, mem_v, tq=512):
    return pl.pallas_call(
        _cross_prompt_body,
        grid=(T_P // tq,),
        in_specs=[pl.BlockSpec((tq, X_W), lambda i: (i, 0)),
                  pl.BlockSpec((N_MEM, X_W), lambda i: (0, 0)),
                  pl.BlockSpec((N_MEM, X_W), lambda i: (0, 0))],
        out_specs=pl.BlockSpec((tq, X_W), lambda i: (i, 0)),
        out_shape=jax.ShapeDtypeStruct((T_P, X_W), BF16),
        compiler_params=_params(("parallel",)),
        name="cross_prompt",
    )(q, mem_k, mem_v)


def _cross_sample_body(q_ref, k_ref, v_ref, o_ref):
    q = q_ref[...].astype(BF16)
    k = k_ref[...].astype(BF16)
    v = v_ref[...].astype(BF16)
    s = jnp.einsum("bqd,bkd->bqk", q, k, preferred_element_type=F32) * (X_HEAD_DIM ** -0.5)
    row = lax.broadcasted_iota(jnp.int32, s.shape[1:], 0)
    col = lax.broadcasted_iota(jnp.int32, s.shape[1:], 1)
    same_head = (col % X_HEADS) == (row // DEC_SEQ)
    s = jnp.where(same_head, s, NEG)
    p = jnp.exp(s - jnp.max(s, axis=-1, keepdims=True))
    den = jnp.sum(p, axis=-1, keepdims=True)
    o = jnp.einsum("bqk,bkd->bqd", p.astype(BF16), v, preferred_element_type=F32) / den
    o_ref[...] = o.astype(o_ref.dtype)


def cross_sample(q_ht, mem_k, mem_v, layer, bt=8):
    rows = X_HEADS * DEC_SEQ
    b3 = lambda i: (i, 0, 0)
    b4 = lambda i: (layer, i, 0, 0)
    return pl.pallas_call(
        _cross_sample_body,
        grid=(DEC_BATCH // bt,),
        in_specs=[pl.BlockSpec((bt, rows, X_HEAD_DIM), b3),
                  pl.BlockSpec((None, bt, N_MEM * X_HEADS, X_HEAD_DIM), b4),
                  pl.BlockSpec((None, bt, N_MEM * X_HEADS, X_HEAD_DIM), b4)],
        out_specs=pl.BlockSpec((bt, rows, X_HEAD_DIM), b3),
        out_shape=jax.ShapeDtypeStruct((DEC_BATCH, rows, X_HEAD_DIM), BF16),
        compiler_params=_params(("parallel",)),
        name="cross_sample",
    )(q_ht, mem_k, mem_v)


def _swiglu_step(x, wg_ref, wu_ref, wd_ref):
    hg = jnp.dot(x, wg_ref[...].astype(BF16), preferred_element_type=F32)
    hu = jnp.dot(x, wu_ref[...].astype(BF16), preferred_element_type=F32)
    h = (hg * _sigmoid(hg) * hu).astype(BF16)
    return jnp.dot(h, wd_ref[...].astype(BF16), preferred_element_type=F32)


def _dense_ffn_body(x_ref, a_ref, wa_ref, g_ref, wg_ref, wu_ref, wd_ref, o_ref, xn_ref):
    @pl.when(pl.program_id(1) == 0)
    def _():
        x = x_ref[...] + jnp.dot(a_ref[...], wa_ref[...], preferred_element_type=F32)
        xn_ref[...] = _rms(x, g_ref[...]).astype(BF16)
        o_ref[...] = x

    o_ref[...] += _swiglu_step(xn_ref[...], wg_ref, wu_ref, wd_ref)


def dense_ffn(x, a, wa, g, wg, wu, wd, tm, tf):
    m, d = x.shape
    ka = a.shape[1]
    dff = wg.shape[1]
    return pl.pallas_call(
        _dense_ffn_body,
        grid=(m // tm, dff // tf),
        in_specs=[pl.BlockSpec((tm, d), lambda i, f: (i, 0), pipeline_mode=pl.Buffered(1)),
                  pl.BlockSpec((tm, ka), lambda i, f: (i, 0)),
                  pl.BlockSpec((ka, d), lambda i, f: (0, 0)),
                  pl.BlockSpec((1, d), lambda i, f: (0, 0)),
                  pl.BlockSpec((d, tf), lambda i, f: (0, f)),
                  pl.BlockSpec((d, tf), lambda i, f: (0, f)),
                  pl.BlockSpec((tf, d), lambda i, f: (f, 0))],
        out_specs=pl.BlockSpec((tm, d), lambda i, f: (i, 0)),
        out_shape=jax.ShapeDtypeStruct((m, d), F32),
        scratch_shapes=[pltpu.VMEM((tm, d), BF16)],
        compiler_params=_params(("parallel", "arbitrary")),
        name="dense_ffn",
    )(x, a, wa, g.reshape(1, d), wg, wu, wd)


def _router_body(x_ref, a_ref, wa_ref, g_ref, wr_ref, x1_ref, idx_ref, gate_ref, cnt_ref,
                 carry_ref):
    @pl.when(pl.program_id(0) == 0)
    def _():
        carry_ref[...] = jnp.zeros(carry_ref.shape, F32)

    x1 = x_ref[...] + jnp.dot(a_ref[...], wa_ref[...], preferred_element_type=F32)
    x1_ref[...] = x1
    xn = _rms(x1, g_ref[...])
    xh, xl = _split_bf16(xn)
    wh, wl = _split_bf16(wr_ref[...])
    logits = (jnp.dot(xh, wh, preferred_element_type=F32)
              + (jnp.dot(xl, wh, preferred_element_type=F32)
                 + jnp.dot(xh, wl, preferred_element_type=F32)))
    tm = logits.shape[0]
    lane = lax.broadcasted_iota(jnp.int32, logits.shape, 1)
    lg = jnp.where(lane < N_EXPERTS, logits, -jnp.inf)
    m1 = jnp.max(lg, axis=-1, keepdims=True)
    i1 = jnp.min(jnp.where(lg == m1, lane, LANES), axis=-1, keepdims=True)
    lg2 = jnp.where(lane == i1, -jnp.inf, lg)
    m2 = jnp.max(lg2, axis=-1, keepdims=True)
    i2 = jnp.min(jnp.where(lg2 == m2, lane, LANES), axis=-1, keepdims=True)
    e = jnp.exp(m2 - m1)
    den = 1.0 + e
    gate_ref[...] = jnp.where(lane == 0, 1.0 / den, jnp.where(lane == 1, e / den, 0.0))

    chosen = (lane == i1) | (lane == i2)
    r = lax.broadcasted_iota(jnp.int32, (tm, tm), 0)
    c = lax.broadcasted_iota(jnp.int32, (tm, tm), 1)
    earlier = jnp.where(c < r, 1.0, 0.0).astype(BF16)
    before = jnp.dot(earlier, jnp.where(chosen, 1.0, 0.0).astype(BF16),
                     preferred_element_type=F32) + carry_ref[...]
    r1 = jnp.sum(jnp.where(lane == i1, before, 0.0), axis=-1, keepdims=True).astype(jnp.int32)
    r2 = jnp.sum(jnp.where(lane == i2, before, 0.0), axis=-1, keepdims=True).astype(jnp.int32)
    idx_ref[...] = jnp.where(lane == 0, i1, jnp.where(lane == 1, i2,
                             jnp.where(lane == 2, r1, jnp.where(lane == 3, r2, 0))))
    carry_ref[...] += jnp.sum(jnp.where(chosen, 1.0, 0.0), axis=0, keepdims=True)
    cnt_ref[...] = carry_ref[...]


def router(x, a, wa, g, w_router, tm=512):
    m, d = x.shape
    ka = a.shape[1]
    wr = jnp.zeros((d, LANES), F32).at[:, :N_EXPERTS].set(w_router)
    return pl.pallas_call(
        _router_body,
        grid=(m // tm,),
        in_specs=[pl.BlockSpec((tm, d), lambda i: (i, 0)),
                  pl.BlockSpec((tm, ka), lambda i: (i, 0)),
                  pl.BlockSpec((ka, d), lambda i: (0, 0)),
                  pl.BlockSpec((1, d), lambda i: (0, 0)),
                  pl.BlockSpec((d, LANES), lambda i: (0, 0))],
        out_specs=[pl.BlockSpec((tm, d), lambda i: (i, 0)),
                   pl.BlockSpec((tm, LANES), lambda i: (i, 0)),
                   pl.BlockSpec((tm, LANES), lambda i: (i, 0)),
                   pl.BlockSpec((1, LANES), lambda i: (0, 0))],
        out_shape=[jax.ShapeDtypeStruct((m, d), F32),
                   jax.ShapeDtypeStruct((m, LANES), jnp.int32),
                   jax.ShapeDtypeStruct((m, LANES), F32),
                   jax.ShapeDtypeStruct((1, LANES), F32)],
        scratch_shapes=[pltpu.VMEM((1, LANES), F32)],
        compiler_params=_params(("arbitrary",)),
        name="router",
    )(x, a, wa, g.reshape(1, d), wr)


def _row_copy(src_hbm, row, dst_ref, r, sem):
    return pltpu.make_async_copy(src_hbm.at[pl.ds(row, 1)], dst_ref.at[pl.ds(r, 1)], sem)


def _gather_norm_body(src_ref, x_hbm, g_ref, o_ref, buf_ref, sem, *, rows):
    base = pl.program_id(0) * rows

    def start(k, c):
        r0 = pl.multiple_of(k * SUBLANES, SUBLANES)
        for s in range(SUBLANES):
            _row_copy(x_hbm, src_ref[base + r0 + s], buf_ref, r0 + s, sem).start()
        return c

    lax.fori_loop(0, rows // SUBLANES, start, 0)
    pltpu.make_async_copy(x_hbm.at[pl.ds(0, rows)], buf_ref, sem).wait()
    o_ref[...] = _rms(buf_ref[...], g_ref[...]).astype(o_ref.dtype)


def gather_norm(x, g, src_rows, rows=1024):
    d = x.shape[1]
    n = src_rows.shape[0]
    grid_spec = pltpu.PrefetchScalarGridSpec(
        num_scalar_prefetch=1,
        grid=(n // rows,),
        in_specs=[pl.BlockSpec(memory_space=pl.ANY),
                  pl.BlockSpec((1, d), lambda i, s: (0, 0))],
        out_specs=pl.BlockSpec((rows, d), lambda i, s: (i, 0)),
        scratch_shapes=[pltpu.VMEM((rows, d), F32), pltpu.SemaphoreType.DMA(())],
    )
    return pl.pallas_call(
        functools.partial(_gather_norm_body, rows=rows),
        grid_spec=grid_spec,
        out_shape=jax.ShapeDtypeStruct((n, d), BF16),
        compiler_params=_params(("arbitrary",)),
        name="gather_norm",
    )(src_rows, x, g.reshape(1, d))


def _moe_ffn_body(ve_ref, vs_ref, vc_ref, vlive_ref, x_hbm, wg_ref, wu_ref, wd_ref, y_hbm,
                  xbuf_ref, acc_ref, wgb_ref, wub_ref, wdb_ref, sem_in, sem_out, *, sub, c_max, nf):
    v = pl.program_id(0)
    f = pl.program_id(1)
    count = vc_ref[v]
    first = vs_ref[v]
    live = vlive_ref[v] > 0

    def rows_of(j):
        return pl.ds(pl.multiple_of(j * sub, sub), sub)

    def copy_in(j):
        src = x_hbm.at[pl.ds(pl.multiple_of((first + j) * sub, sub), sub)]
        return pltpu.make_async_copy(src, xbuf_ref.at[rows_of(j)], sem_in.at[j])

    def copy_out(j):
        dst = y_hbm.at[pl.ds(pl.multiple_of((first + j) * sub, sub), sub)]
        return pltpu.make_async_copy(acc_ref.at[rows_of(j)], dst, sem_out)

    @pl.when(f == 0)
    def _():
        for j in range(c_max):
            @pl.when((j < count) & live)
            def _(j=j):
                copy_in(j).start()
        for j in range(c_max):
            @pl.when(j < count)
            def _(j=j):
                acc_ref[rows_of(j), :] = jnp.zeros((sub, acc_ref.shape[1]), F32)

    @pl.when(live)
    def _():
        wgb_ref[...] = wg_ref[...].astype(BF16)
        wub_ref[...] = wu_ref[...].astype(BF16)
        wdb_ref[...] = wd_ref[...].astype(BF16)

        def chain(j0, n):
            @pl.when(f == 0)
            def _():
                for k in range(n):
                    copy_in(j0 + k).wait()

            rows = pl.ds(pl.multiple_of(j0 * sub, sub), n * sub)
            acc_ref[rows, :] += _swiglu_step(xbuf_ref[rows, :], wgb_ref, wub_ref, wdb_ref)

            @pl.when(f == nf - 1)
            def _():
                for k in range(n):
                    copy_out(j0 + k).start()

        quads = count // 4

        def quad(i, carry):
            chain(i * 4, 4)
            return carry

        lax.fori_loop(0, quads, quad, 0)
        rest = count - 4 * quads

        @pl.when(rest >= 2)
        def _():
            chain(quads * 4, 2)

        @pl.when(rest % 2 == 1)
        def _():
            chain(count - 1, 1)

    @pl.when(f == nf - 1)
    def _():
        for j in range(c_max):
            @pl.when((j < count) & jnp.logical_not(live))
            def _(j=j):
                copy_out(j).start()
        for j in range(c_max):
            @pl.when(j < count)
            def _(j=j):
                copy_out(j).wait()


def moe_ffn(xs, vis_expert, vis_start, vis_count, vis_live, wg, wu, wd, sub, c_max, tf):
    n, d = xs.shape
    dff = wg.shape[2]
    nf = dff // tf
    n_vis = vis_expert.shape[0]
    fi = lambda v, f, live: jnp.where(live[v] > 0, f, nf - 1)
    grid_spec = pltpu.PrefetchScalarGridSpec(
        num_scalar_prefetch=4,
        grid=(n_vis, nf),
        in_specs=[pl.BlockSpec(memory_space=pl.ANY),
                  pl.BlockSpec((None, d, tf), lambda v, f, ve, vs, vc, lv: (ve[v], 0, fi(v, f, lv))),
                  pl.BlockSpec((None, d, tf), lambda v, f, ve, vs, vc, lv: (ve[v], 0, fi(v, f, lv))),
                  pl.BlockSpec((None, tf, d), lambda v, f, ve, vs, vc, lv: (ve[v], fi(v, f, lv), 0))],
        out_specs=pl.BlockSpec(memory_space=pl.ANY),
        scratch_shapes=[pltpu.VMEM((c_max * sub, d), BF16), pltpu.VMEM((c_max * sub, d), F32),
                        pltpu.VMEM((d, tf), BF16), pltpu.VMEM((d, tf), BF16),
                        pltpu.VMEM((tf, d), BF16),
                        pltpu.SemaphoreType.DMA((c_max,)), pltpu.SemaphoreType.DMA(())],
    )
    return pl.pallas_call(
        functools.partial(_moe_ffn_body, sub=sub, c_max=c_max, nf=nf),
        grid_spec=grid_spec,
        out_shape=jax.ShapeDtypeStruct((n, d), F32),
        compiler_params=_params(("arbitrary", "arbitrary")),
        name="moe_ffn",
    )(vis_expert, vis_start, vis_count, vis_live, xs, wg, wu, wd)


def _combine_rows(pos_ref, x_ref, gate_ref, y_hbm, buf_ref, sem, rows, n_tok):
    base = pl.program_id(0) * rows

    def start(k, c):
        r0 = pl.multiple_of(k * SUBLANES, SUBLANES)
        for s in range(SUBLANES):
            _row_copy(y_hbm, pos_ref[base + r0 + s], buf_ref.at[0], r0 + s, sem).start()
            _row_copy(y_hbm, pos_ref[n_tok + base + r0 + s], buf_ref.at[1], r0 + s, sem).start()
        return c

    lax.fori_loop(0, rows // SUBLANES, start, 0)
    for slot in range(2):
        pltpu.make_async_copy(y_hbm.at[pl.ds(0, rows)], buf_ref.at[slot], sem).wait()
    gates = gate_ref[...]
    return x_ref[...] + (gates[:, 0:1] * buf_ref[0] + gates[:, 1:2] * buf_ref[1])


def _combine_body(pos_ref, x_ref, gate_ref, y_hbm, o_ref, buf_ref, sem, *, rows, n_tok):
    o_ref[...] = _combine_rows(pos_ref, x_ref, gate_ref, y_hbm, buf_ref, sem, rows, n_tok)


def _combine_norm_body(pos_ref, x_ref, gate_ref, g_ref, y_hbm, op_ref, os_ref, buf_ref, sem, *,
                       rows, n_tok, n_p):
    out = _rms(_combine_rows(pos_ref, x_ref, gate_ref, y_hbm, buf_ref, sem, rows, n_tok),
               g_ref[...])

    @pl.when(pl.program_id(0) < n_p)
    def _():
        op_ref[...] = out

    @pl.when(pl.program_id(0) >= n_p)
    def _():
        os_ref[...] = out


def moe_combine(x, gates, y, pos, out_norm=None, rows=512):
    n_tok, d = x.shape
    in_specs = [pl.BlockSpec((rows, d), lambda i, p: (i, 0)),
                pl.BlockSpec((rows, LANES), lambda i, p: (i, 0))]
    scratch = [pltpu.VMEM((2, rows, d), F32), pltpu.SemaphoreType.DMA(())]
    any_spec = pl.BlockSpec(memory_space=pl.ANY)
    if out_norm is None:
        return pl.pallas_call(
            functools.partial(_combine_body, rows=rows, n_tok=n_tok),
            grid_spec=pltpu.PrefetchScalarGridSpec(
                num_scalar_prefetch=1, grid=(n_tok // rows,),
                in_specs=in_specs + [any_spec],
                out_specs=pl.BlockSpec((rows, d), lambda i, p: (i, 0)),
                scratch_shapes=scratch),
            out_shape=jax.ShapeDtypeStruct((n_tok, d), F32),
            compiler_params=_params(("arbitrary",)),
            name="moe_combine",
        )(pos, x, gates, y)
    n_p = T_P // rows
    return pl.pallas_call(
        functools.partial(_combine_norm_body, rows=rows, n_tok=n_tok, n_p=n_p),
        grid_spec=pltpu.PrefetchScalarGridSpec(
            num_scalar_prefetch=1, grid=(n_tok // rows,),
            in_specs=in_specs + [pl.BlockSpec((1, d), lambda i, p: (0, 0)), any_spec],
            out_specs=[pl.BlockSpec((rows, d), lambda i, p: (jnp.minimum(i, n_p - 1), 0)),
                       pl.BlockSpec((rows, d), lambda i, p: (jnp.maximum(i - n_p, 0), 0))],
            scratch_shapes=scratch),
        out_shape=[jax.ShapeDtypeStruct((T_P, d), F32), jax.ShapeDtypeStruct((n_tok - T_P, d), F32)],
        compiler_params=_params(("arbitrary",)),
        name="moe_combine_norm",
    )(pos, x, gates, out_norm.reshape(1, d), y)


def moe_layer(x, a, wa, g, w_router, wg, wu, wd, out_norm=None, sub=256, c_max=10, tf=256):
    assert c_max >= N_EXPERTS
    n_tok = x.shape[0]
    i32 = jnp.int32
    x1, idx_full, gate_full, cnt = router(x, a, wa, g, w_router)
    experts = idx_full[:, 0:2]
    ranks = idx_full[:, 2:4]
    counts = cnt[0, :N_EXPERTS].astype(i32)
    n_assign = 2 * n_tok
    n_sb = n_assign // sub + N_EXPERTS
    n_vis = (n_sb + (c_max - 1) * N_EXPERTS) // c_max + 1

    e_ids = jnp.arange(N_EXPERTS, dtype=i32)
    sb_per = (counts + sub - 1) // sub
    sb_end = jnp.cumsum(sb_per)
    sb_start = sb_end - sb_per
    sb_used = sb_end[-1]
    row_start = sb_start * sub

    start_of = jnp.sum(jnp.where(experts[:, :, None] == e_ids, row_start, 0), axis=-1)
    pos = (start_of + ranks).astype(i32).T.reshape(-1)

    n_pad = n_sb * sub - n_assign
    pad_end = jnp.cumsum(sb_per * sub - counts)
    pad_ids = jnp.arange(n_pad, dtype=i32)
    pad_e = jnp.sum((pad_ids[:, None] >= pad_end[None, :]).astype(i32), axis=1)
    keys = jnp.concatenate([experts.reshape(-1) * 2, pad_e * 2 + 1])
    toks = jnp.concatenate([jnp.arange(n_assign, dtype=i32) // 2, (pad_ids * 8) % n_tok])
    _, src_rows = lax.sort((keys, toks), num_keys=1, is_stable=True)

    vis_per = (sb_per + c_max - 1) // c_max
    vis_end = jnp.cumsum(vis_per)
    vis_used = vis_end[-1]
    v_ids = jnp.arange(n_vis, dtype=i32)
    live = v_ids < vis_used
    v_eff = jnp.minimum(v_ids, vis_used - 1)
    vis_expert = jnp.sum((v_eff[:, None] >= vis_end[None, :]).astype(i32), axis=1)
    of_expert = vis_expert[:, None] == e_ids[None, :]
    pick = lambda table: jnp.sum(jnp.where(of_expert, table[None, :], 0), axis=1)
    nth = v_eff - (pick(vis_end) - pick(vis_per))
    vis_start = pick(sb_start) + c_max * nth
    vis_count = jnp.minimum(c_max, pick(sb_per) - c_max * nth)
    tail = v_ids == vis_used
    vis_start = jnp.where(live, vis_start, jnp.where(tail, sb_used, 0)).astype(i32)
    vis_count = jnp.where(live, vis_count, jnp.where(tail, n_sb - sb_used, 0)).astype(i32)

    xs = gather_norm(x1, g, src_rows.astype(i32))
    y = moe_ffn(xs, vis_expert.astype(i32), vis_start, vis_count, live.astype(i32), wg, wu, wd,
                sub, c_max, tf)
    return moe_combine(x1, gate_full, y, pos, out_norm)


def _final_norm_body(x_ref, g_ref, o_ref):
    o_ref[...] = _rms(x_ref[...], g_ref[...])


def final_norm(x, g, first_block, n_blocks, tm=512):
    d = x.shape[1]
    return pl.pallas_call(
        _final_norm_body,
        grid=(n_blocks,),
        in_specs=[pl.BlockSpec((tm, d), lambda i: (first_block + i, 0)),
                  pl.BlockSpec((1, d), lambda i: (0, 0))],
        out_specs=pl.BlockSpec((tm, d), lambda i: (i, 0)),
        out_shape=jax.ShapeDtypeStruct((n_blocks * tm, d), F32),
        compiler_params=_params(("parallel",)),
        name="final_norm",
    )(x, g.reshape(1, d))


def kernel(x_prompt, x_sample, cache_win_k, cache_win_v, state_conv, cache_mem_k, cache_mem_v,
           mem_prompt, norm_mix, w_in, attn_sinks, att_out_norm, conv_w, conv_b, conv_ln_g,
           conv_ln_b, conv_out_norm, w_out, norm_cross, norm_mem, w_xq, w_xk, w_xv, w_xo,
           norm_ffn, w_gate, w_up, w_down, w_router, we_gate, we_up, we_down, final_norm_g):
    x = (x_prompt.reshape(T_P, D_MODEL), x_sample.transpose(1, 0, 2).reshape(T_S, D_MODEL))
    mem = mem_prompt.reshape(N_MEM, D_MODEL)
    o1 = ATT_W
    o2 = o1 + KV_W
    o3 = o2 + KV_W
    o4 = o3 + CONV_C
    wb = cache_win_k.shape[2]
    mem_k_rows = cache_mem_k.reshape(DEPTH, DEC_BATCH, N_MEM * X_HEADS, X_HEAD_DIM)
    mem_v_rows = cache_mem_v.reshape(DEPTH, DEC_BATCH, N_MEM * X_HEADS, X_HEAD_DIM)
    cache_kt = cache_win_k.transpose(0, 1, 3, 4, 2).reshape(DEPTH, DEC_BATCH, KV_W, wb)
    cache_vt = cache_win_v.transpose(0, 1, 3, 4, 2).reshape(DEPTH, DEC_BATCH, KV_W, wb)
    state_rows = state_conv.transpose(0, 2, 1, 3)
    win_out = state_out = None
    pk, pv, pc, pmk, pmv = [], [], [], [], []
    for l in range(DEPTH):
        wl = w_in[l]
        w_u = jnp.concatenate([wl[:, :o1], wl[:, o3:o4], wl[:, o4:], wl[:, o1:o2], wl[:, o2:o3]],
                              axis=1).astype(BF16)
        u = rms_matmul(x, norm_mix[l], w_u, tm=512, tn=U_W)
        ka, kg, kk, kv = ATT_W, ATT_W + CONV_C, ATT_W + 2 * CONV_C, ATT_W + 2 * CONV_C + KV_W

        att_p = swa_prompt(u, attn_sinks[l], att_out_norm[l])
        conv_p, state_p = conv_prompt(u, conv_w[l], conv_b[l], conv_ln_g[l], conv_ln_b[l],
                                      conv_out_norm[l])
        us = u[T_P:].reshape(DEC_SEQ, DEC_BATCH, U_W).transpose(1, 0, 2)
        att_s, win_kt, win_vt = swa_sample(us[:, :, :ATT_W], us[:, :, kk:kv], us[:, :, kv:],
                                           cache_kt, cache_vt, l, win_out,
                                           attn_sinks[l], att_out_norm[l], bt=16)
        win_out = (win_kt, win_vt)
        att_s = att_s.transpose(1, 0, 2).reshape(T_S, ATT_W)
        conv_s, state_out = conv_sample(state_rows, u, l, state_out, conv_w[l], conv_b[l],
                                        conv_ln_g[l], conv_ln_b[l], conv_out_norm[l], bt=16)
        conv_s = conv_s.reshape(T_S, CONV_C)
        x, qx = matmul_add(x, [(att_p, att_s), (conv_p, conv_s)],
                           w_out[l].astype(BF16), tm=256, tn=D_MODEL,
                           then=(norm_cross[l], w_xq[l].astype(BF16)))

        keep = min(WINDOW, SEQ)
        pk.append(u[T_P - keep:T_P, kk:kv].reshape(1, keep, N_KV, HEAD_DIM))
        pv.append(u[T_P - keep:T_P, kv:].reshape(1, keep, N_KV, HEAD_DIM))
        pc.append(state_p.reshape(1, STATE_ROWS, CONV_C))

        mk = rms_matmul(mem, norm_mem[l], w_xk[l].astype(BF16), tm=N_MEM, tn=X_W)
        mv = rms_matmul(mem, norm_mem[l], w_xv[l].astype(BF16), tm=N_MEM, tn=X_W)
        pmk.append(mk.reshape(1, N_MEM, X_HEADS, X_HEAD_DIM))
        pmv.append(mv.reshape(1, N_MEM, X_HEADS, X_HEAD_DIM))
        o_p = cross_prompt(qx, mk, mv)
        q_ht = qx[T_P:].reshape(DEC_SEQ, DEC_BATCH, X_HEADS, X_HEAD_DIM).transpose(1, 2, 0, 3)
        o_s = cross_sample(q_ht.reshape(DEC_BATCH, X_HEADS * DEC_SEQ, X_HEAD_DIM), mem_k_rows,
                           mem_v_rows, l)
        o_s = o_s.reshape(DEC_BATCH, X_HEADS, DEC_SEQ, X_HEAD_DIM).transpose(2, 0, 1, 3)
        o_all = jnp.concatenate([o_p, o_s.reshape(T_S, X_W)], axis=0)
        wxo = w_xo[l].astype(BF16)

        if l % 2 == 0:
            d = l // 2
            x = dense_ffn(x, o_all, wxo, norm_ffn[l], w_gate[d], w_up[d], w_down[d], tm=1088, tf=256)
        else:
            m = l // 2
            last = l == DEPTH - 1
            x = moe_layer(x, o_all, wxo, norm_ffn[l], w_router[m], we_gate[m], we_up[m], we_down[m],
                          out_norm=final_norm_g if last else None)

    if isinstance(x, tuple):
        y_prompt, y_sample = x
    else:
        y_prompt = final_norm(x, final_norm_g, 0, T_P // 512)
        y_sample = final_norm(x, final_norm_g, T_P // 512, T_S // 512)
    y_prompt = y_prompt.reshape(1, SEQ, D_MODEL)
    y_sample = y_sample.reshape(DEC_SEQ, DEC_BATCH, D_MODEL).transpose(1, 0, 2)
    win_k_s, win_v_s = (w.reshape(DEPTH, DEC_BATCH, N_KV, HEAD_DIM, wb).transpose(0, 1, 4, 2, 3)
                        for w in win_out)
    conv_s_state = state_out.transpose(0, 2, 1, 3)
    return (y_prompt, y_sample, jnp.stack(pk), jnp.stack(pv), jnp.stack(pc), jnp.stack(pmk),
            jnp.stack(pmv), win_k_s, win_v_s, conv_s_state)
```

```python
import functools

import jax
import jax.numpy as jnp
import numpy as np
from jax import lax
from jax.experimental import pallas as pl
from jax.experimental.pallas import tpu as pltpu

F32 = jnp.float32
BF16 = jnp.bfloat16

D_MODEL = 2048
SEQ = 8192
DEPTH = 2
DEC_BATCH = 128
DEC_SEQ = 4
HEAD_DIM = 64
ATT_W = 1024
N_HEADS = 16
N_KV = 4
KV_W = N_KV * HEAD_DIM
WINDOW = 128
CONV_C = 1024
CONV_W = 31
N_MEM = 256
X_HEADS = 4
X_HEAD_DIM = 128
X_W = X_HEADS * X_HEAD_DIM
D_FF = 5632
N_EXPERTS = 8
D_FF_E = 7168
EPS = 1e-6
NEG = -1e30

T_P = SEQ
T_S = DEC_BATCH * DEC_SEQ
T_ALL = T_P + T_S

LANES = 128
SUBLANES = 8
HALF = HEAD_DIM
STATE_ROWS = CONV_W - 1
STATE_PAD = 32
KEYS_PAD = 256
VMEM_LIMIT = 56 * 1024 * 1024

SLOPES = [float(2.0 ** (-8.0 * (h + 1) / N_HEADS)) for h in range(N_HEADS)]

U_W = ATT_W + 2 * CONV_C + 2 * KV_W


def _params(sem):
    return pltpu.CompilerParams(dimension_semantics=sem, vmem_limit_bytes=VMEM_LIMIT)


def _rms(x, g):
    r = lax.rsqrt(jnp.mean(x * x, axis=-1, keepdims=True) + EPS)
    return x * r * g


def _sigmoid(x):
    return 1.0 / (1.0 + jnp.exp(-x))


def _split_bf16(x):
    hi = x.astype(BF16)
    return hi, (x - hi.astype(F32)).astype(BF16)


def _parts(x):
    return x if isinstance(x, tuple) else (x,)


def _n_rows(parts):
    return sum(p.shape[0] for p in parts)


def _row_specs(parts, tm, width, col):
    if len(parts) == 1:
        return [pl.BlockSpec((tm, width), lambda i, j: (i, col(j)))]
    n_p = parts[0].shape[0] // tm
    return [pl.BlockSpec((tm, width), lambda i, j: (jnp.minimum(i, n_p - 1), col(j))),
            pl.BlockSpec((tm, width), lambda i, j: (jnp.maximum(i - n_p, 0), col(j)))]


def _row_value(refs, n_p):
    if len(refs) == 1:
        return refs[0][...]
    return jnp.where(pl.program_id(0) < n_p, refs[0][...], refs[1][...])


def _rms_matmul_body(*refs, n_x, n_p):
    x_refs = refs[:n_x]
    g_ref, w_ref, o_ref, xn_ref = refs[n_x:]

    @pl.when(pl.program_id(1) == 0)
    def _():
        xn_ref[...] = _rms(_row_value(x_refs, n_p), g_ref[...]).astype(BF16)

    o_ref[...] = jnp.dot(xn_ref[...], w_ref[...], preferred_element_type=F32).astype(o_ref.dtype)


def rms_matmul(x, g, w, tm, tn, out_dtype=F32):
    parts = _parts(x)
    m, k = _n_rows(parts), parts[0].shape[1]
    n = w.shape[1]
    return pl.pallas_call(
        functools.partial(_rms_matmul_body, n_x=len(parts), n_p=parts[0].shape[0] // tm),
        grid=(m // tm, n // tn),
        in_specs=_row_specs(parts, tm, k, lambda j: 0) + [
            pl.BlockSpec((1, k), lambda i, j: (0, 0)),
            pl.BlockSpec((k, tn), lambda i, j: (0, j))],
        out_specs=pl.BlockSpec((tm, tn), lambda i, j: (i, j)),
        out_shape=jax.ShapeDtypeStruct((m, n), out_dtype),
        scratch_shapes=[pltpu.VMEM((tm, k), BF16)],
        compiler_params=_params(("parallel", "arbitrary")),
        name="rms_matmul",
    )(*parts, g.reshape(1, k), w)


def _matmul_add_body(*refs, counts, n_p, n_next):
    n_out = 1 + (n_next > 0)
    n_terms = len(counts) - 1
    n_in = len(refs) - n_out
    w_refs = refs[n_in - n_next - n_terms:n_in - n_next]
    at = counts[0]
    acc = _row_value(refs[:at], n_p)
    for t in range(n_terms):
        a = _row_value(refs[at:at + counts[1 + t]], n_p)
        at += counts[1 + t]
        acc = acc + jnp.dot(a, w_refs[t][...], preferred_element_type=F32)
    refs[n_in][...] = acc
    if n_next:
        g_ref, w2_ref = refs[n_in - 2:n_in]
        refs[n_in + 1][...] = jnp.dot(_rms(acc, g_ref[...]).astype(BF16), w2_ref[...],
                                      preferred_element_type=F32)


def matmul_add(x, a_list, w, tm, tn, then=None):
    x_parts = _parts(x)
    a_parts = [_parts(a) for a in a_list]
    m, n = _n_rows(x_parts), x_parts[0].shape[1]
    in_specs = _row_specs(x_parts, tm, tn, lambda j: j)
    operands = list(x_parts)
    for parts in a_parts:
        in_specs += _row_specs(parts, tm, parts[0].shape[1], lambda j: 0)
        operands += list(parts)
    row = 0
    for parts in a_parts:
        ka = parts[0].shape[1]
        in_specs.append(pl.BlockSpec((ka, tn), lambda i, j, blk=row // ka: (blk, j)))
        operands.append(w)
        row += ka
    out_specs = [pl.BlockSpec((tm, tn), lambda i, j: (i, j))]
    out_shape = [jax.ShapeDtypeStruct((m, n), F32)]
    if then is not None:
        assert tn == n
        gain, w2 = then
        n2 = w2.shape[1]
        in_specs += [pl.BlockSpec((1, n), lambda i, j: (0, 0)),
                     pl.BlockSpec((n, n2), lambda i, j: (0, 0))]
        operands += [gain.reshape(1, n), w2]
        out_specs.append(pl.BlockSpec((tm, n2), lambda i, j: (i, 0)))
        out_shape.append(jax.ShapeDtypeStruct((m, n2), F32))
    counts = (len(x_parts),) + tuple(len(p) for p in a_parts)
    out = pl.pallas_call(
        functools.partial(_matmul_add_body, counts=counts, n_p=T_P // tm,
                          n_next=0 if then is None else 2),
        grid=(m // tm, n // tn),
        in_specs=in_specs,
        out_specs=out_specs,
        out_shape=out_shape,
        input_output_aliases={0: 0} if len(x_parts) == 1 else {},
        compiler_params=_params(("parallel", "parallel")),
        name="matmul_add",
    )(*operands)
    return out[0] if then is None else out


def _half_mask(shape, half):
    lane = lax.broadcasted_iota(jnp.int32, shape, len(shape) - 1)
    return (lane >= HALF) if half else (lane < HALF)


def _sink_softmax_pv(s, sink, v_tile, batched):
    m = jnp.maximum(jnp.max(s, axis=-1, keepdims=True), sink)
    p = jnp.exp(s - m)
    den = jnp.sum(p, axis=-1, keepdims=True) + jnp.exp(sink - m)
    if batched:
        o = jnp.einsum("bqk,bkd->bqd", p.astype(BF16), v_tile, preferred_element_type=F32)
    else:
        o = jnp.dot(p.astype(BF16), v_tile, preferred_element_type=F32)
    return o / den


def _swa_prompt_body(sink_ref, q_ref, kp_ref, kc_ref, vp_ref, vc_ref, ga_ref, o_ref):
    i = pl.program_id(0)
    blk = WINDOW
    q = q_ref[...] * (HEAD_DIM ** -0.5)
    k = jnp.concatenate([kp_ref[...], kc_ref[...]], axis=0).astype(BF16)
    v = jnp.concatenate([vp_ref[...], vc_ref[...]], axis=0).astype(BF16)
    row = lax.broadcasted_iota(jnp.int32, (blk, 2 * blk), 0)
    col = lax.broadcasted_iota(jnp.int32, (blk, 2 * blk), 1)
    dist = blk + row - col
    valid = (dist >= 0) & (dist <= WINDOW) & ((col >= blk) | (i > 0))
    distf = dist.astype(F32)
    tiles = [None] * (N_HEADS // 2)
    for n in range(N_KV):
        half = n % 2
        ts = slice((n // 2) * LANES, (n // 2 + 1) * LANES)
        k_m = jnp.where(_half_mask((2 * blk, LANES), half), k[:, ts], jnp.zeros((), BF16))
        v_t = v[:, ts]
        q_al = jnp.concatenate([q[:, (2 * n) * LANES:(2 * n + 1) * LANES],
                                q[:, (2 * n + 1) * LANES:(2 * n + 2) * LANES]], axis=0)
        q_mis = pltpu.roll(q_al, HALF, 1)
        lhs = jnp.concatenate([q_al, q_mis], axis=0).astype(BF16)
        s = lax.dot_general(lhs, k_m, (((1,), (1,)), ((), ())), preferred_element_type=F32)
        outs = []
        for b4 in range(4):
            a, j = divmod(b4, 2)
            h = 4 * n + 2 * j + (half if a == 0 else 1 - half)
            sb = s[b4 * blk:(b4 + 1) * blk]
            sb = jnp.where(valid, sb - SLOPES[h] * distf, NEG)
            outs.append(_sink_softmax_pv(sb, sink_ref[h], v_t, False))
        keep = _half_mask((blk, LANES), half)
        for j in range(2):
            tiles[2 * n + j] = jnp.where(keep, outs[j], pltpu.roll(outs[2 + j], HALF, 1))
    att = jnp.concatenate(tiles, axis=1)
    o_ref[...] = _rms(att, ga_ref[...]).astype(o_ref.dtype)


def swa_prompt(u, sinks, g_att):
    blk = WINDOW
    nb = T_P // blk
    kcol = (ATT_W + 2 * CONV_C) // KV_W
    prev = lambda i: jnp.maximum(i - 1, 0)
    return pl.pallas_call(
        _swa_prompt_body,
        grid=(nb,),
        in_specs=[pl.BlockSpec(memory_space=pltpu.SMEM),
                  pl.BlockSpec((blk, ATT_W), lambda i: (i, 0)),
                  pl.BlockSpec((blk, KV_W), lambda i: (prev(i), kcol)),
                  pl.BlockSpec((blk, KV_W), lambda i: (i, kcol)),
                  pl.BlockSpec((blk, KV_W), lambda i: (prev(i), kcol + 1)),
                  pl.BlockSpec((blk, KV_W), lambda i: (i, kcol + 1)),
                  pl.BlockSpec((1, ATT_W), lambda i: (0, 0))],
        out_specs=pl.BlockSpec((blk, ATT_W), lambda i: (i, 0)),
        out_shape=jax.ShapeDtypeStruct((T_P, ATT_W), BF16),
        compiler_params=_params(("parallel",)),
        name="swa_prompt",
    )(sinks, u, u, u, u, u, g_att.reshape(1, ATT_W))


def _layer_view(out_ref, layer):
    if len(out_ref.shape) == 3:
        return out_ref
    for other in range(out_ref.shape[0]):
        if other != layer:
            out_ref[other] = jnp.zeros(out_ref.shape[1:], out_ref.dtype)
    return out_ref.at[layer]


def _swa_sample_body(sink_ref, q_ref, kn_ref, vn_ref, kc_ref, vc_ref, bias_ref, ga_ref, *rest,
                     layer):
    o_ref, ko_ref, vo_ref, kall_ref, vall_ref = rest[-5:]
    bt = q_ref.shape[0]
    wb = kc_ref.shape[2]
    for new_ref, cache_ref, out_ref, all_ref in ((kn_ref, kc_ref, ko_ref, kall_ref),
                                                 (vn_ref, vc_ref, vo_ref, vall_ref)):
        out_ref = _layer_view(out_ref, layer)
        all_ref[:, wb:wb + DEC_SEQ, :] = new_ref[...]
        all_ref[:, wb + DEC_SEQ:, :] = jnp.zeros((bt, KEYS_PAD - wb - DEC_SEQ, KV_W), F32)
        for b in range(bt):
            all_ref[b, 0:wb, :] = cache_ref[b].T
        for b in range(bt):
            out_ref[b] = all_ref[b, DEC_SEQ:DEC_SEQ + wb, :].T

    q = q_ref[...] * (HEAD_DIM ** -0.5)
    k = kall_ref[...].astype(BF16)
    v = vall_ref[...].astype(BF16)
    tiles = [None] * (N_HEADS // 2)
    for n in range(N_KV):
        half = n % 2
        ts = slice((n // 2) * LANES, (n // 2 + 1) * LANES)
        k_m = jnp.where(_half_mask((bt, KEYS_PAD, LANES), half), k[:, :, ts], jnp.zeros((), BF16))
        v_t = v[:, :, ts]
        q_al = jnp.concatenate([q[:, :, (2 * n) * LANES:(2 * n + 1) * LANES],
                                q[:, :, (2 * n + 1) * LANES:(2 * n + 2) * LANES]], axis=1)
        q_mis = pltpu.roll(q_al, HALF, 2)
        lhs = jnp.concatenate([q_al, q_mis], axis=1).astype(BF16)
        s = jnp.einsum("bqd,bkd->bqk", lhs, k_m, preferred_element_type=F32)
        bias = bias_ref[n]
        s = jnp.where(bias > 0.5 * NEG, s + bias, NEG)
        sink = sink_ref[n]
        o = _sink_softmax_pv(s, sink, v_t, True)
        keep = _half_mask((bt, DEC_SEQ, LANES), half)
        for j in range(2):
            o_al = o[:, j * DEC_SEQ:(j + 1) * DEC_SEQ]
            o_mis = pltpu.roll(o[:, (2 + j) * DEC_SEQ:(3 + j) * DEC_SEQ], HALF, 2)
            tiles[2 * n + j] = jnp.where(keep, o_al, o_mis)
    att = jnp.concatenate(tiles, axis=2)
    o_ref[...] = _rms(att, ga_ref[...]).astype(o_ref.dtype)


def _sample_bias_and_sinks(sinks):
    wb = WINDOW
    bias = np.full((N_KV, 4 * DEC_SEQ, KEYS_PAD), NEG, np.float32)
    head = np.zeros((N_KV, 4 * DEC_SEQ), np.int32)
    for n in range(N_KV):
        half = n % 2
        for a in range(2):
            for j in range(2):
                h = 4 * n + 2 * j + (half if a == 0 else 1 - half)
                for t in range(DEC_SEQ):
                    r = (2 * a + j) * DEC_SEQ + t
                    head[n, r] = h
                    for kk in range(wb + DEC_SEQ):
                        d = t + wb - kk
                        if 0 <= d <= WINDOW:
                            bias[n, r, kk] = -SLOPES[h] * d
    sink_rows = sinks[jnp.asarray(head)][..., None]
    return jnp.asarray(bias), sink_rows


def swa_sample(q_s, k_new, v_new, cache_kt, cache_vt, layer, prev_out, sinks, g_att, bt):
    wb = cache_kt.shape[3]
    bias, sink_rows = _sample_bias_and_sinks(sinks)
    nq = 4 * DEC_SEQ
    b3 = lambda i: (i, 0, 0)
    z3 = lambda i: (0, 0, 0)
    cache_spec = pl.BlockSpec((None, bt, KV_W, wb), lambda i: (layer, i, 0, 0))
    in_specs = [pl.BlockSpec((N_KV, nq, 1), z3),
                pl.BlockSpec((bt, DEC_SEQ, ATT_W), b3),
                pl.BlockSpec((bt, DEC_SEQ, KV_W), b3),
                pl.BlockSpec((bt, DEC_SEQ, KV_W), b3),
                cache_spec, cache_spec,
                pl.BlockSpec((N_KV, nq, KEYS_PAD), z3),
                pl.BlockSpec((1, 1, ATT_W), z3)]
    operands = [sink_rows, q_s, k_new, v_new, cache_kt, cache_vt, bias, g_att.reshape(1, 1, ATT_W)]
    aliases = {}
    out_cache_spec = cache_spec
    if prev_out is None:
        out_cache_spec = pl.BlockSpec((DEPTH, bt, KV_W, wb), lambda i: (0, i, 0, 0))
    else:
        in_specs += [pl.BlockSpec(memory_space=pl.ANY)] * 2
        operands += list(prev_out)
        aliases = {len(operands) - 2: 1, len(operands) - 1: 2}
    return pl.pallas_call(
        functools.partial(_swa_sample_body, layer=layer),
        grid=(DEC_BATCH // bt,),
        in_specs=in_specs,
        out_specs=[pl.BlockSpec((bt, DEC_SEQ, ATT_W), b3), out_cache_spec, out_cache_spec],
        out_shape=[jax.ShapeDtypeStruct((DEC_BATCH, DEC_SEQ, ATT_W), BF16),
                   jax.ShapeDtypeStruct(cache_kt.shape, F32),
                   jax.ShapeDtypeStruct(cache_vt.shape, F32)],
        input_output_aliases=aliases,
        scratch_shapes=[pltpu.VMEM((bt, KEYS_PAD, KV_W), F32),
                        pltpu.VMEM((bt, KEYS_PAD, KV_W), F32)],
        compiler_params=_params(("parallel",)),
        name="swa_sample",
    )(*operands)


def _conv_post(y, cb, lg, lb, gc):
    y = y + cb
    mu = jnp.mean(y, axis=-1, keepdims=True)
    yc = y - mu
    z = yc * lax.rsqrt(jnp.mean(yc * yc, axis=-1, keepdims=True) + EPS) * lg + lb
    c = z * _sigmoid(z)
    return _rms(c, gc)


def _conv_prompt_body(ap_ref, gp_ref, a_ref, g_ref, cw_ref, cb_ref, lg_ref, lb_ref, gc_ref,
                      o_ref, st_ref, ext_ref, y_ref, *, tt, tc):
    i = pl.program_id(0)
    glu_prev = ap_ref[...] * _sigmoid(gp_ref[...])
    ext_ref[0:STATE_PAD, :] = jnp.where(i > 0, glu_prev, 0.0)
    glu = a_ref[...] * _sigmoid(g_ref[...])
    ext_ref[STATE_PAD:STATE_PAD + tt, :] = glu
    st_ref[...] = glu[tt - STATE_PAD:]
    first = STATE_PAD - STATE_ROWS
    for cblk in range(CONV_C // LANES):
        cs = slice(cblk * LANES, (cblk + 1) * LANES)
        for t0 in range(0, tt, tc):
            total = None
            for lo in range(SUBLANES):
                n_hi = (CONV_W - lo + SUBLANES - 1) // SUBLANES
                start = t0 + first + lo
                win = ext_ref[start:start + tc + SUBLANES * (n_hi - 1), cs]
                acc = None
                for hi in range(n_hi):
                    j = SUBLANES * hi + lo
                    term = win[SUBLANES * hi:SUBLANES * hi + tc] * cw_ref[j:j + 1, cs]
                    acc = term if acc is None else acc + term
                total = acc if total is None else total + acc
            y_ref[t0:t0 + tc, cs] = total
    o_ref[...] = _conv_post(y_ref[...], cb_ref[...], lg_ref[...], lb_ref[...],
                            gc_ref[...]).astype(o_ref.dtype)


def conv_prompt(u, conv_w, conv_b, ln_g, ln_b, g_conv, tt=128, tc=64):
    nt = T_P // tt
    per = tt // STATE_PAD
    prev = lambda i: jnp.maximum(i * per - 1, 0)
    acol, gcol = ATT_W // CONV_C, ATT_W // CONV_C + 1
    vec = lambda: pl.BlockSpec((1, CONV_C), lambda i: (0, 0))
    out, state = pl.pallas_call(
        functools.partial(_conv_prompt_body, tt=tt, tc=tc),
        grid=(nt,),
        in_specs=[pl.BlockSpec((STATE_PAD, CONV_C), lambda i: (prev(i), acol)),
                  pl.BlockSpec((STATE_PAD, CONV_C), lambda i: (prev(i), gcol)),
                  pl.BlockSpec((tt, CONV_C), lambda i: (i, acol)),
                  pl.BlockSpec((tt, CONV_C), lambda i: (i, gcol)),
                  pl.BlockSpec((CONV_W, CONV_C), lambda i: (0, 0)),
                  vec(), vec(), vec(), vec()],
        out_specs=[pl.BlockSpec((tt, CONV_C), lambda i: (i, 0)),
                   pl.BlockSpec((STATE_PAD, CONV_C), lambda i: (0, 0))],
        out_shape=[jax.ShapeDtypeStruct((T_P, CONV_C), BF16),
                   jax.ShapeDtypeStruct((STATE_PAD, CONV_C), F32)],
        scratch_shapes=[pltpu.VMEM((STATE_PAD + tt, CONV_C), F32),
                        pltpu.VMEM((tt, CONV_C), F32)],
        compiler_params=_params(("arbitrary",)),
        name="conv_prompt",
    )(u, u, u, u, conv_w, conv_b.reshape(1, -1), ln_g.reshape(1, -1), ln_b.reshape(1, -1),
      g_conv.reshape(1, -1))
    return out, state[STATE_PAD - STATE_ROWS:]


def _conv_sample_body(st_ref, *rest, layer):
    ag_refs = rest[:2 * DEC_SEQ]
    cw_ref, cb_ref, lg_ref, lb_ref, gc_ref = rest[2 * DEC_SEQ:2 * DEC_SEQ + 5]
    o_ref, so_ref = rest[-2:]
    so_ref = _layer_view(so_ref, layer)
    glu =[ag_refs[t][...] * _sigmoid(ag_refs[DEC_SEQ + t][...]) for t in range(DEC_SEQ)]

    def slab(r):
        return st_ref[r] if r < STATE_ROWS else glu[r - STATE_ROWS]

    for r in range(STATE_ROWS):
        so_ref[r] = slab(r + DEC_SEQ)
    for t in range(DEC_SEQ):
        y = None
        for j in range(CONV_W):
            term = slab(t + j) * cw_ref[j:j + 1, :]
            y = term if y is None else y + term
        o_ref[t] = _conv_post(y, cb_ref[...], lg_ref[...], lb_ref[...],
                              gc_ref[...]).astype(o_ref.dtype)


def conv_sample(state_rows, u, layer, prev_out, conv_w, conv_b, ln_g, ln_b, g_conv, bt):
    acol, gcol = ATT_W // CONV_C, ATT_W // CONV_C + 1
    vec = lambda: pl.BlockSpec((1, CONV_C), lambda i: (0, 0))
    st_spec = pl.BlockSpec((None, STATE_ROWS, bt, CONV_C), lambda i: (layer, 0, i, 0))
    step_spec = lambda t, col: pl.BlockSpec(
        (bt, CONV_C), lambda i: ((T_P + t * DEC_BATCH) // bt + i, col))
    in_specs = ([st_spec] + [step_spec(t, acol) for t in range(DEC_SEQ)]
                + [step_spec(t, gcol) for t in range(DEC_SEQ)]
                + [pl.BlockSpec((CONV_W, CONV_C), lambda i: (0, 0)), vec(), vec(), vec(), vec()])
    operands = [state_rows] + [u] * (2 * DEC_SEQ) + [
        conv_w, conv_b.reshape(1, -1), ln_g.reshape(1, -1), ln_b.reshape(1, -1),
        g_conv.reshape(1, -1)]
    aliases = {}
    out_st_spec = st_spec
    if prev_out is None:
        out_st_spec = pl.BlockSpec((DEPTH, STATE_ROWS, bt, CONV_C), lambda i: (0, 0, i, 0))
    else:
        in_specs.append(pl.BlockSpec(memory_space=pl.ANY))
        operands.append(prev_out)
        aliases = {len(operands) - 1: 1}
    return pl.pallas_call(
        functools.partial(_conv_sample_body, layer=layer),
        grid=(DEC_BATCH // bt,),
        in_specs=in_specs,
        out_specs=[pl.BlockSpec((DEC_SEQ, bt, CONV_C), lambda i: (0, i, 0)), out_st_spec],
        out_shape=[jax.ShapeDtypeStruct((DEC_SEQ, DEC_BATCH, CONV_C), BF16),
                   jax.ShapeDtypeStruct(state_rows.shape, F32)],
        input_output_aliases=aliases,
        compiler_params=_params(("parallel",)),
        name="conv_sample",
    )(*operands)


def _cross_prompt_body(q_ref, k_ref, v_ref, o_ref):
    q = q_ref[...].astype(BF16)
    k = k_ref[...].astype(BF16)
    v = v_ref[...].astype(BF16)
    outs = []
    for h in range(X_HEADS):
        hs = slice(h * X_HEAD_DIM, (h + 1) * X_HEAD_DIM)
        s = lax.dot_general(q[:, hs], k[:, hs], (((1,), (1,)), ((), ())),
                            preferred_element_type=F32) * (X_HEAD_DIM ** -0.5)
        p = jnp.exp(s - jnp.max(s, axis=-1, keepdims=True))
        den = jnp.sum(p, axis=-1, keepdims=True)
        outs.append(jnp.dot(p.astype(BF16), v[:, hs], preferred_element_type=F32) / den)
    o_ref[...] = jnp.concatenate(outs, axis=1).astype(o_ref.dtype)


def cross_prompt(q, mem_k, mem_v, tq=512):
    return pl.pallas_call(
        _cross_prompt_body,
        grid=(T_P // tq,),
        in_specs=[pl.BlockSpec((tq, X_W), lambda i: (i, 0)),
                  pl.BlockSpec((N_MEM, X_W), lambda i: (0, 0)),
                  pl.BlockSpec((N_MEM, X_W), lambda i: (0, 0))],
        out_specs=pl.BlockSpec((tq, X_W), lambda i: (i, 0)),
        out_shape=jax.ShapeDtypeStruct((T_P, X_W), BF16),
        compiler_params=_params(("parallel",)),
        name="cross_prompt",
    )(q, mem_k, mem_v)


def _cross_sample_body(q_ref, k_ref, v_ref, o_ref):
    q = q_ref[...].astype(BF16)
    k = k_ref[...].astype(BF16)
    v = v_ref[...].astype(BF16)
    s = jnp.einsum("bqd,bkd->bqk", q, k, preferred_element_type=F32) * (X_HEAD_DIM ** -0.5)
    row = lax.broadcasted_iota(jnp.int32, s.shape[1:], 0)
    col = lax.broadcasted_iota(jnp.int32, s.shape[1:], 1)
    same_head = (col % X_HEADS) == (row // DEC_SEQ)
    s = jnp.where(same_head, s, NEG)
    p = jnp.exp(s - jnp.max(s, axis=-1, keepdims=True))
    den = jnp.sum(p, axis=-1, keepdims=True)
    o = jnp.einsum("bqk,bkd->bqd", p.astype(BF16), v, preferred_element_type=F32) / den
    o_ref[...] = o.astype(o_ref.dtype)


def cross_sample(q_ht, mem_k, mem_v, layer, bt=8):
    rows = X_HEADS * DEC_SEQ
    b3 = lambda i: (i, 0, 0)
    b4 = lambda i: (layer, i, 0, 0)
    return pl.pallas_call(
        _cross_sample_body,
        grid=(DEC_BATCH // bt,),
        in_specs=[pl.BlockSpec((bt, rows, X_HEAD_DIM), b3),
                  pl.BlockSpec((None, bt, N_MEM * X_HEADS, X_HEAD_DIM), b4),
                  pl.BlockSpec((None, bt, N_MEM * X_HEADS, X_HEAD_DIM), b4)],
        out_specs=pl.BlockSpec((bt, rows, X_HEAD_DIM), b3),
        out_shape=jax.ShapeDtypeStruct((DEC_BATCH, rows, X_HEAD_DIM), BF16),
        compiler_params=_params(("parallel",)),
        name="cross_sample",
    )(q_ht, mem_k, mem_v)


def _swiglu_step(x, wg_ref, wu_ref, wd_ref):
    hg = jnp.dot(x, wg_ref[...].astype(BF16), preferred_element_type=F32)
    hu = jnp.dot(x, wu_ref[...].astype(BF16), preferred_element_type=F32)
    h = (hg * _sigmoid(hg) * hu).astype(BF16)
    return jnp.dot(h, wd_ref[...].astype(BF16), preferred_element_type=F32)


def _dense_ffn_body(x_ref, a_ref, wa_ref, g_ref, wg_ref, wu_ref, wd_ref, o_ref, xn_ref):
    @pl.when(pl.program_id(1) == 0)
    def _():
        x = x_ref[...] + jnp.dot(a_ref[...], wa_ref[...], preferred_element_type=F32)
        xn_ref[...] = _rms(x, g_ref[...]).astype(BF16)
        o_ref[...] = x

    o_ref[...] += _swiglu_step(xn_ref[...], wg_ref, wu_ref, wd_ref)


def dense_ffn(x, a, wa, g, wg, wu, wd, tm, tf):
    m, d = x.shape
    ka = a.shape[1]
    dff = wg.shape[1]
    return pl.pallas_call(
        _dense_ffn_body,
        grid=(m // tm, dff // tf),
        in_specs=[pl.BlockSpec((tm, d), lambda i, f: (i, 0), pipeline_mode=pl.Buffered(1)),
                  pl.BlockSpec((tm, ka), lambda i, f: (i, 0)),
                  pl.BlockSpec((ka, d), lambda i, f: (0, 0)),
                  pl.BlockSpec((1, d), lambda i, f: (0, 0)),
                  pl.BlockSpec((d, tf), lambda i, f: (0, f)),
                  pl.BlockSpec((d, tf), lambda i, f: (0, f)),
                  pl.BlockSpec((tf, d), lambda i, f: (f, 0))],
        out_specs=pl.BlockSpec((tm, d), lambda i, f: (i, 0)),
        out_shape=jax.ShapeDtypeStruct((m, d), F32),
        scratch_shapes=[pltpu.VMEM((tm, d), BF16)],
        compiler_params=_params(("parallel", "arbitrary")),
        name="dense_ffn",
    )(x, a, wa, g.reshape(1, d), wg, wu, wd)


def _router_body(x_ref, a_ref, wa_ref, g_ref, wr_ref, x1_ref, idx_ref, gate_ref, cnt_ref,
                 carry_ref):
    @pl.when(pl.program_id(0) == 0)
    def _():
        carry_ref[...] = jnp.zeros(carry_ref.shape, F32)

    x1 = x_ref[...] + jnp.dot(a_ref[...], wa_ref[...], preferred_element_type=F32)
    x1_ref[...] = x1
    xn = _rms(x1, g_ref[...])
    xh, xl = _split_bf16(xn)
    wh, wl = _split_bf16(wr_ref[...])
    logits = (jnp.dot(xh, wh, preferred_element_type=F32)
              + (jnp.dot(xl, wh, preferred_element_type=F32)
                 + jnp.dot(xh, wl, preferred_element_type=F32)))
    tm = logits.shape[0]
    lane = lax.broadcasted_iota(jnp.int32, logits.shape, 1)
    lg = jnp.where(lane < N_EXPERTS, logits, -jnp.inf)
    m1 = jnp.max(lg, axis=-1, keepdims=True)
    i1 = jnp.min(jnp.where(lg == m1, lane, LANES), axis=-1, keepdims=True)
    lg2 = jnp.where(lane == i1, -jnp.inf, lg)
    m2 = jnp.max(lg2, axis=-1, keepdims=True)
    i2 = jnp.min(jnp.where(lg2 == m2, lane, LANES), axis=-1, keepdims=True)
    e = jnp.exp(m2 - m1)
    den = 1.0 + e
    gate_ref[...] = jnp.where(lane == 0, 1.0 / den, jnp.where(lane == 1, e / den, 0.0))

    chosen = (lane == i1) | (lane == i2)
    r = lax.broadcasted_iota(jnp.int32, (tm, tm), 0)
    c = lax.broadcasted_iota(jnp.int32, (tm, tm), 1)
    earlier = jnp.where(c < r, 1.0, 0.0).astype(BF16)
    before = jnp.dot(earlier, jnp.where(chosen, 1.0, 0.0).astype(BF16),
                     preferred_element_type=F32) + carry_ref[...]
    r1 = jnp.sum(jnp.where(lane == i1, before, 0.0), axis=-1, keepdims=True).astype(jnp.int32)
    r2 = jnp.sum(jnp.where(lane == i2, before, 0.0), axis=-1, keepdims=True).astype(jnp.int32)
    idx_ref[...] = jnp.where(lane == 0, i1, jnp.where(lane == 1, i2,
                             jnp.where(lane == 2, r1, jnp.where(lane == 3, r2, 0))))
    carry_ref[...] += jnp.sum(jnp.where(chosen, 1.0, 0.0), axis=0, keepdims=True)
    cnt_ref[...] = carry_ref[...]


def router(x, a, wa, g, w_router, tm=512):
    m, d = x.shape
    ka = a.shape[1]
    wr = jnp.zeros((d, LANES), F32).at[:, :N_EXPERTS].set(w_router)
    return pl.pallas_call(
        _router_body,
        grid=(m // tm,),
        in_specs=[pl.BlockSpec((tm, d), lambda i: (i, 0)),
                  pl.BlockSpec((tm, ka), lambda i: (i, 0)),
                  pl.BlockSpec((ka, d), lambda i: (0, 0)),
                  pl.BlockSpec((1, d), lambda i: (0, 0)),
                  pl.BlockSpec((d, LANES), lambda i: (0, 0))],
        out_specs=[pl.BlockSpec((tm, d), lambda i: (i, 0)),
                   pl.BlockSpec((tm, LANES), lambda i: (i, 0)),
                   pl.BlockSpec((tm, LANES), lambda i: (i, 0)),
                   pl.BlockSpec((1, LANES), lambda i: (0, 0))],
        out_shape=[jax.ShapeDtypeStruct((m, d), F32),
                   jax.ShapeDtypeStruct((m, LANES), jnp.int32),
                   jax.ShapeDtypeStruct((m, LANES), F32),
                   jax.ShapeDtypeStruct((1, LANES), F32)],
        scratch_shapes=[pltpu.VMEM((1, LANES), F32)],
        compiler_params=_params(("arbitrary",)),
        name="router",
    )(x, a, wa, g.reshape(1, d), wr)


def _row_copy(src_hbm, row, dst_ref, r, sem):
    return pltpu.make_async_copy(src_hbm.at[pl.ds(row, 1)], dst_ref.at[pl.ds(r, 1)], sem)


def _gather_norm_body(src_ref, x_hbm, g_ref, o_ref, buf_ref, sem, *, rows):
    base = pl.program_id(0) * rows

    def start(k, c):
        r0 = pl.multiple_of(k * SUBLANES, SUBLANES)
        for s in range(SUBLANES):
            _row_copy(x_hbm, src_ref[base + r0 + s], buf_ref, r0 + s, sem).start()
        return c

    lax.fori_loop(0, rows // SUBLANES, start, 0)
    pltpu.make_async_copy(x_hbm.at[pl.ds(0, rows)], buf_ref, sem).wait()
    o_ref[...] = _rms(buf_ref[...], g_ref[...]).astype(o_ref.dtype)


def gather_norm(x, g, src_rows, rows=1024):
    d = x.shape[1]
    n = src_rows.shape[0]
    grid_spec = pltpu.PrefetchScalarGridSpec(
        num_scalar_prefetch=1,
        grid=(n // rows,),
        in_specs=[pl.BlockSpec(memory_space=pl.ANY),
                  pl.BlockSpec((1, d), lambda i, s: (0, 0))],
        out_specs=pl.BlockSpec((rows, d), lambda i, s: (i, 0)),
        scratch_shapes=[pltpu.VMEM((rows, d), F32), pltpu.SemaphoreType.DMA(())],
    )
    return pl.pallas_call(
        functools.partial(_gather_norm_body, rows=rows),
        grid_spec=grid_spec,
        out_shape=jax.ShapeDtypeStruct((n, d), BF16),
        compiler_params=_params(("arbitrary",)),
        name="gather_norm",
    )(src_rows, x, g.reshape(1, d))


def _moe_ffn_body(ve_ref, vs_ref, vc_ref, vlive_ref, x_hbm, wg_ref, wu_ref, wd_ref, y_hbm,
                  xbuf_ref, acc_ref, wgb_ref, wub_ref, wdb_ref, sem_in, sem_out, *, sub, c_max, nf):
    v = pl.program_id(0)
    f = pl.program_id(1)
    count = vc_ref[v]
    first = vs_ref[v]
    live = vlive_ref[v] > 0

    def rows_of(j):
        return pl.ds(pl.multiple_of(j * sub, sub), sub)

    def copy_in(j):
        src = x_hbm.at[pl.ds(pl.multiple_of((first + j) * sub, sub), sub)]
        return pltpu.make_async_copy(src, xbuf_ref.at[rows_of(j)], sem_in.at[j])

    def copy_out(j):
        dst = y_hbm.at[pl.ds(pl.multiple_of((first + j) * sub, sub), sub)]
        return pltpu.make_async_copy(acc_ref.at[rows_of(j)], dst, sem_out)

    @pl.when(f == 0)
    def _():
        for j in range(c_max):
            @pl.when((j < count) & live)
            def _(j=j):
                copy_in(j).start()
        for j in range(c_max):
            @pl.when(j < count)
            def _(j=j):
                acc_ref[rows_of(j), :] = jnp.zeros((sub, acc_ref.shape[1]), F32)

    @pl.when(live)
    def _():
        wgb_ref[...] = wg_ref[...].astype(BF16)
        wub_ref[...] = wu_ref[...].astype(BF16)
        wdb_ref[...] = wd_ref[...].astype(BF16)

        def chain(j0, n):
            @pl.when(f == 0)
            def _():
                for k in range(n):
                    copy_in(j0 + k).wait()

            rows = pl.ds(pl.multiple_of(j0 * sub, sub), n * sub)
            acc_ref[rows, :] += _swiglu_step(xbuf_ref[rows, :], wgb_ref, wub_ref, wdb_ref)

            @pl.when(f == nf - 1)
            def _():
                for k in range(n):
                    copy_out(j0 + k).start()

        quads = count // 4
        rest = count - 4 * quads
        fold = (rest == 1) & (quads >= 1)
        n_full = jnp.where(fold, quads - 1, quads)

        def quad(i, carry):
            chain(i * 4, 4)
            return carry

        lax.fori_loop(0, n_full, quad, 0)

        @pl.when(fold)
        def _():
            chain(n_full * 4, 5)

        @pl.when(rest >= 2)
        def _():
            chain(quads * 4, 2)

        @pl.when((rest % 2 == 1) & jnp.logical_not(fold))
        def _():
            chain(count - 1, 1)

    @pl.when(f == nf - 1)
    def _():
        for j in range(c_max):
            @pl.when((j < count) & jnp.logical_not(live))
            def _(j=j):
                copy_out(j).start()
        for j in range(c_max):
            @pl.when(j < count)
            def _(j=j):
                copy_out(j).wait()


def moe_ffn(xs, vis_expert, vis_start, vis_count, vis_live, wg, wu, wd, sub, c_max, tf):
    n, d = xs.shape
    dff = wg.shape[2]
    nf = dff // tf
    n_vis = vis_expert.shape[0]
    fi = lambda v, f, live: jnp.where(live[v] > 0, f, nf - 1)
    grid_spec = pltpu.PrefetchScalarGridSpec(
        num_scalar_prefetch=4,
        grid=(n_vis, nf),
        in_specs=[pl.BlockSpec(memory_space=pl.ANY),
                  pl.BlockSpec((None, d, tf), lambda v, f, ve, vs, vc, lv: (ve[v], 0, fi(v, f, lv))),
                  pl.BlockSpec((None, d, tf), lambda v, f, ve, vs, vc, lv: (ve[v], 0, fi(v, f, lv))),
                  pl.BlockSpec((None, tf, d), lambda v, f, ve, vs, vc, lv: (ve[v], fi(v, f, lv), 0))],
        out_specs=pl.BlockSpec(memory_space=pl.ANY),
        scratch_shapes=[pltpu.VMEM((c_max * sub, d), BF16), pltpu.VMEM((c_max * sub, d), F32),
                        pltpu.VMEM((d, tf), BF16), pltpu.VMEM((d, tf), BF16),
                        pltpu.VMEM((tf, d), BF16),
                        pltpu.SemaphoreType.DMA((c_max,)), pltpu.SemaphoreType.DMA(())],
    )
    return pl.pallas_call(
        functools.partial(_moe_ffn_body, sub=sub, c_max=c_max, nf=nf),
        grid_spec=grid_spec,
        out_shape=jax.ShapeDtypeStruct((n, d), F32),
        compiler_params=_params(("arbitrary", "arbitrary")),
        name="moe_ffn",
    )(vis_expert, vis_start, vis_count, vis_live, xs, wg, wu, wd)


def _combine_rows(pos_ref, x_ref, gate_ref, y_hbm, buf_ref, sem, rows, n_tok):
    base = pl.program_id(0) * rows

    def start(k, c):
        r0 = pl.multiple_of(k * SUBLANES, SUBLANES)
        for s in range(SUBLANES):
            _row_copy(y_hbm, pos_ref[base + r0 + s], buf_ref.at[0], r0 + s, sem).start()
            _row_copy(y_hbm, pos_ref[n_tok + base + r0 + s], buf_ref.at[1], r0 + s, sem).start()
        return c

    lax.fori_loop(0, rows // SUBLANES, start, 0)
    for slot in range(2):
        pltpu.make_async_copy(y_hbm.at[pl.ds(0, rows)], buf_ref.at[slot], sem).wait()
    gates = gate_ref[...]
    return x_ref[...] + (gates[:, 0:1] * buf_ref[0] + gates[:, 1:2] * buf_ref[1])


def _combine_body(pos_ref, x_ref, gate_ref, y_hbm, o_ref, buf_ref, sem, *, rows, n_tok):
    o_ref[...] = _combine_rows(pos_ref, x_ref, gate_ref, y_hbm, buf_ref, sem, rows, n_tok)


def _combine_norm_body(pos_ref, x_ref, gate_ref, g_ref, y_hbm, op_ref, os_ref, buf_ref, sem, *,
                       rows, n_tok, n_p):
    out = _rms(_combine_rows(pos_ref, x_ref, gate_ref, y_hbm, buf_ref, sem, rows, n_tok),
               g_ref[...])

    @pl.when(pl.program_id(0) < n_p)
    def _():
        op_ref[...] = out

    @pl.when(pl.program_id(0) >= n_p)
    def _():
        os_ref[...] = out


def moe_combine(x, gates, y, pos, out_norm=None, rows=512):
    n_tok, d = x.shape
    in_specs = [pl.BlockSpec((rows, d), lambda i, p: (i, 0)),
                pl.BlockSpec((rows, LANES), lambda i, p: (i, 0))]
    scratch = [pltpu.VMEM((2, rows, d), F32), pltpu.SemaphoreType.DMA(())]
    any_spec = pl.BlockSpec(memory_space=pl.ANY)
    if out_norm is None:
        return pl.pallas_call(
            functools.partial(_combine_body, rows=rows, n_tok=n_tok),
            grid_spec=pltpu.PrefetchScalarGridSpec(
                num_scalar_prefetch=1, grid=(n_tok // rows,),
                in_specs=in_specs + [any_spec],
                out_specs=pl.BlockSpec((rows, d), lambda i, p: (i, 0)),
                scratch_shapes=scratch),
            out_shape=jax.ShapeDtypeStruct((n_tok, d), F32),
            compiler_params=_params(("arbitrary",)),
            name="moe_combine",
        )(pos, x, gates, y)
    n_p = T_P // rows
    return pl.pallas_call(
        functools.partial(_combine_norm_body, rows=rows, n_tok=n_tok, n_p=n_p),
        grid_spec=pltpu.PrefetchScalarGridSpec(
            num_scalar_prefetch=1, grid=(n_tok // rows,),
            in_specs=in_specs + [pl.BlockSpec((1, d), lambda i, p: (0, 0)), any_spec],
            out_specs=[pl.BlockSpec((rows, d), lambda i, p: (jnp.minimum(i, n_p - 1), 0)),
                       pl.BlockSpec((rows, d), lambda i, p: (jnp.maximum(i - n_p, 0), 0))],
            scratch_shapes=scratch),
        out_shape=[jax.ShapeDtypeStruct((T_P, d), F32), jax.ShapeDtypeStruct((n_tok - T_P, d), F32)],
        compiler_params=_params(("arbitrary",)),
        name="moe_combine_norm",
    )(pos, x, gates, out_norm.reshape(1, d), y)


def moe_layer(x, a, wa, g, w_router, wg, wu, wd, out_norm=None, sub=256, c_max=10, tf=256):
    assert c_max >= N_EXPERTS
    n_tok = x.shape[0]
    i32 = jnp.int32
    x1, idx_full, gate_full, cnt = router(x, a, wa, g, w_router)
    experts = idx_full[:, 0:2]
    ranks = idx_full[:, 2:4]
    counts = cnt[0, :N_EXPERTS].astype(i32)
    n_assign = 2 * n_tok
    n_sb = n_assign // sub + N_EXPERTS
    n_vis = (n_sb + (c_max - 1) * N_EXPERTS) // c_max + 1

    e_ids = jnp.arange(N_EXPERTS, dtype=i32)
    sb_per = (counts + sub - 1) // sub
    sb_end = jnp.cumsum(sb_per)
    sb_start = sb_end - sb_per
    sb_used = sb_end[-1]
    row_start = sb_start * sub

    start_of = jnp.sum(jnp.where(experts[:, :, None] == e_ids, row_start, 0), axis=-1)
    pos = (start_of + ranks).astype(i32).T.reshape(-1)

    n_pad = n_sb * sub - n_assign
    pad_end = jnp.cumsum(sb_per * sub - counts)
    pad_ids = jnp.arange(n_pad, dtype=i32)
    pad_e = jnp.sum((pad_ids[:, None] >= pad_end[None, :]).astype(i32), axis=1)
    keys = jnp.concatenate([experts.reshape(-1) * 2, pad_e * 2 + 1])
    toks = jnp.concatenate([jnp.arange(n_assign, dtype=i32) // 2, (pad_ids * 8) % n_tok])
    _, src_rows = lax.sort((keys, toks), num_keys=1, is_stable=True)

    vis_per = (sb_per + c_max - 1) // c_max
    vis_end = jnp.cumsum(vis_per)
    vis_used = vis_end[-1]
    v_ids = jnp.arange(n_vis, dtype=i32)
    live = v_ids < vis_used
    v_eff = jnp.minimum(v_ids, vis_used - 1)
    vis_expert = jnp.sum((v_eff[:, None] >= vis_end[None, :]).astype(i32), axis=1)
    of_expert = vis_expert[:, None] == e_ids[None, :]
    pick = lambda table: jnp.sum(jnp.where(of_expert, table[None, :], 0), axis=1)
    nth = v_eff - (pick(vis_end) - pick(vis_per))
    vis_start = pick(sb_start) + c_max * nth
    vis_count = jnp.minimum(c_max, pick(sb_per) - c_max * nth)
    tail = v_ids == vis_used
    vis_start = jnp.where(live, vis_start, jnp.where(tail, sb_used, 0)).astype(i32)
    vis_count = jnp.where(live, vis_count, jnp.where(tail, n_sb - sb_used, 0)).astype(i32)

    xs = gather_norm(x1, g, src_rows.astype(i32))
    y = moe_ffn(xs, vis_expert.astype(i32), vis_start, vis_count, live.astype(i32), wg, wu, wd,
                sub, c_max, tf)
    return moe_combine(x1, gate_full, y, pos, out_norm)


def _final_norm_body(x_ref, g_ref, o_ref):
    o_ref[...] = _rms(x_ref[...], g_ref[...])


def final_norm(x, g, first_block, n_blocks, tm=512):
    d = x.shape[1]
    return pl.pallas_call(
        _final_norm_body,
        grid=(n_blocks,),
        in_specs=[pl.BlockSpec((tm, d), lambda i: (first_block + i, 0)),
                  pl.BlockSpec((1, d), lambda i: (0, 0))],
        out_specs=pl.BlockSpec((tm, d), lambda i: (i, 0)),
        out_shape=jax.ShapeDtypeStruct((n_blocks * tm, d), F32),
        compiler_params=_params(("parallel",)),
        name="final_norm",
    )(x, g.reshape(1, d))


def kernel(x_prompt, x_sample, cache_win_k, cache_win_v, state_conv, cache_mem_k, cache_mem_v,
           mem_prompt, norm_mix, w_in, attn_sinks, att_out_norm, conv_w, conv_b, conv_ln_g,
           conv_ln_b, conv_out_norm, w_out, norm_cross, norm_mem, w_xq, w_xk, w_xv, w_xo,
           norm_ffn, w_gate, w_up, w_down, w_router, we_gate, we_up, we_down, final_norm_g):
    x = (x_prompt.reshape(T_P, D_MODEL), x_sample.transpose(1, 0, 2).reshape(T_S, D_MODEL))
    mem = mem_prompt.reshape(N_MEM, D_MODEL)
    o1 = ATT_W
    o2 = o1 + KV_W
    o3 = o2 + KV_W
    o4 = o3 + CONV_C
    wb = cache_win_k.shape[2]
    mem_k_rows = cache_mem_k.reshape(DEPTH, DEC_BATCH, N_MEM * X_HEADS, X_HEAD_DIM)
    mem_v_rows = cache_mem_v.reshape(DEPTH, DEC_BATCH, N_MEM * X_HEADS, X_HEAD_DIM)
    cache_kt = cache_win_k.transpose(0, 1, 3, 4, 2).reshape(DEPTH, DEC_BATCH, KV_W, wb)
    cache_vt = cache_win_v.transpose(0, 1, 3, 4, 2).reshape(DEPTH, DEC_BATCH, KV_W, wb)
    state_rows = state_conv.transpose(0, 2, 1, 3)
    win_out = state_out = None
    pk, pv, pc, pmk, pmv = [], [], [], [], []
    for l in range(DEPTH):
        wl = w_in[l]
        w_u = jnp.concatenate([wl[:, :o1], wl[:, o3:o4], wl[:, o4:], wl[:, o1:o2], wl[:, o2:o3]],
                              axis=1).astype(BF16)
        u = rms_matmul(x, norm_mix[l], w_u, tm=512, tn=U_W)
        ka, kg, kk, kv = ATT_W, ATT_W + CONV_C, ATT_W + 2 * CONV_C, ATT_W + 2 * CONV_C + KV_W

        att_p = swa_prompt(u, attn_sinks[l], att_out_norm[l])
        conv_p, state_p = conv_prompt(u, conv_w[l], conv_b[l], conv_ln_g[l], conv_ln_b[l],
                                      conv_out_norm[l])
        us = u[T_P:].reshape(DEC_SEQ, DEC_BATCH, U_W).transpose(1, 0, 2)
        att_s, win_kt, win_vt = swa_sample(us[:, :, :ATT_W], us[:, :, kk:kv], us[:, :, kv:],
                                           cache_kt, cache_vt, l, win_out,
                                           attn_sinks[l], att_out_norm[l], bt=16)
        win_out = (win_kt, win_vt)
        att_s = att_s.transpose(1, 0, 2).reshape(T_S, ATT_W)
        conv_s, state_out = conv_sample(state_rows, u, l, state_out, conv_w[l], conv_b[l],
                                        conv_ln_g[l], conv_ln_b[l], conv_out_norm[l], bt=16)
        conv_s = conv_s.reshape(T_S, CONV_C)
        x, qx = matmul_add(x, [(att_p, att_s), (conv_p, conv_s)],
                           w_out[l].astype(BF16), tm=512, tn=D_MODEL,
                           then=(norm_cross[l], w_xq[l].astype(BF16)))

        keep = min(WINDOW, SEQ)
        pk.append(u[T_P - keep:T_P, kk:kv].reshape(1, keep, N_KV, HEAD_DIM))
        pv.append(u[T_P - keep:T_P, kv:].reshape(1, keep, N_KV, HEAD_DIM))
        pc.append(state_p.reshape(1, STATE_ROWS, CONV_C))

        mk = rms_matmul(mem, norm_mem[l], w_xk[l].astype(BF16), tm=N_MEM, tn=X_W)
        mv = rms_matmul(mem, norm_mem[l], w_xv[l].astype(BF16), tm=N_MEM, tn=X_W)
        pmk.append(mk.reshape(1, N_MEM, X_HEADS, X_HEAD_DIM))
        pmv.append(mv.reshape(1, N_MEM, X_HEADS, X_HEAD_DIM))
        o_p = cross_prompt(qx, mk, mv)
        q_ht = qx[T_P:].reshape(DEC_SEQ, DEC_BATCH, X_HEADS, X_HEAD_DIM).transpose(1, 2, 0, 3)
        o_s = cross_sample(q_ht.reshape(DEC_BATCH, X_HEADS * DEC_SEQ, X_HEAD_DIM), mem_k_rows,
                           mem_v_rows, l)
        o_s = o_s.reshape(DEC_BATCH, X_HEADS, DEC_SEQ, X_HEAD_DIM).transpose(2, 0, 1, 3)
        o_all = jnp.concatenate([o_p, o_s.reshape(T_S, X_W)], axis=0)
        wxo = w_xo[l].astype(BF16)

        if l % 2 == 0:
            d = l // 2
            x = dense_ffn(x, o_all, wxo, norm_ffn[l], w_gate[d], w_up[d], w_down[d], tm=1088, tf=256)
        else:
            m = l // 2
            last = l == DEPTH - 1
            x = moe_layer(x, o_all, wxo, norm_ffn[l], w_router[m], we_gate[m], we_up[m], we_down[m],
                          out_norm=final_norm_g if last else None)

    if isinstance(x, tuple):
        y_prompt, y_sample = x
    else:
        y_prompt = final_norm(x, final_norm_g, 0, T_P // 512)
        y_sample = final_norm(x, final_norm_g, T_P // 512, T_S // 512)
    y_prompt = y_prompt.reshape(1, SEQ, D_MODEL)
    y_sample = y_sample.reshape(DEC_SEQ, DEC_BATCH, D_MODEL).transpose(1, 0, 2)
    win_k_s, win_v_s = (w.reshape(DEPTH, DEC_BATCH, N_KV, HEAD_DIM, wb).transpose(0, 1, 4, 2, 3)
                        for w in win_out)
    conv_s_state = state_out.transpose(0, 2, 1, 3)
    return (y_prompt, y_sample, jnp.stack(pk), jnp.stack(pv), jnp.stack(pc), jnp.stack(pmk),
            jnp.stack(pmv), win_k_s, win_v_s, conv_s_state)
```

```python
import functools

import jax
import jax.numpy as jnp
import numpy as np
from jax import lax
from jax.experimental import pallas as pl
from jax.experimental.pallas import tpu as pltpu

F32 = jnp.float32
BF16 = jnp.bfloat16

D_MODEL = 2048
SEQ = 8192
DEPTH = 2
DEC_BATCH = 128
DEC_SEQ = 4
HEAD_DIM = 64
ATT_W = 1024
N_HEADS = 16
N_KV = 4
KV_W = N_KV * HEAD_DIM
WINDOW = 128
CONV_C = 1024
CONV_W = 31
N_MEM = 256
X_HEADS = 4
X_HEAD_DIM = 128
X_W = X_HEADS * X_HEAD_DIM
D_FF = 5632
N_EXPERTS = 8
D_FF_E = 7168
EPS = 1e-6
NEG = -1e30

T_P = SEQ
T_S = DEC_BATCH * DEC_SEQ
T_ALL = T_P + T_S

LANES = 128
SUBLANES = 8
HALF = HEAD_DIM
STATE_ROWS = CONV_W - 1
STATE_PAD = 32
KEYS_PAD = 256
VMEM_LIMIT = 56 * 1024 * 1024

SLOPES = [float(2.0 ** (-8.0 * (h + 1) / N_HEADS)) for h in range(N_HEADS)]

U_W = ATT_W + 2 * CONV_C + 2 * KV_W


def _params(sem):
    return pltpu.CompilerParams(dimension_semantics=sem, vmem_limit_bytes=VMEM_LIMIT)


def _rms(x, g):
    r = lax.rsqrt(jnp.mean(x * x, axis=-1, keepdims=True) + EPS)
    return x * r * g


def _sigmoid(x):
    return 1.0 / (1.0 + jnp.exp(-x))


def _split_bf16(x):
    hi = x.astype(BF16)
    return hi, (x - hi.astype(F32)).astype(BF16)


def _parts(x):
    return x if isinstance(x, tuple) else (x,)


def _n_rows(parts):
    return sum(p.shape[0] for p in parts)


def _row_specs(parts, tm, width, col):
    if len(parts) == 1:
        return [pl.BlockSpec((tm, width), lambda i, j: (i, col(j)))]
    n_p = parts[0].shape[0] // tm
    return [pl.BlockSpec((tm, width), lambda i, j: (jnp.minimum(i, n_p - 1), col(j))),
            pl.BlockSpec((tm, width), lambda i, j: (jnp.maximum(i - n_p, 0), col(j)))]


def _row_value(refs, n_p):
    if len(refs) == 1:
        return refs[0][...]
    return jnp.where(pl.program_id(0) < n_p, refs[0][...], refs[1][...])


def _rms_matmul_body(*refs, n_x, n_p):
    x_refs = refs[:n_x]
    g_ref, w_ref, o_ref, xn_ref = refs[n_x:]

    @pl.when(pl.program_id(1) == 0)
    def _():
        xn_ref[...] = _rms(_row_value(x_refs, n_p), g_ref[...]).astype(BF16)

    o_ref[...] = jnp.dot(xn_ref[...], w_ref[...], preferred_element_type=F32).astype(o_ref.dtype)


def rms_matmul(x, g, w, tm, tn, out_dtype=F32):
    parts = _parts(x)
    m, k = _n_rows(parts), parts[0].shape[1]
    n = w.shape[1]
    return pl.pallas_call(
        functools.partial(_rms_matmul_body, n_x=len(parts), n_p=parts[0].shape[0] // tm),
        grid=(m // tm, n // tn),
        in_specs=_row_specs(parts, tm, k, lambda j: 0) + [
            pl.BlockSpec((1, k), lambda i, j: (0, 0)),
            pl.BlockSpec((k, tn), lambda i, j: (0, j))],
        out_specs=pl.BlockSpec((tm, tn), lambda i, j: (i, j)),
        out_shape=jax.ShapeDtypeStruct((m, n), out_dtype),
        scratch_shapes=[pltpu.VMEM((tm, k), BF16)],
        compiler_params=_params(("parallel", "arbitrary")),
        name="rms_matmul",
    )(*parts, g.reshape(1, k), w)


def _matmul_add_body(*refs, counts, n_p, n_next):
    n_out = 1 + (n_next > 0)
    n_terms = len(counts) - 1
    n_in = len(refs) - n_out
    w_refs = refs[n_in - n_next - n_terms:n_in - n_next]
    at = counts[0]
    acc = _row_value(refs[:at], n_p)
    for t in range(n_terms):
        a = _row_value(refs[at:at + counts[1 + t]], n_p)
        at += counts[1 + t]
        acc = acc + jnp.dot(a, w_refs[t][...], preferred_element_type=F32)
    refs[n_in][...] = acc
    if n_next:
        g_ref, w2_ref = refs[n_in - 2:n_in]
        refs[n_in + 1][...] = jnp.dot(_rms(acc, g_ref[...]).astype(BF16), w2_ref[...],
                                      preferred_element_type=F32)


def matmul_add(x, a_list, w, tm, tn, then=None):
    x_parts = _parts(x)
    a_parts = [_parts(a) for a in a_list]
    m, n = _n_rows(x_parts), x_parts[0].shape[1]
    in_specs = _row_specs(x_parts, tm, tn, lambda j: j)
    operands = list(x_parts)
    for parts in a_parts:
        in_specs += _row_specs(parts, tm, parts[0].shape[1], lambda j: 0)
        operands += list(parts)
    row = 0
    for parts in a_parts:
        ka = parts[0].shape[1]
        in_specs.append(pl.BlockSpec((ka, tn), lambda i, j, blk=row // ka: (blk, j)))
        operands.append(w)
        row += ka
    out_specs = [pl.BlockSpec((tm, tn), lambda i, j: (i, j))]
    out_shape = [jax.ShapeDtypeStruct((m, n), F32)]
    if then is not None:
        assert tn == n
        gain, w2 = then
        n2 = w2.shape[1]
        in_specs += [pl.BlockSpec((1, n), lambda i, j: (0, 0)),
                     pl.BlockSpec((n, n2), lambda i, j: (0, 0))]
        operands += [gain.reshape(1, n), w2]
        out_specs.append(pl.BlockSpec((tm, n2), lambda i, j: (i, 0)))
        out_shape.append(jax.ShapeDtypeStruct((m, n2), F32))
    counts = (len(x_parts),) + tuple(len(p) for p in a_parts)
    out = pl.pallas_call(
        functools.partial(_matmul_add_body, counts=counts, n_p=T_P // tm,
                          n_next=0 if then is None else 2),
        grid=(m // tm, n // tn),
        in_specs=in_specs,
        out_specs=out_specs,
        out_shape=out_shape,
        input_output_aliases={0: 0} if len(x_parts) == 1 else {},
        compiler_params=_params(("parallel", "parallel")),
        name="matmul_add",
    )(*operands)
    return out[0] if then is None else out


def _half_mask(shape, half):
    lane = lax.broadcasted_iota(jnp.int32, shape, len(shape) - 1)
    return (lane >= HALF) if half else (lane < HALF)


def _sink_softmax_pv(s, sink, v_tile, batched):
    m = jnp.maximum(jnp.max(s, axis=-1, keepdims=True), sink)
    p = jnp.exp(s - m)
    den = jnp.sum(p, axis=-1, keepdims=True) + jnp.exp(sink - m)
    if batched:
        o = jnp.einsum("bqk,bkd->bqd", p.astype(BF16), v_tile, preferred_element_type=F32)
    else:
        o = jnp.dot(p.astype(BF16), v_tile, preferred_element_type=F32)
    return o / den


def _swa_prompt_body(sink_ref, q_ref, kp_ref, kc_ref, vp_ref, vc_ref, ga_ref, o_ref):
    i = pl.program_id(0)
    blk = WINDOW
    q = q_ref[...] * (HEAD_DIM ** -0.5)
    k = jnp.concatenate([kp_ref[...], kc_ref[...]], axis=0).astype(BF16)
    v = jnp.concatenate([vp_ref[...], vc_ref[...]], axis=0).astype(BF16)
    row = lax.broadcasted_iota(jnp.int32, (blk, 2 * blk), 0)
    col = lax.broadcasted_iota(jnp.int32, (blk, 2 * blk), 1)
    dist = blk + row - col
    valid = (dist >= 0) & (dist <= WINDOW) & ((col >= blk) | (i > 0))
    distf = dist.astype(F32)
    tiles = [None] * (N_HEADS // 2)
    for n in range(N_KV):
        half = n % 2
        ts = slice((n // 2) * LANES, (n // 2 + 1) * LANES)
        k_m = jnp.where(_half_mask((2 * blk, LANES), half), k[:, ts], jnp.zeros((), BF16))
        v_t = v[:, ts]
        q_al = jnp.concatenate([q[:, (2 * n) * LANES:(2 * n + 1) * LANES],
                                q[:, (2 * n + 1) * LANES:(2 * n + 2) * LANES]], axis=0)
        q_mis = pltpu.roll(q_al, HALF, 1)
        lhs = jnp.concatenate([q_al, q_mis], axis=0).astype(BF16)
        s = lax.dot_general(lhs, k_m, (((1,), (1,)), ((), ())), preferred_element_type=F32)
        outs = []
        for b4 in range(4):
            a, j = divmod(b4, 2)
            h = 4 * n + 2 * j + (half if a == 0 else 1 - half)
            sb = s[b4 * blk:(b4 + 1) * blk]
            sb = jnp.where(valid, sb - SLOPES[h] * distf, NEG)
            outs.append(_sink_softmax_pv(sb, sink_ref[h], v_t, False))
        keep = _half_mask((blk, LANES), half)
        for j in range(2):
            tiles[2 * n + j] = jnp.where(keep, outs[j], pltpu.roll(outs[2 + j], HALF, 1))
    att = jnp.concatenate(tiles, axis=1)
    o_ref[...] = _rms(att, ga_ref[...]).astype(o_ref.dtype)


def swa_prompt(u, sinks, g_att):
    blk = WINDOW
    nb = T_P // blk
    kcol = (ATT_W + 2 * CONV_C) // KV_W
    prev = lambda i: jnp.maximum(i - 1, 0)
    return pl.pallas_call(
        _swa_prompt_body,
        grid=(nb,),
        in_specs=[pl.BlockSpec(memory_space=pltpu.SMEM),
                  pl.BlockSpec((blk, ATT_W), lambda i: (i, 0)),
                  pl.BlockSpec((blk, KV_W), lambda i: (prev(i), kcol)),
                  pl.BlockSpec((blk, KV_W), lambda i: (i, kcol)),
                  pl.BlockSpec((blk, KV_W), lambda i: (prev(i), kcol + 1)),
                  pl.BlockSpec((blk, KV_W), lambda i: (i, kcol + 1)),
                  pl.BlockSpec((1, ATT_W), lambda i: (0, 0))],
        out_specs=pl.BlockSpec((blk, ATT_W), lambda i: (i, 0)),
        out_shape=jax.ShapeDtypeStruct((T_P, ATT_W), BF16),
        compiler_params=_params(("parallel",)),
        name="swa_prompt",
    )(sinks, u, u, u, u, u, g_att.reshape(1, ATT_W))


def _layer_view(out_ref, layer):
    if len(out_ref.shape) == 3:
        return out_ref
    for other in range(out_ref.shape[0]):
        if other != layer:
            out_ref[other] = jnp.zeros(out_ref.shape[1:], out_ref.dtype)
    return out_ref.at[layer]


def _swa_sample_body(sink_ref, q_ref, kn_ref, vn_ref, kc_ref, vc_ref, bias_ref, ga_ref, *rest,
                     layer):
    o_ref, ko_ref, vo_ref, kall_ref, vall_ref = rest[-5:]
    bt = q_ref.shape[0]
    wb = kc_ref.shape[2]
    for new_ref, cache_ref, out_ref, all_ref in ((kn_ref, kc_ref, ko_ref, kall_ref),
                                                 (vn_ref, vc_ref, vo_ref, vall_ref)):
        out_ref = _layer_view(out_ref, layer)
        all_ref[:, wb:wb + DEC_SEQ, :] = new_ref[...]
        all_ref[:, wb + DEC_SEQ:, :] = jnp.zeros((bt, KEYS_PAD - wb - DEC_SEQ, KV_W), F32)
        for b in range(bt):
            all_ref[b, 0:wb, :] = cache_ref[b].T
        for b in range(bt):
            out_ref[b] = all_ref[b, DEC_SEQ:DEC_SEQ + wb, :].T

    q = q_ref[...] * (HEAD_DIM ** -0.5)
    k = kall_ref[...].astype(BF16)
    v = vall_ref[...].astype(BF16)
    tiles = [None] * (N_HEADS // 2)
    for n in range(N_KV):
        half = n % 2
        ts = slice((n // 2) * LANES, (n // 2 + 1) * LANES)
        k_m = jnp.where(_half_mask((bt, KEYS_PAD, LANES), half), k[:, :, ts], jnp.zeros((), BF16))
        v_t = v[:, :, ts]
        q_al = jnp.concatenate([q[:, :, (2 * n) * LANES:(2 * n + 1) * LANES],
                                q[:, :, (2 * n + 1) * LANES:(2 * n + 2) * LANES]], axis=1)
        q_mis = pltpu.roll(q_al, HALF, 2)
        lhs = jnp.concatenate([q_al, q_mis], axis=1).astype(BF16)
        s = jnp.einsum("bqd,bkd->bqk", lhs, k_m, preferred_element_type=F32)
        bias = bias_ref[n]
        s = jnp.where(bias > 0.5 * NEG, s + bias, NEG)
        sink = sink_ref[n]
        o = _sink_softmax_pv(s, sink, v_t, True)
        keep = _half_mask((bt, DEC_SEQ, LANES), half)
        for j in range(2):
            o_al = o[:, j * DEC_SEQ:(j + 1) * DEC_SEQ]
            o_mis = pltpu.roll(o[:, (2 + j) * DEC_SEQ:(3 + j) * DEC_SEQ], HALF, 2)
            tiles[2 * n + j] = jnp.where(keep, o_al, o_mis)
    att = jnp.concatenate(tiles, axis=2)
    o_ref[...] = _rms(att, ga_ref[...]).astype(o_ref.dtype)


def _sample_bias_and_sinks(sinks):
    wb = WINDOW
    bias = np.full((N_KV, 4 * DEC_SEQ, KEYS_PAD), NEG, np.float32)
    head = np.zeros((N_KV, 4 * DEC_SEQ), np.int32)
    for n in range(N_KV):
        half = n % 2
        for a in range(2):
            for j in range(2):
                h = 4 * n + 2 * j + (half if a == 0 else 1 - half)
                for t in range(DEC_SEQ):
                    r = (2 * a + j) * DEC_SEQ + t
                    head[n, r] = h
                    for kk in range(wb + DEC_SEQ):
                        d = t + wb - kk
                        if 0 <= d <= WINDOW:
                            bias[n, r, kk] = -SLOPES[h] * d
    sink_rows = sinks[jnp.asarray(head)][..., None]
    return jnp.asarray(bias), sink_rows


def swa_sample(q_s, k_new, v_new, cache_kt, cache_vt, layer, prev_out, sinks, g_att, bt):
    wb = cache_kt.shape[3]
    bias, sink_rows = _sample_bias_and_sinks(sinks)
    nq = 4 * DEC_SEQ
    b3 = lambda i: (i, 0, 0)
    z3 = lambda i: (0, 0, 0)
    cache_spec = pl.BlockSpec((None, bt, KV_W, wb), lambda i: (layer, i, 0, 0))
    in_specs = [pl.BlockSpec((N_KV, nq, 1), z3),
                pl.BlockSpec((bt, DEC_SEQ, ATT_W), b3),
                pl.BlockSpec((bt, DEC_SEQ, KV_W), b3),
                pl.BlockSpec((bt, DEC_SEQ, KV_W), b3),
                cache_spec, cache_spec,
                pl.BlockSpec((N_KV, nq, KEYS_PAD), z3),
                pl.BlockSpec((1, 1, ATT_W), z3)]
    operands = [sink_rows, q_s, k_new, v_new, cache_kt, cache_vt, bias, g_att.reshape(1, 1, ATT_W)]
    aliases = {}
    out_cache_spec = cache_spec
    if prev_out is None:
        out_cache_spec = pl.BlockSpec((DEPTH, bt, KV_W, wb), lambda i: (0, i, 0, 0))
    else:
        in_specs += [pl.BlockSpec(memory_space=pl.ANY)] * 2
        operands += list(prev_out)
        aliases = {len(operands) - 2: 1, len(operands) - 1: 2}
    return pl.pallas_call(
        functools.partial(_swa_sample_body, layer=layer),
        grid=(DEC_BATCH // bt,),
        in_specs=in_specs,
        out_specs=[pl.BlockSpec((bt, DEC_SEQ, ATT_W), b3), out_cache_spec, out_cache_spec],
        out_shape=[jax.ShapeDtypeStruct((DEC_BATCH, DEC_SEQ, ATT_W), BF16),
                   jax.ShapeDtypeStruct(cache_kt.shape, F32),
                   jax.ShapeDtypeStruct(cache_vt.shape, F32)],
        input_output_aliases=aliases,
        scratch_shapes=[pltpu.VMEM((bt, KEYS_PAD, KV_W), F32),
                        pltpu.VMEM((bt, KEYS_PAD, KV_W), F32)],
        compiler_params=_params(("parallel",)),
        name="swa_sample",
    )(*operands)


def _conv_post(y, cb, lg, lb, gc):
    y = y + cb
    mu = jnp.mean(y, axis=-1, keepdims=True)
    yc = y - mu
    z = yc * lax.rsqrt(jnp.mean(yc * yc, axis=-1, keepdims=True) + EPS) * lg + lb
    c = z * _sigmoid(z)
    return _rms(c, gc)


def _conv_prompt_body(ap_ref, gp_ref, a_ref, g_ref, cw_ref, cb_ref, lg_ref, lb_ref, gc_ref,
                      o_ref, st_ref, ext_ref, y_ref, *, tt, tc):
    i = pl.program_id(0)
    glu_prev = ap_ref[...] * _sigmoid(gp_ref[...])
    ext_ref[0:STATE_PAD, :] = jnp.where(i > 0, glu_prev, 0.0)
    glu = a_ref[...] * _sigmoid(g_ref[...])
    ext_ref[STATE_PAD:STATE_PAD + tt, :] = glu
    st_ref[...] = glu[tt - STATE_PAD:]
    first = STATE_PAD - STATE_ROWS
    for cblk in range(CONV_C // LANES):
        cs = slice(cblk * LANES, (cblk + 1) * LANES)
        for t0 in range(0, tt, tc):
            total = None
            for lo in range(SUBLANES):
                n_hi = (CONV_W - lo + SUBLANES - 1) // SUBLANES
                start = t0 + first + lo
                win = ext_ref[start:start + tc + SUBLANES * (n_hi - 1), cs]
                acc = None
                for hi in range(n_hi):
                    j = SUBLANES * hi + lo
                    term = win[SUBLANES * hi:SUBLANES * hi + tc] * cw_ref[j:j + 1, cs]
                    acc = term if acc is None else acc + term
                total = acc if total is None else total + acc
            y_ref[t0:t0 + tc, cs] = total
    o_ref[...] = _conv_post(y_ref[...], cb_ref[...], lg_ref[...], lb_ref[...],
                            gc_ref[...]).astype(o_ref.dtype)


def conv_prompt(u, conv_w, conv_b, ln_g, ln_b, g_conv, tt=128, tc=64):
    nt = T_P // tt
    per = tt // STATE_PAD
    prev = lambda i: jnp.maximum(i * per - 1, 0)
    acol, gcol = ATT_W // CONV_C, ATT_W // CONV_C + 1
    vec = lambda: pl.BlockSpec((1, CONV_C), lambda i: (0, 0))
    out, state = pl.pallas_call(
        functools.partial(_conv_prompt_body, tt=tt, tc=tc),
        grid=(nt,),
        in_specs=[pl.BlockSpec((STATE_PAD, CONV_C), lambda i: (prev(i), acol)),
                  pl.BlockSpec((STATE_PAD, CONV_C), lambda i: (prev(i), gcol)),
                  pl.BlockSpec((tt, CONV_C), lambda i: (i, acol)),
                  pl.BlockSpec((tt, CONV_C), lambda i: (i, gcol)),
                  pl.BlockSpec((CONV_W, CONV_C), lambda i: (0, 0)),
                  vec(), vec(), vec(), vec()],
        out_specs=[pl.BlockSpec((tt, CONV_C), lambda i: (i, 0)),
                   pl.BlockSpec((STATE_PAD, CONV_C), lambda i: (0, 0))],
        out_shape=[jax.ShapeDtypeStruct((T_P, CONV_C), BF16),
                   jax.ShapeDtypeStruct((STATE_PAD, CONV_C), F32)],
        scratch_shapes=[pltpu.VMEM((STATE_PAD + tt, CONV_C), F32),
                        pltpu.VMEM((tt, CONV_C), F32)],
        compiler_params=_params(("arbitrary",)),
        name="conv_prompt",
    )(u, u, u, u, conv_w, conv_b.reshape(1, -1), ln_g.reshape(1, -1), ln_b.reshape(1, -1),
      g_conv.reshape(1, -1))
    return out, state[STATE_PAD - STATE_ROWS:]


def _conv_sample_body(st_ref, *rest, layer):
    ag_refs = rest[:2 * DEC_SEQ]
    cw_ref, cb_ref, lg_ref, lb_ref, gc_ref = rest[2 * DEC_SEQ:2 * DEC_SEQ + 5]
    o_ref, so_ref = rest[-2:]
    so_ref = _layer_view(so_ref, layer)
    glu =[ag_refs[t][...] * _sigmoid(ag_refs[DEC_SEQ + t][...]) for t in range(DEC_SEQ)]

    def slab(r):
        return st_ref[r] if r < STATE_ROWS else glu[r - STATE_ROWS]

    for r in range(STATE_ROWS):
        so_ref[r] = slab(r + DEC_SEQ)
    for t in range(DEC_SEQ):
        y = None
        for j in range(CONV_W):
            term = slab(t + j) * cw_ref[j:j + 1, :]
            y = term if y is None else y + term
        o_ref[t] = _conv_post(y, cb_ref[...], lg_ref[...], lb_ref[...],
                              gc_ref[...]).astype(o_ref.dtype)


def conv_sample(state_rows, u, layer, prev_out, conv_w, conv_b, ln_g, ln_b, g_conv, bt):
    acol, gcol = ATT_W // CONV_C, ATT_W // CONV_C + 1
    vec = lambda: pl.BlockSpec((1, CONV_C), lambda i: (0, 0))
    st_spec = pl.BlockSpec((None, STATE_ROWS, bt, CONV_C), lambda i: (layer, 0, i, 0))
    step_spec = lambda t, col: pl.BlockSpec(
        (bt, CONV_C), lambda i: ((T_P + t * DEC_BATCH) // bt + i, col))
    in_specs = ([st_spec] + [step_spec(t, acol) for t in range(DEC_SEQ)]
                + [step_spec(t, gcol) for t in range(DEC_SEQ)]
                + [pl.BlockSpec((CONV_W, CONV_C), lambda i: (0, 0)), vec(), vec(), vec(), vec()])
    operands = [state_rows] + [u] * (2 * DEC_SEQ) + [
        conv_w, conv_b.reshape(1, -1), ln_g.reshape(1, -1), ln_b.reshape(1, -1),
        g_conv.reshape(1, -1)]
    aliases = {}
    out_st_spec = st_spec
    if prev_out is None:
        out_st_spec = pl.BlockSpec((DEPTH, STATE_ROWS, bt, CONV_C), lambda i: (0, 0, i, 0))
    else:
        in_specs.append(pl.BlockSpec(memory_space=pl.ANY))
        operands.append(prev_out)
        aliases = {len(operands) - 1: 1}
    return pl.pallas_call(
        functools.partial(_conv_sample_body, layer=layer),
        grid=(DEC_BATCH // bt,),
        in_specs=in_specs,
        out_specs=[pl.BlockSpec((DEC_SEQ, bt, CONV_C), lambda i: (0, i, 0)), out_st_spec],
        out_shape=[jax.ShapeDtypeStruct((DEC_SEQ, DEC_BATCH, CONV_C), BF16),
                   jax.ShapeDtypeStruct(state_rows.shape, F32)],
        input_output_aliases=aliases,
        compiler_params=_params(("parallel",)),
        name="conv_sample",
    )(*operands)


def _cross_prompt_body(q_ref, k_ref, v_ref, o_ref):
    q = q_ref[...].astype(BF16)
    k = k_ref[...].astype(BF16)
    v = v_ref[...].astype(BF16)
    outs = []
    for h in range(X_HEADS):
        hs = slice(h * X_HEAD_DIM, (h + 1) * X_HEAD_DIM)
        s = lax.dot_general(q[:, hs], k[:, hs], (((1,), (1,)), ((), ())),
                            preferred_element_type=F32) * (X_HEAD_DIM ** -0.5)
        p = jnp.exp(s - jnp.max(s, axis=-1, keepdims=True))
        den = jnp.sum(p, axis=-1, keepdims=True)
        outs.append(jnp.dot(p.astype(BF16), v[:, hs], preferred_element_type=F32) / den)
    o_ref[...] = jnp.concatenate(outs, axis=1).astype(o_ref.dtype)


def cross_prompt(q, mem_k, mem_v, tq=512):
    return pl.pallas_call(
        _cross_prompt_body,
        grid=(T_P // tq,),
        in_specs=[pl.BlockSpec((tq, X_W), lambda i: (i, 0)),
                  pl.BlockSpec((N_MEM, X_W), lambda i: (0, 0)),
                  pl.BlockSpec((N_MEM, X_W), lambda i: (0, 0))],
        out_specs=pl.BlockSpec((tq, X_W), lambda i: (i, 0)),
        out_shape=jax.ShapeDtypeStruct((T_P, X_W), BF16),
        compiler_params=_params(("parallel",)),
        name="cross_prompt",
    )(q, mem_k, mem_v)


def _cross_sample_body(q_ref, k_ref, v_ref, o_ref):
    q = q_ref[...].astype(BF16)
    k = k_ref[...].astype(BF16)
    v = v_ref[...].astype(BF16)
    s = jnp.einsum("bqd,bkd->bqk", q, k, preferred_element_type=F32) * (X_HEAD_DIM ** -0.5)
    row = lax.broadcasted_iota(jnp.int32, s.shape[1:], 0)
    col = lax.broadcasted_iota(jnp.int32, s.shape[1:], 1)
    same_head = (col % X_HEADS) == (row // DEC_SEQ)
    s = jnp.where(same_head, s, NEG)
    p = jnp.exp(s - jnp.max(s, axis=-1, keepdims=True))
    den = jnp.sum(p, axis=-1, keepdims=True)
    o = jnp.einsum("bqk,bkd->bqd", p.astype(BF16), v, preferred_element_type=F32) / den
    o_ref[...] = o.astype(o_ref.dtype)


def cross_sample(q_ht, mem_k, mem_v, layer, bt=8):
    rows = X_HEADS * DEC_SEQ
    b3 = lambda i: (i, 0, 0)
    b4 = lambda i: (layer, i, 0, 0)
    return pl.pallas_call(
        _cross_sample_body,
        grid=(DEC_BATCH // bt,),
        in_specs=[pl.BlockSpec((bt, rows, X_HEAD_DIM), b3),
                  pl.BlockSpec((None, bt, N_MEM * X_HEADS, X_HEAD_DIM), b4),
                  pl.BlockSpec((None, bt, N_MEM * X_HEADS, X_HEAD_DIM), b4)],
        out_specs=pl.BlockSpec((bt, rows, X_HEAD_DIM), b3),
        out_shape=jax.ShapeDtypeStruct((DEC_BATCH, rows, X_HEAD_DIM), BF16),
        compiler_params=_params(("parallel",)),
        name="cross_sample",
    )(q_ht, mem_k, mem_v)


def _swiglu_step(x, wg_ref, wu_ref, wd_ref):
    hg = jnp.dot(x, wg_ref[...].astype(BF16), preferred_element_type=F32)
    hu = jnp.dot(x, wu_ref[...].astype(BF16), preferred_element_type=F32)
    h = (hg * _sigmoid(hg) * hu).astype(BF16)
    return jnp.dot(h, wd_ref[...].astype(BF16), preferred_element_type=F32)


def _dense_ffn_body(x_ref, a_ref, wa_ref, g_ref, wg_ref, wu_ref, wd_ref, o_ref, xn_ref):
    @pl.when(pl.program_id(1) == 0)
    def _():
        x = x_ref[...] + jnp.dot(a_ref[...], wa_ref[...], preferred_element_type=F32)
        xn_ref[...] = _rms(x, g_ref[...]).astype(BF16)
        o_ref[...] = x

    o_ref[...] += _swiglu_step(xn_ref[...], wg_ref, wu_ref, wd_ref)


def dense_ffn(x, a, wa, g, wg, wu, wd, tm, tf):
    m, d = x.shape
    ka = a.shape[1]
    dff = wg.shape[1]
    return pl.pallas_call(
        _dense_ffn_body,
        grid=(m // tm, dff // tf),
        in_specs=[pl.BlockSpec((tm, d), lambda i, f: (i, 0), pipeline_mode=pl.Buffered(1)),
                  pl.BlockSpec((tm, ka), lambda i, f: (i, 0)),
                  pl.BlockSpec((ka, d), lambda i, f: (0, 0)),
                  pl.BlockSpec((1, d), lambda i, f: (0, 0)),
                  pl.BlockSpec((d, tf), lambda i, f: (0, f)),
                  pl.BlockSpec((d, tf), lambda i, f: (0, f)),
                  pl.BlockSpec((tf, d), lambda i, f: (f, 0))],
        out_specs=pl.BlockSpec((tm, d), lambda i, f: (i, 0)),
        out_shape=jax.ShapeDtypeStruct((m, d), F32),
        scratch_shapes=[pltpu.VMEM((tm, d), BF16)],
        compiler_params=_params(("parallel", "arbitrary")),
        name="dense_ffn",
    )(x, a, wa, g.reshape(1, d), wg, wu, wd)


def _router_body(x_ref, a_ref, wa_ref, g_ref, wr_ref, x1_ref, idx_ref, gate_ref, cnt_ref,
                 carry_ref):
    @pl.when(pl.program_id(0) == 0)
    def _():
        carry_ref[...] = jnp.zeros(carry_ref.shape, F32)

    x1 = x_ref[...] + jnp.dot(a_ref[...], wa_ref[...], preferred_element_type=F32)
    x1_ref[...] = x1
    xn = _rms(x1, g_ref[...])
    xh, xl = _split_bf16(xn)
    wh, wl = _split_bf16(wr_ref[...])
    logits = (jnp.dot(xh, wh, preferred_element_type=F32)
              + (jnp.dot(xl, wh, preferred_element_type=F32)
                 + jnp.dot(xh, wl, preferred_element_type=F32)))
    tm = logits.shape[0]
    lane = lax.broadcasted_iota(jnp.int32, logits.shape, 1)
    lg = jnp.where(lane < N_EXPERTS, logits, -jnp.inf)
    m1 = jnp.max(lg, axis=-1, keepdims=True)
    i1 = jnp.min(jnp.where(lg == m1, lane, LANES), axis=-1, keepdims=True)
    lg2 = jnp.where(lane == i1, -jnp.inf, lg)
    m2 = jnp.max(lg2, axis=-1, keepdims=True)
    i2 = jnp.min(jnp.where(lg2 == m2, lane, LANES), axis=-1, keepdims=True)
    e = jnp.exp(m2 - m1)
    den = 1.0 + e
    gate_ref[...] = jnp.where(lane == 0, 1.0 / den, jnp.where(lane == 1, e / den, 0.0))

    chosen = (lane == i1) | (lane == i2)
    r = lax.broadcasted_iota(jnp.int32, (tm, tm), 0)
    c = lax.broadcasted_iota(jnp.int32, (tm, tm), 1)
    earlier = jnp.where(c < r, 1.0, 0.0).astype(BF16)
    before = jnp.dot(earlier, jnp.where(chosen, 1.0, 0.0).astype(BF16),
                     preferred_element_type=F32) + carry_ref[...]
    r1 = jnp.sum(jnp.where(lane == i1, before, 0.0), axis=-1, keepdims=True).astype(jnp.int32)
    r2 = jnp.sum(jnp.where(lane == i2, before, 0.0), axis=-1, keepdims=True).astype(jnp.int32)
    idx_ref[...] = jnp.where(lane == 0, i1, jnp.where(lane == 1, i2,
                             jnp.where(lane == 2, r1, jnp.where(lane == 3, r2, 0))))
    carry_ref[...] += jnp.sum(jnp.where(chosen, 1.0, 0.0), axis=0, keepdims=True)
    cnt_ref[...] = carry_ref[...]


def router(x, a, wa, g, w_router, tm=512):
    m, d = x.shape
    ka = a.shape[1]
    wr = jnp.zeros((d, LANES), F32).at[:, :N_EXPERTS].set(w_router)
    return pl.pallas_call(
        _router_body,
        grid=(m // tm,),
        in_specs=[pl.BlockSpec((tm, d), lambda i: (i, 0)),
                  pl.BlockSpec((tm, ka), lambda i: (i, 0)),
                  pl.BlockSpec((ka, d), lambda i: (0, 0)),
                  pl.BlockSpec((1, d), lambda i: (0, 0)),
                  pl.BlockSpec((d, LANES), lambda i: (0, 0))],
        out_specs=[pl.BlockSpec((tm, d), lambda i: (i, 0)),
                   pl.BlockSpec((tm, LANES), lambda i: (i, 0)),
                   pl.BlockSpec((tm, LANES), lambda i: (i, 0)),
                   pl.BlockSpec((1, LANES), lambda i: (0, 0))],
        out_shape=[jax.ShapeDtypeStruct((m, d), F32),
                   jax.ShapeDtypeStruct((m, LANES), jnp.int32),
                   jax.ShapeDtypeStruct((m, LANES), F32),
                   jax.ShapeDtypeStruct((1, LANES), F32)],
        scratch_shapes=[pltpu.VMEM((1, LANES), F32)],
        compiler_params=_params(("arbitrary",)),
        name="router",
    )(x, a, wa, g.reshape(1, d), wr)


def _row_copy(src_hbm, row, dst_ref, r, sem):
    return pltpu.make_async_copy(src_hbm.at[pl.ds(row, 1)], dst_ref.at[pl.ds(r, 1)], sem)


def _gather_norm_body(src_ref, x_hbm, g_ref, o_ref, buf_ref, sem, *, rows, n_steps):
    i = pl.program_id(0)
    slot = lax.rem(i, 2)

    def start_batch(step, into):
        base = step * rows

        def start(k, c):
            r0 = pl.multiple_of(k * SUBLANES, SUBLANES)
            for s in range(SUBLANES):
                _row_copy(x_hbm, src_ref[base + r0 + s], buf_ref.at[into], r0 + s,
                          sem.at[into]).start()
            return c

        lax.fori_loop(0, rows // SUBLANES, start, 0)

    @pl.when(i == 0)
    def _():
        start_batch(0, 0)

    @pl.when(i + 1 < n_steps)
    def _():
        start_batch(i + 1, 1 - slot)

    pltpu.make_async_copy(x_hbm.at[pl.ds(0, rows)], buf_ref.at[slot], sem.at[slot]).wait()
    o_ref[...] = _rms(buf_ref[slot], g_ref[...]).astype(o_ref.dtype)


def gather_norm(x, g, src_rows, rows=1024):
    d = x.shape[1]
    n = src_rows.shape[0]
    grid_spec = pltpu.PrefetchScalarGridSpec(
        num_scalar_prefetch=1,
        grid=(n // rows,),
        in_specs=[pl.BlockSpec(memory_space=pl.ANY),
                  pl.BlockSpec((1, d), lambda i, s: (0, 0))],
        out_specs=pl.BlockSpec((rows, d), lambda i, s: (i, 0)),
        scratch_shapes=[pltpu.VMEM((2, rows, d), F32), pltpu.SemaphoreType.DMA((2,))],
    )
    return pl.pallas_call(
        functools.partial(_gather_norm_body, rows=rows, n_steps=n // rows),
        grid_spec=grid_spec,
        out_shape=jax.ShapeDtypeStruct((n, d), BF16),
        compiler_params=_params(("arbitrary",)),
        name="gather_norm",
    )(src_rows, x, g.reshape(1, d))


def _moe_ffn_body(ve_ref, vs_ref, vc_ref, vlive_ref, x_hbm, wg_ref, wu_ref, wd_ref, y_hbm,
                  xbuf_ref, acc_ref, wgb_ref, wub_ref, wdb_ref, sem_in, sem_out, *, sub, c_max, nf):
    v = pl.program_id(0)
    f = pl.program_id(1)
    count = vc_ref[v]
    first = vs_ref[v]
    live = vlive_ref[v] > 0

    def rows_of(j):
        return pl.ds(pl.multiple_of(j * sub, sub), sub)

    def copy_in(j):
        src = x_hbm.at[pl.ds(pl.multiple_of((first + j) * sub, sub), sub)]
        return pltpu.make_async_copy(src, xbuf_ref.at[rows_of(j)], sem_in.at[j])

    def copy_out(j):
        dst = y_hbm.at[pl.ds(pl.multiple_of((first + j) * sub, sub), sub)]
        return pltpu.make_async_copy(acc_ref.at[rows_of(j)], dst, sem_out)

    @pl.when(f == 0)
    def _():
        for j in range(c_max):
            @pl.when((j < count) & live)
            def _(j=j):
                copy_in(j).start()
        for j in range(c_max):
            @pl.when(j < count)
            def _(j=j):
                acc_ref[rows_of(j), :] = jnp.zeros((sub, acc_ref.shape[1]), F32)

    @pl.when(live)
    def _():
        wgb_ref[...] = wg_ref[...].astype(BF16)
        wub_ref[...] = wu_ref[...].astype(BF16)
        wdb_ref[...] = wd_ref[...].astype(BF16)

        def chain(j0, n):
            @pl.when(f == 0)
            def _():
                for k in range(n):
                    copy_in(j0 + k).wait()

            rows = pl.ds(pl.multiple_of(j0 * sub, sub), n * sub)
            acc_ref[rows, :] += _swiglu_step(xbuf_ref[rows, :], wgb_ref, wub_ref, wdb_ref)

            @pl.when(f == nf - 1)
            def _():
                for k in range(n):
                    copy_out(j0 + k).start()

        quads = count // 4
        rest = count - 4 * quads
        fold = (rest == 1) & (quads >= 1)
        n_full = jnp.where(fold, quads - 1, quads)

        def quad(i, carry):
            chain(i * 4, 4)
            return carry

        lax.fori_loop(0, n_full, quad, 0)

        @pl.when(fold)
        def _():
            chain(n_full * 4, 5)

        @pl.when(rest >= 2)
        def _():
            chain(quads * 4, 2)

        @pl.when((rest % 2 == 1) & jnp.logical_not(fold))
        def _():
            chain(count - 1, 1)

    @pl.when(f == nf - 1)
    def _():
        for j in range(c_max):
            @pl.when((j < count) & jnp.logical_not(live))
            def _(j=j):
                copy_out(j).start()
        for j in range(c_max):
            @pl.when(j < count)
            def _(j=j):
                copy_out(j).wait()


def moe_ffn(xs, vis_expert, vis_start, vis_count, vis_live, wg, wu, wd, sub, c_max, tf):
    n, d = xs.shape
    dff = wg.shape[2]
    nf = dff // tf
    n_vis = vis_expert.shape[0]
    fi = lambda v, f, live: jnp.where(live[v] > 0, f, nf - 1)
    grid_spec = pltpu.PrefetchScalarGridSpec(
        num_scalar_prefetch=4,
        grid=(n_vis, nf),
        in_specs=[pl.BlockSpec(memory_space=pl.ANY),
                  pl.BlockSpec((None, d, tf), lambda v, f, ve, vs, vc, lv: (ve[v], 0, fi(v, f, lv))),
                  pl.BlockSpec((None, d, tf), lambda v, f, ve, vs, vc, lv: (ve[v], 0, fi(v, f, lv))),
                  pl.BlockSpec((None, tf, d), lambda v, f, ve, vs, vc, lv: (ve[v], fi(v, f, lv), 0))],
        out_specs=pl.BlockSpec(memory_space=pl.ANY),
        scratch_shapes=[pltpu.VMEM((c_max * sub, d), BF16), pltpu.VMEM((c_max * sub, d), F32),
                        pltpu.VMEM((d, tf), BF16), pltpu.VMEM((d, tf), BF16),
                        pltpu.VMEM((tf, d), BF16),
                        pltpu.SemaphoreType.DMA((c_max,)), pltpu.SemaphoreType.DMA(())],
    )
    return pl.pallas_call(
        functools.partial(_moe_ffn_body, sub=sub, c_max=c_max, nf=nf),
        grid_spec=grid_spec,
        out_shape=jax.ShapeDtypeStruct((n, d), F32),
        compiler_params=_params(("arbitrary", "arbitrary")),
        name="moe_ffn",
    )(vis_expert, vis_start, vis_count, vis_live, xs, wg, wu, wd)


def _combine_rows(pos_ref, x_ref, gate_ref, y_hbm, buf_ref, sem, rows, n_tok):
    i = pl.program_id(0)
    phase = lax.rem(i, 2)

    def start_batch(step, into):
        base = step * rows

        def start(k, c):
            r0 = pl.multiple_of(k * SUBLANES, SUBLANES)
            for s in range(SUBLANES):
                for slot in range(2):
                    _row_copy(y_hbm, pos_ref[slot * n_tok + base + r0 + s],
                              buf_ref.at[into, slot], r0 + s, sem.at[into]).start()
            return c

        lax.fori_loop(0, rows // SUBLANES, start, 0)

    @pl.when(i == 0)
    def _():
        start_batch(0, 0)

    @pl.when(i + 1 < n_tok // rows)
    def _():
        start_batch(i + 1, 1 - phase)

    for slot in range(2):
        pltpu.make_async_copy(y_hbm.at[pl.ds(0, rows)], buf_ref.at[phase, slot],
                              sem.at[phase]).wait()
    gates = gate_ref[...]
    return x_ref[...] + (gates[:, 0:1] * buf_ref[phase, 0] + gates[:, 1:2] * buf_ref[phase, 1])


def _combine_body(pos_ref, x_ref, gate_ref, y_hbm, o_ref, buf_ref, sem, *, rows, n_tok):
    o_ref[...] = _combine_rows(pos_ref, x_ref, gate_ref, y_hbm, buf_ref, sem, rows, n_tok)


def _combine_norm_body(pos_ref, x_ref, gate_ref, g_ref, y_hbm, op_ref, os_ref, buf_ref, sem, *,
                       rows, n_tok, n_p):
    out = _rms(_combine_rows(pos_ref, x_ref, gate_ref, y_hbm, buf_ref, sem, rows, n_tok),
               g_ref[...])

    @pl.when(pl.program_id(0) < n_p)
    def _():
        op_ref[...] = out

    @pl.when(pl.program_id(0) >= n_p)
    def _():
        os_ref[...] = out


def moe_combine(x, gates, y, pos, out_norm=None, rows=512):
    n_tok, d = x.shape
    in_specs = [pl.BlockSpec((rows, d), lambda i, p: (i, 0)),
                pl.BlockSpec((rows, LANES), lambda i, p: (i, 0))]
    scratch = [pltpu.VMEM((2, 2, rows, d), F32), pltpu.SemaphoreType.DMA((2,))]
    any_spec = pl.BlockSpec(memory_space=pl.ANY)
    if out_norm is None:
        return pl.pallas_call(
            functools.partial(_combine_body, rows=rows, n_tok=n_tok),
            grid_spec=pltpu.PrefetchScalarGridSpec(
                num_scalar_prefetch=1, grid=(n_tok // rows,),
                in_specs=in_specs + [any_spec],
                out_specs=pl.BlockSpec((rows, d), lambda i, p: (i, 0)),
                scratch_shapes=scratch),
            out_shape=jax.ShapeDtypeStruct((n_tok, d), F32),
            compiler_params=_params(("arbitrary",)),
            name="moe_combine",
        )(pos, x, gates, y)
    n_p = T_P // rows
    return pl.pallas_call(
        functools.partial(_combine_norm_body, rows=rows, n_tok=n_tok, n_p=n_p),
        grid_spec=pltpu.PrefetchScalarGridSpec(
            num_scalar_prefetch=1, grid=(n_tok // rows,),
            in_specs=in_specs + [pl.BlockSpec((1, d), lambda i, p: (0, 0)), any_spec],
            out_specs=[pl.BlockSpec((rows, d), lambda i, p: (jnp.minimum(i, n_p - 1), 0)),
                       pl.BlockSpec((rows, d), lambda i, p: (jnp.maximum(i - n_p, 0), 0))],
            scratch_shapes=scratch),
        out_shape=[jax.ShapeDtypeStruct((T_P, d), F32), jax.ShapeDtypeStruct((n_tok - T_P, d), F32)],
        compiler_params=_params(("arbitrary",)),
        name="moe_combine_norm",
    )(pos, x, gates, out_norm.reshape(1, d), y)


def moe_layer(x, a, wa, g, w_router, wg, wu, wd, out_norm=None, sub=256, c_max=10, tf=256):
    assert c_max >= N_EXPERTS
    n_tok = x.shape[0]
    i32 = jnp.int32
    x1, idx_full, gate_full, cnt = router(x, a, wa, g, w_router)
    experts = idx_full[:, 0:2]
    ranks = idx_full[:, 2:4]
    counts = cnt[0, :N_EXPERTS].astype(i32)
    n_assign = 2 * n_tok
    n_sb = n_assign // sub + N_EXPERTS
    n_vis = (n_sb + (c_max - 1) * N_EXPERTS) // c_max + 1

    e_ids = jnp.arange(N_EXPERTS, dtype=i32)
    sb_per = (counts + sub - 1) // sub
    sb_end = jnp.cumsum(sb_per)
    sb_start = sb_end - sb_per
    sb_used = sb_end[-1]
    row_start = sb_start * sub

    start_of = jnp.sum(jnp.where(experts[:, :, None] == e_ids, row_start, 0), axis=-1)
    pos = (start_of + ranks).astype(i32).T.reshape(-1)

    n_pad = n_sb * sub - n_assign
    pad_end = jnp.cumsum(sb_per * sub - counts)
    pad_ids = jnp.arange(n_pad, dtype=i32)
    pad_e = jnp.sum((pad_ids[:, None] >= pad_end[None, :]).astype(i32), axis=1)
    keys = jnp.concatenate([experts.reshape(-1) * 2, pad_e * 2 + 1])
    toks = jnp.concatenate([jnp.arange(n_assign, dtype=i32) // 2, (pad_ids * 8) % n_tok])
    _, src_rows = lax.sort((keys, toks), num_keys=1, is_stable=True)

    vis_per = (sb_per + c_max - 1) // c_max
    vis_end = jnp.cumsum(vis_per)
    vis_used = vis_end[-1]
    v_ids = jnp.arange(n_vis, dtype=i32)
    live = v_ids < vis_used
    v_eff = jnp.minimum(v_ids, vis_used - 1)
    vis_expert = jnp.sum((v_eff[:, None] >= vis_end[None, :]).astype(i32), axis=1)
    of_expert = vis_expert[:, None] == e_ids[None, :]
    pick = lambda table: jnp.sum(jnp.where(of_expert, table[None, :], 0), axis=1)
    nth = v_eff - (pick(vis_end) - pick(vis_per))
    vis_start = pick(sb_start) + c_max * nth
    vis_count = jnp.minimum(c_max, pick(sb_per) - c_max * nth)
    tail = v_ids == vis_used
    vis_start = jnp.where(live, vis_start, jnp.where(tail, sb_used, 0)).astype(i32)
    vis_count = jnp.where(live, vis_count, jnp.where(tail, n_sb - sb_used, 0)).astype(i32)

    xs = gather_norm(x1, g, src_rows.astype(i32))
    y = moe_ffn(xs, vis_expert.astype(i32), vis_start, vis_count, live.astype(i32), wg, wu, wd,
                sub, c_max, tf)
    return moe_combine(x1, gate_full, y, pos, out_norm)


def _final_norm_body(x_ref, g_ref, o_ref):
    o_ref[...] = _rms(x_ref[...], g_ref[...])


def final_norm(x, g, first_block, n_blocks, tm=512):
    d = x.shape[1]
    return pl.pallas_call(
        _final_norm_body,
        grid=(n_blocks,),
        in_specs=[pl.BlockSpec((tm, d), lambda i: (first_block + i, 0)),
                  pl.BlockSpec((1, d), lambda i: (0, 0))],
        out_specs=pl.BlockSpec((tm, d), lambda i: (i, 0)),
        out_shape=jax.ShapeDtypeStruct((n_blocks * tm, d), F32),
        compiler_params=_params(("parallel",)),
        name="final_norm",
    )(x, g.reshape(1, d))


def kernel(x_prompt, x_sample, cache_win_k, cache_win_v, state_conv, cache_mem_k, cache_mem_v,
           mem_prompt, norm_mix, w_in, attn_sinks, att_out_norm, conv_w, conv_b, conv_ln_g,
           conv_ln_b, conv_out_norm, w_out, norm_cross, norm_mem, w_xq, w_xk, w_xv, w_xo,
           norm_ffn, w_gate, w_up, w_down, w_router, we_gate, we_up, we_down, final_norm_g):
    x = (x_prompt.reshape(T_P, D_MODEL), x_sample.transpose(1, 0, 2).reshape(T_S, D_MODEL))
    mem = mem_prompt.reshape(N_MEM, D_MODEL)
    o1 = ATT_W
    o2 = o1 + KV_W
    o3 = o2 + KV_W
    o4 = o3 + CONV_C
    wb = cache_win_k.shape[2]
    mem_k_rows = cache_mem_k.reshape(DEPTH, DEC_BATCH, N_MEM * X_HEADS, X_HEAD_DIM)
    mem_v_rows = cache_mem_v.reshape(DEPTH, DEC_BATCH, N_MEM * X_HEADS, X_HEAD_DIM)
    cache_kt = cache_win_k.transpose(0, 1, 3, 4, 2).reshape(DEPTH, DEC_BATCH, KV_W, wb)
    cache_vt = cache_win_v.transpose(0, 1, 3, 4, 2).reshape(DEPTH, DEC_BATCH, KV_W, wb)
    state_rows = state_conv.transpose(0, 2, 1, 3)
    win_out = state_out = None
    pk, pv, pc, pmk, pmv = [], [], [], [], []
    for l in range(DEPTH):
        wl = w_in[l]
        w_u = jnp.concatenate([wl[:, :o1], wl[:, o3:o4], wl[:, o4:], wl[:, o1:o2], wl[:, o2:o3]],
                              axis=1).astype(BF16)
        u = rms_matmul(x, norm_mix[l], w_u, tm=512, tn=U_W)
        ka, kg, kk, kv = ATT_W, ATT_W + CONV_C, ATT_W + 2 * CONV_C, ATT_W + 2 * CONV_C + KV_W

        att_p = swa_prompt(u, attn_sinks[l], att_out_norm[l])
        conv_p, state_p = conv_prompt(u, conv_w[l], conv_b[l], conv_ln_g[l], conv_ln_b[l],
                                      conv_out_norm[l])
        us = u[T_P:].reshape(DEC_SEQ, DEC_BATCH, U_W).transpose(1, 0, 2)
        att_s, win_kt, win_vt = swa_sample(us[:, :, :ATT_W], us[:, :, kk:kv], us[:, :, kv:],
                                           cache_kt, cache_vt, l, win_out,
                                           attn_sinks[l], att_out_norm[l], bt=16)
        win_out = (win_kt, win_vt)
        att_s = att_s.transpose(1, 0, 2).reshape(T_S, ATT_W)
        conv_s, state_out = conv_sample(state_rows, u, l, state_out, conv_w[l], conv_b[l],
                                        conv_ln_g[l], conv_ln_b[l], conv_out_norm[l], bt=16)
        conv_s = conv_s.reshape(T_S, CONV_C)
        x, qx = matmul_add(x, [(att_p, att_s), (conv_p, conv_s)],
                           w_out[l].astype(BF16), tm=512, tn=D_MODEL,
                           then=(norm_cross[l], w_xq[l].astype(BF16)))

        keep = min(WINDOW, SEQ)
        pk.append(u[T_P - keep:T_P, kk:kv].reshape(1, keep, N_KV, HEAD_DIM))
        pv.append(u[T_P - keep:T_P, kv:].reshape(1, keep, N_KV, HEAD_DIM))
        pc.append(state_p.reshape(1, STATE_ROWS, CONV_C))

        mk = rms_matmul(mem, norm_mem[l], w_xk[l].astype(BF16), tm=N_MEM, tn=X_W)
        mv = rms_matmul(mem, norm_mem[l], w_xv[l].astype(BF16), tm=N_MEM, tn=X_W)
        pmk.append(mk.reshape(1, N_MEM, X_HEADS, X_HEAD_DIM))
        pmv.append(mv.reshape(1, N_MEM, X_HEADS, X_HEAD_DIM))
        o_p = cross_prompt(qx, mk, mv)
        q_ht = qx[T_P:].reshape(DEC_SEQ, DEC_BATCH, X_HEADS, X_HEAD_DIM).transpose(1, 2, 0, 3)
        o_s = cross_sample(q_ht.reshape(DEC_BATCH, X_HEADS * DEC_SEQ, X_HEAD_DIM), mem_k_rows,
                           mem_v_rows, l)
        o_s = o_s.reshape(DEC_BATCH, X_HEADS, DEC_SEQ, X_HEAD_DIM).transpose(2, 0, 1, 3)
        o_all = jnp.concatenate([o_p, o_s.reshape(T_S, X_W)], axis=0)
        wxo = w_xo[l].astype(BF16)

        if l % 2 == 0:
            d = l // 2
            x = dense_ffn(x, o_all, wxo, norm_ffn[l], w_gate[d], w_up[d], w_down[d], tm=1088, tf=256)
        else:
            m = l // 2
            last = l == DEPTH - 1
            x = moe_layer(x, o_all, wxo, norm_ffn[l], w_router[m], we_gate[m], we_up[m], we_down[m],
                          out_norm=final_norm_g if last else None)

    if isinstance(x, tuple):
        y_prompt, y_sample = x
    else:
        y_prompt = final_norm(x, final_norm_g, 0, T_P // 512)
        y_sample = final_norm(x, final_norm_g, T_P // 512, T_S // 512)
    y_prompt = y_prompt.reshape(1, SEQ, D_MODEL)
    y_sample = y_sample.reshape(DEC_SEQ, DEC_BATCH, D_MODEL).transpose(1, 0, 2)
    win_k_s, win_v_s = (w.reshape(DEPTH, DEC_BATCH, N_KV, HEAD_DIM, wb).transpose(0, 1, 4, 2, 3)
                        for w in win_out)
    conv_s_state = state_out.transpose(0, 2, 1, 3)
    return (y_prompt, y_sample, jnp.stack(pk), jnp.stack(pv), jnp.stack(pc), jnp.stack(pmk),
            jnp.stack(pmv), win_k_s, win_v_s, conv_s_state)
```

```python
import functools

import jax
import jax.numpy as jnp
import numpy as np
from jax import lax
from jax.experimental import pallas as pl
from jax.experimental.pallas import tpu as pltpu

F32 = jnp.float32
BF16 = jnp.bfloat16

D_MODEL = 2048
SEQ = 8192
DEPTH = 2
DEC_BATCH = 128
DEC_SEQ = 4
HEAD_DIM = 64
ATT_W = 1024
N_HEADS = 16
N_KV = 4
KV_W = N_KV * HEAD_DIM
WINDOW = 128
CONV_C = 1024
CONV_W = 31
N_MEM = 256
X_HEADS = 4
X_HEAD_DIM = 128
X_W = X_HEADS * X_HEAD_DIM
D_FF = 5632
N_EXPERTS = 8
D_FF_E = 7168
EPS = 1e-6
NEG = -1e30

T_P = SEQ
T_S = DEC_BATCH * DEC_SEQ
T_ALL = T_P + T_S

LANES = 128
SUBLANES = 8
HALF = HEAD_DIM
STATE_ROWS = CONV_W - 1
STATE_PAD = 32
KEYS_PAD = 256
VMEM_LIMIT = 56 * 1024 * 1024

SLOPES = [float(2.0 ** (-8.0 * (h + 1) / N_HEADS)) for h in range(N_HEADS)]

U_W = ATT_W + 2 * CONV_C + 2 * KV_W


def _params(sem):
    return pltpu.CompilerParams(dimension_semantics=sem, vmem_limit_bytes=VMEM_LIMIT)


def _rms(x, g):
    r = lax.rsqrt(jnp.mean(x * x, axis=-1, keepdims=True) + EPS)
    return x * r * g


def _sigmoid(x):
    return 1.0 / (1.0 + jnp.exp(-x))


def _split_bf16(x):
    hi = x.astype(BF16)
    return hi, (x - hi.astype(F32)).astype(BF16)


def _parts(x):
    return x if isinstance(x, tuple) else (x,)


def _n_rows(parts):
    return sum(p.shape[0] for p in parts)


def _row_specs(parts, tm, width, col):
    if len(parts) == 1:
        return [pl.BlockSpec((tm, width), lambda i, j: (i, col(j)))]
    n_p = parts[0].shape[0] // tm
    return [pl.BlockSpec((tm, width), lambda i, j: (jnp.minimum(i, n_p - 1), col(j))),
            pl.BlockSpec((tm, width), lambda i, j: (jnp.maximum(i - n_p, 0), col(j)))]


def _row_value(refs, n_p):
    if len(refs) == 1:
        return refs[0][...]
    return jnp.where(pl.program_id(0) < n_p, refs[0][...], refs[1][...])


def _rms_matmul_body(*refs, n_x, n_p):
    x_refs = refs[:n_x]
    g_ref, w_ref, o_ref, xn_ref = refs[n_x:]

    @pl.when(pl.program_id(1) == 0)
    def _():
        xn_ref[...] = _rms(_row_value(x_refs, n_p), g_ref[...]).astype(BF16)

    o_ref[...] = jnp.dot(xn_ref[...], w_ref[...], preferred_element_type=F32).astype(o_ref.dtype)


def rms_matmul(x, g, w, tm, tn, out_dtype=F32):
    parts = _parts(x)
    m, k = _n_rows(parts), parts[0].shape[1]
    n = w.shape[1]
    return pl.pallas_call(
        functools.partial(_rms_matmul_body, n_x=len(parts), n_p=parts[0].shape[0] // tm),
        grid=(m // tm, n // tn),
        in_specs=_row_specs(parts, tm, k, lambda j: 0) + [
            pl.BlockSpec((1, k), lambda i, j: (0, 0)),
            pl.BlockSpec((k, tn), lambda i, j: (0, j))],
        out_specs=pl.BlockSpec((tm, tn), lambda i, j: (i, j)),
        out_shape=jax.ShapeDtypeStruct((m, n), out_dtype),
        scratch_shapes=[pltpu.VMEM((tm, k), BF16)],
        compiler_params=_params(("parallel", "arbitrary")),
        name="rms_matmul",
    )(*parts, g.reshape(1, k), w)


def _matmul_add_body(*refs, counts, n_p, n_next):
    n_out = 1 + (n_next > 0)
    n_terms = len(counts) - 1
    n_in = len(refs) - n_out
    w_refs = refs[n_in - n_next - n_terms:n_in - n_next]
    at = counts[0]
    acc = _row_value(refs[:at], n_p)
    for t in range(n_terms):
        a = _row_value(refs[at:at + counts[1 + t]], n_p)
        at += counts[1 + t]
        acc = acc + jnp.dot(a, w_refs[t][...], preferred_element_type=F32)
    refs[n_in][...] = acc
    if n_next:
        g_ref, w2_ref = refs[n_in - 2:n_in]
        refs[n_in + 1][...] = jnp.dot(_rms(acc, g_ref[...]).astype(BF16), w2_ref[...],
                                      preferred_element_type=F32)


def matmul_add(x, a_list, w, tm, tn, then=None):
    x_parts = _parts(x)
    a_parts = [_parts(a) for a in a_list]
    m, n = _n_rows(x_parts), x_parts[0].shape[1]
    in_specs = _row_specs(x_parts, tm, tn, lambda j: j)
    operands = list(x_parts)
    for parts in a_parts:
        in_specs += _row_specs(parts, tm, parts[0].shape[1], lambda j: 0)
        operands += list(parts)
    row = 0
    for parts in a_parts:
        ka = parts[0].shape[1]
        in_specs.append(pl.BlockSpec((ka, tn), lambda i, j, blk=row // ka: (blk, j)))
        operands.append(w)
        row += ka
    out_specs = [pl.BlockSpec((tm, tn), lambda i, j: (i, j))]
    out_shape = [jax.ShapeDtypeStruct((m, n), F32)]
    if then is not None:
        assert tn == n
        gain, w2 = then
        n2 = w2.shape[1]
        in_specs += [pl.BlockSpec((1, n), lambda i, j: (0, 0)),
                     pl.BlockSpec((n, n2), lambda i, j: (0, 0))]
        operands += [gain.reshape(1, n), w2]
        out_specs.append(pl.BlockSpec((tm, n2), lambda i, j: (i, 0)))
        out_shape.append(jax.ShapeDtypeStruct((m, n2), F32))
    counts = (len(x_parts),) + tuple(len(p) for p in a_parts)
    out = pl.pallas_call(
        functools.partial(_matmul_add_body, counts=counts, n_p=T_P // tm,
                          n_next=0 if then is None else 2),
        grid=(m // tm, n // tn),
        in_specs=in_specs,
        out_specs=out_specs,
        out_shape=out_shape,
        input_output_aliases={0: 0} if len(x_parts) == 1 else {},
        compiler_params=_params(("parallel", "parallel")),
        name="matmul_add",
    )(*operands)
    return out[0] if then is None else out


def _half_mask(shape, half):
    lane = lax.broadcasted_iota(jnp.int32, shape, len(shape) - 1)
    return (lane >= HALF) if half else (lane < HALF)


def _sink_softmax_pv(s, sink, v_tile, batched):
    m = jnp.maximum(jnp.max(s, axis=-1, keepdims=True), sink)
    p = jnp.exp(s - m)
    den = jnp.sum(p, axis=-1, keepdims=True) + jnp.exp(sink - m)
    if batched:
        o = jnp.einsum("bqk,bkd->bqd", p.astype(BF16), v_tile, preferred_element_type=F32)
    else:
        o = jnp.dot(p.astype(BF16), v_tile, preferred_element_type=F32)
    return o / den


def _swa_prompt_body(sink_ref, q_ref, kp_ref, kc_ref, vp_ref, vc_ref, ga_ref, o_ref):
    i = pl.program_id(0)
    blk = WINDOW
    q = q_ref[...] * (HEAD_DIM ** -0.5)
    k = jnp.concatenate([kp_ref[...], kc_ref[...]], axis=0).astype(BF16)
    v = jnp.concatenate([vp_ref[...], vc_ref[...]], axis=0).astype(BF16)
    row = lax.broadcasted_iota(jnp.int32, (blk, 2 * blk), 0)
    col = lax.broadcasted_iota(jnp.int32, (blk, 2 * blk), 1)
    dist = blk + row - col
    valid = (dist >= 0) & (dist <= WINDOW) & ((col >= blk) | (i > 0))
    distf = dist.astype(F32)
    tiles = [None] * (N_HEADS // 2)
    for n in range(N_KV):
        half = n % 2
        ts = slice((n // 2) * LANES, (n // 2 + 1) * LANES)
        k_m = jnp.where(_half_mask((2 * blk, LANES), half), k[:, ts], jnp.zeros((), BF16))
        v_t = v[:, ts]
        q_al = jnp.concatenate([q[:, (2 * n) * LANES:(2 * n + 1) * LANES],
                                q[:, (2 * n + 1) * LANES:(2 * n + 2) * LANES]], axis=0)
        q_mis = pltpu.roll(q_al, HALF, 1)
        lhs = jnp.concatenate([q_al, q_mis], axis=0).astype(BF16)
        s = lax.dot_general(lhs, k_m, (((1,), (1,)), ((), ())), preferred_element_type=F32)
        outs = []
        for b4 in range(4):
            a, j = divmod(b4, 2)
            h = 4 * n + 2 * j + (half if a == 0 else 1 - half)
            sb = s[b4 * blk:(b4 + 1) * blk]
            sb = jnp.where(valid, sb - SLOPES[h] * distf, NEG)
            outs.append(_sink_softmax_pv(sb, sink_ref[h], v_t, False))
        keep = _half_mask((blk, LANES), half)
        for j in range(2):
            tiles[2 * n + j] = jnp.where(keep, outs[j], pltpu.roll(outs[2 + j], HALF, 1))
    att = jnp.concatenate(tiles, axis=1)
    o_ref[...] = _rms(att, ga_ref[...]).astype(o_ref.dtype)


def swa_prompt(u, sinks, g_att):
    blk = WINDOW
    nb = T_P // blk
    kcol = (ATT_W + 2 * CONV_C) // KV_W
    prev = lambda i: jnp.maximum(i - 1, 0)
    return pl.pallas_call(
        _swa_prompt_body,
        grid=(nb,),
        in_specs=[pl.BlockSpec(memory_space=pltpu.SMEM),
                  pl.BlockSpec((blk, ATT_W), lambda i: (i, 0)),
                  pl.BlockSpec((blk, KV_W), lambda i: (prev(i), kcol)),
                  pl.BlockSpec((blk, KV_W), lambda i: (i, kcol)),
                  pl.BlockSpec((blk, KV_W), lambda i: (prev(i), kcol + 1)),
                  pl.BlockSpec((blk, KV_W), lambda i: (i, kcol + 1)),
                  pl.BlockSpec((1, ATT_W), lambda i: (0, 0))],
        out_specs=pl.BlockSpec((blk, ATT_W), lambda i: (i, 0)),
        out_shape=jax.ShapeDtypeStruct((T_P, ATT_W), BF16),
        compiler_params=_params(("parallel",)),
        name="swa_prompt",
    )(sinks, u, u, u, u, u, g_att.reshape(1, ATT_W))


def _layer_view(out_ref, layer):
    if len(out_ref.shape) == 3:
        return out_ref
    for other in range(out_ref.shape[0]):
        if other != layer:
            out_ref[other] = jnp.zeros(out_ref.shape[1:], out_ref.dtype)
    return out_ref.at[layer]


def _swa_sample_body(sink_ref, q_ref, kn_ref, vn_ref, kc_ref, vc_ref, bias_ref, ga_ref, *rest,
                     layer):
    o_ref, ko_ref, vo_ref, kall_ref, vall_ref = rest[-5:]
    bt = q_ref.shape[0]
    wb = kc_ref.shape[2]
    for new_ref, cache_ref, out_ref, all_ref in ((kn_ref, kc_ref, ko_ref, kall_ref),
                                                 (vn_ref, vc_ref, vo_ref, vall_ref)):
        out_ref = _layer_view(out_ref, layer)
        all_ref[:, wb:wb + DEC_SEQ, :] = new_ref[...]
        all_ref[:, wb + DEC_SEQ:, :] = jnp.zeros((bt, KEYS_PAD - wb - DEC_SEQ, KV_W), F32)
        for b in range(bt):
            all_ref[b, 0:wb, :] = cache_ref[b].T
        for b in range(bt):
            out_ref[b] = all_ref[b, DEC_SEQ:DEC_SEQ + wb, :].T

    q = q_ref[...] * (HEAD_DIM ** -0.5)
    k = kall_ref[...].astype(BF16)
    v = vall_ref[...].astype(BF16)
    tiles = [None] * (N_HEADS // 2)
    for n in range(N_KV):
        half = n % 2
        ts = slice((n // 2) * LANES, (n // 2 + 1) * LANES)
        k_m = jnp.where(_half_mask((bt, KEYS_PAD, LANES), half), k[:, :, ts], jnp.zeros((), BF16))
        v_t = v[:, :, ts]
        q_al = jnp.concatenate([q[:, :, (2 * n) * LANES:(2 * n + 1) * LANES],
                                q[:, :, (2 * n + 1) * LANES:(2 * n + 2) * LANES]], axis=1)
        q_mis = pltpu.roll(q_al, HALF, 2)
        lhs = jnp.concatenate([q_al, q_mis], axis=1).astype(BF16)
        s = jnp.einsum("bqd,bkd->bqk", lhs, k_m, preferred_element_type=F32)
        bias = bias_ref[n]
        s = jnp.where(bias > 0.5 * NEG, s + bias, NEG)
        sink = sink_ref[n]
        o = _sink_softmax_pv(s, sink, v_t, True)
        keep = _half_mask((bt, DEC_SEQ, LANES), half)
        for j in range(2):
            o_al = o[:, j * DEC_SEQ:(j + 1) * DEC_SEQ]
            o_mis = pltpu.roll(o[:, (2 + j) * DEC_SEQ:(3 + j) * DEC_SEQ], HALF, 2)
            tiles[2 * n + j] = jnp.where(keep, o_al, o_mis)
    att = jnp.concatenate(tiles, axis=2)
    o_ref[...] = _rms(att, ga_ref[...]).astype(o_ref.dtype)


def _sample_bias_and_sinks(sinks):
    wb = WINDOW
    bias = np.full((N_KV, 4 * DEC_SEQ, KEYS_PAD), NEG, np.float32)
    head = np.zeros((N_KV, 4 * DEC_SEQ), np.int32)
    for n in range(N_KV):
        half = n % 2
        for a in range(2):
            for j in range(2):
                h = 4 * n + 2 * j + (half if a == 0 else 1 - half)
                for t in range(DEC_SEQ):
                    r = (2 * a + j) * DEC_SEQ + t
                    head[n, r] = h
                    for kk in range(wb + DEC_SEQ):
                        d = t + wb - kk
                        if 0 <= d <= WINDOW:
                            bias[n, r, kk] = -SLOPES[h] * d
    sink_rows = sinks[jnp.asarray(head)][..., None]
    return jnp.asarray(bias), sink_rows


def swa_sample(q_s, k_new, v_new, cache_kt, cache_vt, layer, prev_out, sinks, g_att, bt):
    wb = cache_kt.shape[3]
    bias, sink_rows = _sample_bias_and_sinks(sinks)
    nq = 4 * DEC_SEQ
    b3 = lambda i: (i, 0, 0)
    z3 = lambda i: (0, 0, 0)
    cache_spec = pl.BlockSpec((None, bt, KV_W, wb), lambda i: (layer, i, 0, 0))
    in_specs = [pl.BlockSpec((N_KV, nq, 1), z3),
                pl.BlockSpec((bt, DEC_SEQ, ATT_W), b3),
                pl.BlockSpec((bt, DEC_SEQ, KV_W), b3),
                pl.BlockSpec((bt, DEC_SEQ, KV_W), b3),
                cache_spec, cache_spec,
                pl.BlockSpec((N_KV, nq, KEYS_PAD), z3),
                pl.BlockSpec((1, 1, ATT_W), z3)]
    operands = [sink_rows, q_s, k_new, v_new, cache_kt, cache_vt, bias, g_att.reshape(1, 1, ATT_W)]
    aliases = {}
    out_cache_spec = cache_spec
    if prev_out is None:
        out_cache_spec = pl.BlockSpec((DEPTH, bt, KV_W, wb), lambda i: (0, i, 0, 0))
    else:
        in_specs += [pl.BlockSpec(memory_space=pl.ANY)] * 2
        operands += list(prev_out)
        aliases = {len(operands) - 2: 1, len(operands) - 1: 2}
    return pl.pallas_call(
        functools.partial(_swa_sample_body, layer=layer),
        grid=(DEC_BATCH // bt,),
        in_specs=in_specs,
        out_specs=[pl.BlockSpec((bt, DEC_SEQ, ATT_W), b3), out_cache_spec, out_cache_spec],
        out_shape=[jax.ShapeDtypeStruct((DEC_BATCH, DEC_SEQ, ATT_W), BF16),
                   jax.ShapeDtypeStruct(cache_kt.shape, F32),
                   jax.ShapeDtypeStruct(cache_vt.shape, F32)],
        input_output_aliases=aliases,
        scratch_shapes=[pltpu.VMEM((bt, KEYS_PAD, KV_W), F32),
                        pltpu.VMEM((bt, KEYS_PAD, KV_W), F32)],
        compiler_params=_params(("parallel",)),
        name="swa_sample",
    )(*operands)


def _conv_post(y, cb, lg, lb, gc):
    y = y + cb
    mu = jnp.mean(y, axis=-1, keepdims=True)
    yc = y - mu
    z = yc * lax.rsqrt(jnp.mean(yc * yc, axis=-1, keepdims=True) + EPS) * lg + lb
    c = z * _sigmoid(z)
    return _rms(c, gc)


def _conv_prompt_body(ap_ref, gp_ref, a_ref, g_ref, cw_ref, cb_ref, lg_ref, lb_ref, gc_ref,
                      o_ref, st_ref, ext_ref, y_ref, *, tt, tc):
    i = pl.program_id(0)
    glu_prev = ap_ref[...] * _sigmoid(gp_ref[...])
    ext_ref[0:STATE_PAD, :] = jnp.where(i > 0, glu_prev, 0.0)
    glu = a_ref[...] * _sigmoid(g_ref[...])
    ext_ref[STATE_PAD:STATE_PAD + tt, :] = glu
    st_ref[...] = glu[tt - STATE_PAD:]
    first = STATE_PAD - STATE_ROWS
    for cblk in range(CONV_C // LANES):
        cs = slice(cblk * LANES, (cblk + 1) * LANES)
        for t0 in range(0, tt, tc):
            total = None
            for lo in range(SUBLANES):
                n_hi = (CONV_W - lo + SUBLANES - 1) // SUBLANES
                start = t0 + first + lo
                win = ext_ref[start:start + tc + SUBLANES * (n_hi - 1), cs]
                acc = None
                for hi in range(n_hi):
                    j = SUBLANES * hi + lo
                    term = win[SUBLANES * hi:SUBLANES * hi + tc] * cw_ref[j:j + 1, cs]
                    acc = term if acc is None else acc + term
                total = acc if total is None else total + acc
            y_ref[t0:t0 + tc, cs] = total
    o_ref[...] = _conv_post(y_ref[...], cb_ref[...], lg_ref[...], lb_ref[...],
                            gc_ref[...]).astype(o_ref.dtype)


def conv_prompt(u, conv_w, conv_b, ln_g, ln_b, g_conv, tt=256, tc=64):
    nt = T_P // tt
    per = tt // STATE_PAD
    prev = lambda i: jnp.maximum(i * per - 1, 0)
    acol, gcol = ATT_W // CONV_C, ATT_W // CONV_C + 1
    vec = lambda: pl.BlockSpec((1, CONV_C), lambda i: (0, 0))
    out, state = pl.pallas_call(
        functools.partial(_conv_prompt_body, tt=tt, tc=tc),
        grid=(nt,),
        in_specs=[pl.BlockSpec((STATE_PAD, CONV_C), lambda i: (prev(i), acol)),
                  pl.BlockSpec((STATE_PAD, CONV_C), lambda i: (prev(i), gcol)),
                  pl.BlockSpec((tt, CONV_C), lambda i: (i, acol)),
                  pl.BlockSpec((tt, CONV_C), lambda i: (i, gcol)),
                  pl.BlockSpec((CONV_W, CONV_C), lambda i: (0, 0)),
                  vec(), vec(), vec(), vec()],
        out_specs=[pl.BlockSpec((tt, CONV_C), lambda i: (i, 0)),
                   pl.BlockSpec((STATE_PAD, CONV_C), lambda i: (0, 0))],
        out_shape=[jax.ShapeDtypeStruct((T_P, CONV_C), BF16),
                   jax.ShapeDtypeStruct((STATE_PAD, CONV_C), F32)],
        scratch_shapes=[pltpu.VMEM((STATE_PAD + tt, CONV_C), F32),
                        pltpu.VMEM((tt, CONV_C), F32)],
        compiler_params=_params(("arbitrary",)),
        name="conv_prompt",
    )(u, u, u, u, conv_w, conv_b.reshape(1, -1), ln_g.reshape(1, -1), ln_b.reshape(1, -1),
      g_conv.reshape(1, -1))
    return out, state[STATE_PAD - STATE_ROWS:]


def _conv_sample_body(st_ref, *rest, layer):
    ag_refs = rest[:2 * DEC_SEQ]
    cw_ref, cb_ref, lg_ref, lb_ref, gc_ref = rest[2 * DEC_SEQ:2 * DEC_SEQ + 5]
    o_ref, so_ref = rest[-2:]
    so_ref = _layer_view(so_ref, layer)
    glu =[ag_refs[t][...] * _sigmoid(ag_refs[DEC_SEQ + t][...]) for t in range(DEC_SEQ)]

    def slab(r):
        return st_ref[r] if r < STATE_ROWS else glu[r - STATE_ROWS]

    for r in range(STATE_ROWS):
        so_ref[r] = slab(r + DEC_SEQ)
    for t in range(DEC_SEQ):
        y = None
        for j in range(CONV_W):
            term = slab(t + j) * cw_ref[j:j + 1, :]
            y = term if y is None else y + term
        o_ref[t] = _conv_post(y, cb_ref[...], lg_ref[...], lb_ref[...],
                              gc_ref[...]).astype(o_ref.dtype)


def conv_sample(state_rows, u, layer, prev_out, conv_w, conv_b, ln_g, ln_b, g_conv, bt):
    acol, gcol = ATT_W // CONV_C, ATT_W // CONV_C + 1
    vec = lambda: pl.BlockSpec((1, CONV_C), lambda i: (0, 0))
    st_spec = pl.BlockSpec((None, STATE_ROWS, bt, CONV_C), lambda i: (layer, 0, i, 0))
    step_spec = lambda t, col: pl.BlockSpec(
        (bt, CONV_C), lambda i: ((T_P + t * DEC_BATCH) // bt + i, col))
    in_specs = ([st_spec] + [step_spec(t, acol) for t in range(DEC_SEQ)]
                + [step_spec(t, gcol) for t in range(DEC_SEQ)]
                + [pl.BlockSpec((CONV_W, CONV_C), lambda i: (0, 0)), vec(), vec(), vec(), vec()])
    operands = [state_rows] + [u] * (2 * DEC_SEQ) + [
        conv_w, conv_b.reshape(1, -1), ln_g.reshape(1, -1), ln_b.reshape(1, -1),
        g_conv.reshape(1, -1)]
    aliases = {}
    out_st_spec = st_spec
    if prev_out is None:
        out_st_spec = pl.BlockSpec((DEPTH, STATE_ROWS, bt, CONV_C), lambda i: (0, 0, i, 0))
    else:
        in_specs.append(pl.BlockSpec(memory_space=pl.ANY))
        operands.append(prev_out)
        aliases = {len(operands) - 1: 1}
    return pl.pallas_call(
        functools.partial(_conv_sample_body, layer=layer),
        grid=(DEC_BATCH // bt,),
        in_specs=in_specs,
        out_specs=[pl.BlockSpec((DEC_SEQ, bt, CONV_C), lambda i: (0, i, 0)), out_st_spec],
        out_shape=[jax.ShapeDtypeStruct((DEC_SEQ, DEC_BATCH, CONV_C), BF16),
                   jax.ShapeDtypeStruct(state_rows.shape, F32)],
        input_output_aliases=aliases,
        compiler_params=_params(("parallel",)),
        name="conv_sample",
    )(*operands)


def _cross_prompt_body(q_ref, k_ref, v_ref, o_ref):
    q = q_ref[...].astype(BF16)
    k = k_ref[...].astype(BF16)
    v = v_ref[...].astype(BF16)
    outs = []
    for h in range(X_HEADS):
        hs = slice(h * X_HEAD_DIM, (h + 1) * X_HEAD_DIM)
        s = lax.dot_general(q[:, hs], k[:, hs], (((1,), (1,)), ((), ())),
                            preferred_element_type=F32) * (X_HEAD_DIM ** -0.5)
        p = jnp.exp(s - jnp.max(s, axis=-1, keepdims=True))
        den = jnp.sum(p, axis=-1, keepdims=True)
        outs.append(jnp.dot(p.astype(BF16), v[:, hs], preferred_element_type=F32) / den)
    o_ref[...] = jnp.concatenate(outs, axis=1).astype(o_ref.dtype)


def cross_prompt(q, mem_k, mem_v, tq=512):
    return pl.pallas_call(
        _cross_prompt_body,
        grid=(T_P // tq,),
        in_specs=[pl.BlockSpec((tq, X_W), lambda i: (i, 0)),
                  pl.BlockSpec((N_MEM, X_W), lambda i: (0, 0)),
                  pl.BlockSpec((N_MEM, X_W), lambda i: (0, 0))],
        out_specs=pl.BlockSpec((tq, X_W), lambda i: (i, 0)),
        out_shape=jax.ShapeDtypeStruct((T_P, X_W), BF16),
        compiler_params=_params(("parallel",)),
        name="cross_prompt",
    )(q, mem_k, mem_v)


def _cross_sample_body(q_ref, k_ref, v_ref, o_ref):
    q = q_ref[...].astype(BF16)
    k = k_ref[...].astype(BF16)
    v = v_ref[...].astype(BF16)
    s = jnp.einsum("bqd,bkd->bqk", q, k, preferred_element_type=F32) * (X_HEAD_DIM ** -0.5)
    row = lax.broadcasted_iota(jnp.int32, s.shape[1:], 0)
    col = lax.broadcasted_iota(jnp.int32, s.shape[1:], 1)
    same_head = (col % X_HEADS) == (row // DEC_SEQ)
    s = jnp.where(same_head, s, NEG)
    p = jnp.exp(s - jnp.max(s, axis=-1, keepdims=True))
    den = jnp.sum(p, axis=-1, keepdims=True)
    o = jnp.einsum("bqk,bkd->bqd", p.astype(BF16), v, preferred_element_type=F32) / den
    o_ref[...] = o.astype(o_ref.dtype)


def cross_sample(q_ht, mem_k, mem_v, layer, bt=16):
    rows = X_HEADS * DEC_SEQ
    b3 = lambda i: (i, 0, 0)
    b4 = lambda i: (layer, i, 0, 0)
    return pl.pallas_call(
        _cross_sample_body,
        grid=(DEC_BATCH // bt,),
        in_specs=[pl.BlockSpec((bt, rows, X_HEAD_DIM), b3),
                  pl.BlockSpec((None, bt, N_MEM * X_HEADS, X_HEAD_DIM), b4),
                  pl.BlockSpec((None, bt, N_MEM * X_HEADS, X_HEAD_DIM), b4)],
        out_specs=pl.BlockSpec((bt, rows, X_HEAD_DIM), b3),
        out_shape=jax.ShapeDtypeStruct((DEC_BATCH, rows, X_HEAD_DIM), BF16),
        compiler_params=_params(("parallel",)),
        name="cross_sample",
    )(q_ht, mem_k, mem_v)


def _swiglu_step(x, wg_ref, wu_ref, wd_ref):
    hg = jnp.dot(x, wg_ref[...].astype(BF16), preferred_element_type=F32)
    hu = jnp.dot(x, wu_ref[...].astype(BF16), preferred_element_type=F32)
    h = (hg * _sigmoid(hg) * hu).astype(BF16)
    return jnp.dot(h, wd_ref[...].astype(BF16), preferred_element_type=F32)


def _dense_ffn_body(x_ref, a_ref, wa_ref, g_ref, wg_ref, wu_ref, wd_ref, o_ref, xn_ref):
    @pl.when(pl.program_id(1) == 0)
    def _():
        x = x_ref[...] + jnp.dot(a_ref[...], wa_ref[...], preferred_element_type=F32)
        xn_ref[...] = _rms(x, g_ref[...]).astype(BF16)
        o_ref[...] = x

    o_ref[...] += _swiglu_step(xn_ref[...], wg_ref, wu_ref, wd_ref)


def dense_ffn(x, a, wa, g, wg, wu, wd, tm, tf):
    m, d = x.shape
    ka = a.shape[1]
    dff = wg.shape[1]
    return pl.pallas_call(
        _dense_ffn_body,
        grid=(m // tm, dff // tf),
        in_specs=[pl.BlockSpec((tm, d), lambda i, f: (i, 0), pipeline_mode=pl.Buffered(1)),
                  pl.BlockSpec((tm, ka), lambda i, f: (i, 0)),
                  pl.BlockSpec((ka, d), lambda i, f: (0, 0)),
                  pl.BlockSpec((1, d), lambda i, f: (0, 0)),
                  pl.BlockSpec((d, tf), lambda i, f: (0, f)),
                  pl.BlockSpec((d, tf), lambda i, f: (0, f)),
                  pl.BlockSpec((tf, d), lambda i, f: (f, 0))],
        out_specs=pl.BlockSpec((tm, d), lambda i, f: (i, 0)),
        out_shape=jax.ShapeDtypeStruct((m, d), F32),
        scratch_shapes=[pltpu.VMEM((tm, d), BF16)],
        compiler_params=_params(("parallel", "arbitrary")),
        name="dense_ffn",
    )(x, a, wa, g.reshape(1, d), wg, wu, wd)


def _router_body(x_ref, a_ref, wa_ref, g_ref, wr_ref, x1_ref, idx_ref, gate_ref, cnt_ref,
                 carry_ref):
    @pl.when(pl.program_id(0) == 0)
    def _():
        carry_ref[...] = jnp.zeros(carry_ref.shape, F32)

    x1 = x_ref[...] + jnp.dot(a_ref[...], wa_ref[...], preferred_element_type=F32)
    x1_ref[...] = x1
    xn = _rms(x1, g_ref[...])
    xh, xl = _split_bf16(xn)
    wh, wl = _split_bf16(wr_ref[...])
    logits = (jnp.dot(xh, wh, preferred_element_type=F32)
              + (jnp.dot(xl, wh, preferred_element_type=F32)
                 + jnp.dot(xh, wl, preferred_element_type=F32)))
    tm = logits.shape[0]
    lane = lax.broadcasted_iota(jnp.int32, logits.shape, 1)
    lg = jnp.where(lane < N_EXPERTS, logits, -jnp.inf)
    m1 = jnp.max(lg, axis=-1, keepdims=True)
    i1 = jnp.min(jnp.where(lg == m1, lane, LANES), axis=-1, keepdims=True)
    lg2 = jnp.where(lane == i1, -jnp.inf, lg)
    m2 = jnp.max(lg2, axis=-1, keepdims=True)
    i2 = jnp.min(jnp.where(lg2 == m2, lane, LANES), axis=-1, keepdims=True)
    e = jnp.exp(m2 - m1)
    den = 1.0 + e
    gate_ref[...] = jnp.where(lane == 0, 1.0 / den, jnp.where(lane == 1, e / den, 0.0))

    chosen = (lane == i1) | (lane == i2)
    r = lax.broadcasted_iota(jnp.int32, (tm, tm), 0)
    c = lax.broadcasted_iota(jnp.int32, (tm, tm), 1)
    earlier = jnp.where(c < r, 1.0, 0.0).astype(BF16)
    before = jnp.dot(earlier, jnp.where(chosen, 1.0, 0.0).astype(BF16),
                     preferred_element_type=F32) + carry_ref[...]
    r1 = jnp.sum(jnp.where(lane == i1, before, 0.0), axis=-1, keepdims=True).astype(jnp.int32)
    r2 = jnp.sum(jnp.where(lane == i2, before, 0.0), axis=-1, keepdims=True).astype(jnp.int32)
    idx_ref[...] = jnp.where(lane == 0, i1, jnp.where(lane == 1, i2,
                             jnp.where(lane == 2, r1, jnp.where(lane == 3, r2, 0))))
    carry_ref[...] += jnp.sum(jnp.where(chosen, 1.0, 0.0), axis=0, keepdims=True)
    cnt_ref[...] = carry_ref[...]


def router(x, a, wa, g, w_router, tm=512):
    m, d = x.shape
    ka = a.shape[1]
    wr = jnp.zeros((d, LANES), F32).at[:, :N_EXPERTS].set(w_router)
    return pl.pallas_call(
        _router_body,
        grid=(m // tm,),
        in_specs=[pl.BlockSpec((tm, d), lambda i: (i, 0)),
                  pl.BlockSpec((tm, ka), lambda i: (i, 0)),
                  pl.BlockSpec((ka, d), lambda i: (0, 0)),
                  pl.BlockSpec((1, d), lambda i: (0, 0)),
                  pl.BlockSpec((d, LANES), lambda i: (0, 0))],
        out_specs=[pl.BlockSpec((tm, d), lambda i: (i, 0)),
                   pl.BlockSpec((tm, LANES), lambda i: (i, 0)),
                   pl.BlockSpec((tm, LANES), lambda i: (i, 0)),
                   pl.BlockSpec((1, LANES), lambda i: (0, 0))],
        out_shape=[jax.ShapeDtypeStruct((m, d), F32),
                   jax.ShapeDtypeStruct((m, LANES), jnp.int32),
                   jax.ShapeDtypeStruct((m, LANES), F32),
                   jax.ShapeDtypeStruct((1, LANES), F32)],
        scratch_shapes=[pltpu.VMEM((1, LANES), F32)],
        compiler_params=_params(("arbitrary",)),
        name="router",
    )(x, a, wa, g.reshape(1, d), wr)


def _row_copy(src_hbm, row, dst_ref, r, sem):
    return pltpu.make_async_copy(src_hbm.at[pl.ds(row, 1)], dst_ref.at[pl.ds(r, 1)], sem)


def _gather_norm_body(src_ref, x_hbm, g_ref, o_ref, buf_ref, sem, *, rows, n_steps):
    i = pl.program_id(0)
    slot = lax.rem(i, 2)

    def start_batch(step, into):
        base = step * rows

        def start(k, c):
            r0 = pl.multiple_of(k * SUBLANES, SUBLANES)
            for s in range(SUBLANES):
                _row_copy(x_hbm, src_ref[base + r0 + s], buf_ref.at[into], r0 + s,
                          sem.at[into]).start()
            return c

        lax.fori_loop(0, rows // SUBLANES, start, 0)

    @pl.when(i == 0)
    def _():
        start_batch(0, 0)

    @pl.when(i + 1 < n_steps)
    def _():
        start_batch(i + 1, 1 - slot)

    pltpu.make_async_copy(x_hbm.at[pl.ds(0, rows)], buf_ref.at[slot], sem.at[slot]).wait()
    o_ref[...] = _rms(buf_ref[slot], g_ref[...]).astype(o_ref.dtype)


def gather_norm(x, g, src_rows, rows=1024):
    d = x.shape[1]
    n = src_rows.shape[0]
    grid_spec = pltpu.PrefetchScalarGridSpec(
        num_scalar_prefetch=1,
        grid=(n // rows,),
        in_specs=[pl.BlockSpec(memory_space=pl.ANY),
                  pl.BlockSpec((1, d), lambda i, s: (0, 0))],
        out_specs=pl.BlockSpec((rows, d), lambda i, s: (i, 0)),
        scratch_shapes=[pltpu.VMEM((2, rows, d), F32), pltpu.SemaphoreType.DMA((2,))],
    )
    return pl.pallas_call(
        functools.partial(_gather_norm_body, rows=rows, n_steps=n // rows),
        grid_spec=grid_spec,
        out_shape=jax.ShapeDtypeStruct((n, d), BF16),
        compiler_params=_params(("arbitrary",)),
        name="gather_norm",
    )(src_rows, x, g.reshape(1, d))


def _moe_ffn_body(ve_ref, vs_ref, vc_ref, vlive_ref, x_hbm, wg_ref, wu_ref, wd_ref, y_hbm,
                  xbuf_ref, acc_ref, wgb_ref, wub_ref, wdb_ref, sem_in, sem_out, *, sub, c_max, nf):
    v = pl.program_id(0)
    f = pl.program_id(1)
    count = vc_ref[v]
    first = vs_ref[v]
    live = vlive_ref[v] > 0

    def rows_of(j):
        return pl.ds(pl.multiple_of(j * sub, sub), sub)

    def copy_in(j):
        src = x_hbm.at[pl.ds(pl.multiple_of((first + j) * sub, sub), sub)]
        return pltpu.make_async_copy(src, xbuf_ref.at[rows_of(j)], sem_in.at[j])

    def copy_out(j):
        dst = y_hbm.at[pl.ds(pl.multiple_of((first + j) * sub, sub), sub)]
        return pltpu.make_async_copy(acc_ref.at[rows_of(j)], dst, sem_out)

    @pl.when(f == 0)
    def _():
        for j in range(c_max):
            @pl.when((j < count) & live)
            def _(j=j):
                copy_in(j).start()
        for j in range(c_max):
            @pl.when(j < count)
            def _(j=j):
                acc_ref[rows_of(j), :] = jnp.zeros((sub, acc_ref.shape[1]), F32)

    @pl.when(live)
    def _():
        wgb_ref[...] = wg_ref[...].astype(BF16)
        wub_ref[...] = wu_ref[...].astype(BF16)
        wdb_ref[...] = wd_ref[...].astype(BF16)

        def chain(j0, n):
            @pl.when(f == 0)
            def _():
                for k in range(n):
                    copy_in(j0 + k).wait()

            rows = pl.ds(pl.multiple_of(j0 * sub, sub), n * sub)
            acc_ref[rows, :] += _swiglu_step(xbuf_ref[rows, :], wgb_ref, wub_ref, wdb_ref)

            @pl.when(f == nf - 1)
            def _():
                for k in range(n):
                    copy_out(j0 + k).start()

        quads = count // 4
        rest = count - 4 * quads
        fold = (rest == 1) & (quads >= 1)
        n_full = jnp.where(fold, quads - 1, quads)

        def quad(i, carry):
            chain(i * 4, 4)
            return carry

        lax.fori_loop(0, n_full, quad, 0)

        @pl.when(fold)
        def _():
            chain(n_full * 4, 5)

        @pl.when(rest >= 2)
        def _():
            chain(quads * 4, 2)

        @pl.when((rest % 2 == 1) & jnp.logical_not(fold))
        def _():
            chain(count - 1, 1)

    @pl.when(f == nf - 1)
    def _():
        for j in range(c_max):
            @pl.when((j < count) & jnp.logical_not(live))
            def _(j=j):
                copy_out(j).start()
        for j in range(c_max):
            @pl.when(j < count)
            def _(j=j):
                copy_out(j).wait()


def moe_ffn(xs, vis_expert, vis_start, vis_count, vis_live, wg, wu, wd, sub, c_max, tf):
    n, d = xs.shape
    dff = wg.shape[2]
    nf = dff // tf
    n_vis = vis_expert.shape[0]
    fi = lambda v, f, live: jnp.where(live[v] > 0, f, nf - 1)
    grid_spec = pltpu.PrefetchScalarGridSpec(
        num_scalar_prefetch=4,
        grid=(n_vis, nf),
        in_specs=[pl.BlockSpec(memory_space=pl.ANY),
                  pl.BlockSpec((None, d, tf), lambda v, f, ve, vs, vc, lv: (ve[v], 0, fi(v, f, lv))),
                  pl.BlockSpec((None, d, tf), lambda v, f, ve, vs, vc, lv: (ve[v], 0, fi(v, f, lv))),
                  pl.BlockSpec((None, tf, d), lambda v, f, ve, vs, vc, lv: (ve[v], fi(v, f, lv), 0))],
        out_specs=pl.BlockSpec(memory_space=pl.ANY),
        scratch_shapes=[pltpu.VMEM((c_max * sub, d), BF16), pltpu.VMEM((c_max * sub, d), F32),
                        pltpu.VMEM((d, tf), BF16), pltpu.VMEM((d, tf), BF16),
                        pltpu.VMEM((tf, d), BF16),
                        pltpu.SemaphoreType.DMA((c_max,)), pltpu.SemaphoreType.DMA(())],
    )
    return pl.pallas_call(
        functools.partial(_moe_ffn_body, sub=sub, c_max=c_max, nf=nf),
        grid_spec=grid_spec,
        out_shape=jax.ShapeDtypeStruct((n, d), F32),
        compiler_params=_params(("arbitrary", "arbitrary")),
        name="moe_ffn",
    )(vis_expert, vis_start, vis_count, vis_live, xs, wg, wu, wd)


def _combine_rows(pos_ref, x_ref, gate_ref, y_hbm, buf_ref, sem, rows, n_tok):
    i = pl.program_id(0)
    phase = lax.rem(i, 2)

    def start_batch(step, into):
        base = step * rows

        def start(k, c):
            r0 = pl.multiple_of(k * SUBLANES, SUBLANES)
            for s in range(SUBLANES):
                for slot in range(2):
                    _row_copy(y_hbm, pos_ref[slot * n_tok + base + r0 + s],
                              buf_ref.at[into, slot], r0 + s, sem.at[into]).start()
            return c

        lax.fori_loop(0, rows // SUBLANES, start, 0)

    @pl.when(i == 0)
    def _():
        start_batch(0, 0)

    @pl.when(i + 1 < n_tok // rows)
    def _():
        start_batch(i + 1, 1 - phase)

    for slot in range(2):
        pltpu.make_async_copy(y_hbm.at[pl.ds(0, rows)], buf_ref.at[phase, slot],
                              sem.at[phase]).wait()
    gates = gate_ref[...]
    return x_ref[...] + (gates[:, 0:1] * buf_ref[phase, 0] + gates[:, 1:2] * buf_ref[phase, 1])


def _combine_body(pos_ref, x_ref, gate_ref, y_hbm, o_ref, buf_ref, sem, *, rows, n_tok):
    o_ref[...] = _combine_rows(pos_ref, x_ref, gate_ref, y_hbm, buf_ref, sem, rows, n_tok)


def _combine_norm_body(pos_ref, x_ref, gate_ref, g_ref, y_hbm, op_ref, os_ref, buf_ref, sem, *,
                       rows, n_tok, n_p):
    out = _rms(_combine_rows(pos_ref, x_ref, gate_ref, y_hbm, buf_ref, sem, rows, n_tok),
               g_ref[...])

    @pl.when(pl.program_id(0) < n_p)
    def _():
        op_ref[...] = out

    @pl.when(pl.program_id(0) >= n_p)
    def _():
        os_ref[...] = out


def moe_combine(x, gates, y, pos, out_norm=None, rows=512):
    n_tok, d = x.shape
    in_specs = [pl.BlockSpec((rows, d), lambda i, p: (i, 0)),
                pl.BlockSpec((rows, LANES), lambda i, p: (i, 0))]
    scratch = [pltpu.VMEM((2, 2, rows, d), F32), pltpu.SemaphoreType.DMA((2,))]
    any_spec = pl.BlockSpec(memory_space=pl.ANY)
    if out_norm is None:
        return pl.pallas_call(
            functools.partial(_combine_body, rows=rows, n_tok=n_tok),
            grid_spec=pltpu.PrefetchScalarGridSpec(
                num_scalar_prefetch=1, grid=(n_tok // rows,),
                in_specs=in_specs + [any_spec],
                out_specs=pl.BlockSpec((rows, d), lambda i, p: (i, 0)),
                scratch_shapes=scratch),
            out_shape=jax.ShapeDtypeStruct((n_tok, d), F32),
            compiler_params=_params(("arbitrary",)),
            name="moe_combine",
        )(pos, x, gates, y)
    n_p = T_P // rows
    return pl.pallas_call(
        functools.partial(_combine_norm_body, rows=rows, n_tok=n_tok, n_p=n_p),
        grid_spec=pltpu.PrefetchScalarGridSpec(
            num_scalar_prefetch=1, grid=(n_tok // rows,),
            in_specs=in_specs + [pl.BlockSpec((1, d), lambda i, p: (0, 0)), any_spec],
            out_specs=[pl.BlockSpec((rows, d), lambda i, p: (jnp.minimum(i, n_p - 1), 0)),
                       pl.BlockSpec((rows, d), lambda i, p: (jnp.maximum(i - n_p, 0), 0))],
            scratch_shapes=scratch),
        out_shape=[jax.ShapeDtypeStruct((T_P, d), F32), jax.ShapeDtypeStruct((n_tok - T_P, d), F32)],
        compiler_params=_params(("arbitrary",)),
        name="moe_combine_norm",
    )(pos, x, gates, out_norm.reshape(1, d), y)


def moe_layer(x, a, wa, g, w_router, wg, wu, wd, out_norm=None, sub=256, c_max=10, tf=256):
    assert c_max >= N_EXPERTS
    n_tok = x.shape[0]
    i32 = jnp.int32
    x1, idx_full, gate_full, cnt = router(x, a, wa, g, w_router)
    experts = idx_full[:, 0:2]
    ranks = idx_full[:, 2:4]
    counts = cnt[0, :N_EXPERTS].astype(i32)
    n_assign = 2 * n_tok
    n_sb = n_assign // sub + N_EXPERTS
    n_vis = (n_sb + (c_max - 1) * N_EXPERTS) // c_max + 1

    e_ids = jnp.arange(N_EXPERTS, dtype=i32)
    sb_per = (counts + sub - 1) // sub
    sb_end = jnp.cumsum(sb_per)
    sb_start = sb_end - sb_per
    sb_used = sb_end[-1]
    row_start = sb_start * sub

    start_of = jnp.sum(jnp.where(experts[:, :, None] == e_ids, row_start, 0), axis=-1)
    pos = (start_of + ranks).astype(i32).T.reshape(-1)

    n_pad = n_sb * sub - n_assign
    pad_end = jnp.cumsum(sb_per * sub - counts)
    pad_ids = jnp.arange(n_pad, dtype=i32)
    pad_e = jnp.sum((pad_ids[:, None] >= pad_end[None, :]).astype(i32), axis=1)
    keys = jnp.concatenate([experts.reshape(-1) * 2, pad_e * 2 + 1])
    toks = jnp.concatenate([jnp.arange(n_assign, dtype=i32) // 2, (pad_ids * 8) % n_tok])
    _, src_rows = lax.sort((keys, toks), num_keys=1, is_stable=True)

    vis_per = (sb_per + c_max - 1) // c_max
    vis_end = jnp.cumsum(vis_per)
    vis_used = vis_end[-1]
    v_ids = jnp.arange(n_vis, dtype=i32)
    live = v_ids < vis_used
    v_eff = jnp.minimum(v_ids, vis_used - 1)
    vis_expert = jnp.sum((v_eff[:, None] >= vis_end[None, :]).astype(i32), axis=1)
    of_expert = vis_expert[:, None] == e_ids[None, :]
    pick = lambda table: jnp.sum(jnp.where(of_expert, table[None, :], 0), axis=1)
    nth = v_eff - (pick(vis_end) - pick(vis_per))
    vis_start = pick(sb_start) + c_max * nth
    vis_count = jnp.minimum(c_max, pick(sb_per) - c_max * nth)
    tail = v_ids == vis_used
    vis_start = jnp.where(live, vis_start, jnp.where(tail, sb_used, 0)).astype(i32)
    vis_count = jnp.where(live, vis_count, jnp.where(tail, n_sb - sb_used, 0)).astype(i32)

    xs = gather_norm(x1, g, src_rows.astype(i32))
    y = moe_ffn(xs, vis_expert.astype(i32), vis_start, vis_count, live.astype(i32), wg, wu, wd,
                sub, c_max, tf)
    return moe_combine(x1, gate_full, y, pos, out_norm)


def _final_norm_body(x_ref, g_ref, o_ref):
    o_ref[...] = _rms(x_ref[...], g_ref[...])


def final_norm(x, g, first_block, n_blocks, tm=512):
    d = x.shape[1]
    return pl.pallas_call(
        _final_norm_body,
        grid=(n_blocks,),
        in_specs=[pl.BlockSpec((tm, d), lambda i: (first_block + i, 0)),
                  pl.BlockSpec((1, d), lambda i: (0, 0))],
        out_specs=pl.BlockSpec((tm, d), lambda i: (i, 0)),
        out_shape=jax.ShapeDtypeStruct((n_blocks * tm, d), F32),
        compiler_params=_params(("parallel",)),
        name="final_norm",
    )(x, g.reshape(1, d))


def kernel(x_prompt, x_sample, cache_win_k, cache_win_v, state_conv, cache_mem_k, cache_mem_v,
           mem_prompt, norm_mix, w_in, attn_sinks, att_out_norm, conv_w, conv_b, conv_ln_g,
           conv_ln_b, conv_out_norm, w_out, norm_cross, norm_mem, w_xq, w_xk, w_xv, w_xo,
           norm_ffn, w_gate, w_up, w_down, w_router, we_gate, we_up, we_down, final_norm_g):
    x = (x_prompt.reshape(T_P, D_MODEL), x_sample.transpose(1, 0, 2).reshape(T_S, D_MODEL))
    mem = mem_prompt.reshape(N_MEM, D_MODEL)
    o1 = ATT_W
    o2 = o1 + KV_W
    o3 = o2 + KV_W
    o4 = o3 + CONV_C
    wb = cache_win_k.shape[2]
    mem_k_rows = cache_mem_k.reshape(DEPTH, DEC_BATCH, N_MEM * X_HEADS, X_HEAD_DIM)
    mem_v_rows = cache_mem_v.reshape(DEPTH, DEC_BATCH, N_MEM * X_HEADS, X_HEAD_DIM)
    cache_kt = cache_win_k.transpose(0, 1, 3, 4, 2).reshape(DEPTH, DEC_BATCH, KV_W, wb)
    cache_vt = cache_win_v.transpose(0, 1, 3, 4, 2).reshape(DEPTH, DEC_BATCH, KV_W, wb)
    state_rows = state_conv.transpose(0, 2, 1, 3)
    win_out = state_out = None
    pk, pv, pc, pmk, pmv = [], [], [], [], []
    for l in range(DEPTH):
        wl = w_in[l]
        w_u = jnp.concatenate([wl[:, :o1], wl[:, o3:o4], wl[:, o4:], wl[:, o1:o2], wl[:, o2:o3]],
                              axis=1).astype(BF16)
        u = rms_matmul(x, norm_mix[l], w_u, tm=512, tn=U_W)
        ka, kg, kk, kv = ATT_W, ATT_W + CONV_C, ATT_W + 2 * CONV_C, ATT_W + 2 * CONV_C + KV_W

        att_p = swa_prompt(u, attn_sinks[l], att_out_norm[l])
        conv_p, state_p = conv_prompt(u, conv_w[l], conv_b[l], conv_ln_g[l], conv_ln_b[l],
                                      conv_out_norm[l])
        us = u[T_P:].reshape(DEC_SEQ, DEC_BATCH, U_W).transpose(1, 0, 2)
        att_s, win_kt, win_vt = swa_sample(us[:, :, :ATT_W], us[:, :, kk:kv], us[:, :, kv:],
                                           cache_kt, cache_vt, l, win_out,
                                           attn_sinks[l], att_out_norm[l], bt=16)
        win_out = (win_kt, win_vt)
        att_s = att_s.transpose(1, 0, 2).reshape(T_S, ATT_W)
        conv_s, state_out = conv_sample(state_rows, u, l, state_out, conv_w[l], conv_b[l],
                                        conv_ln_g[l], conv_ln_b[l], conv_out_norm[l], bt=16)
        conv_s = conv_s.reshape(T_S, CONV_C)
        x, qx = matmul_add(x, [(att_p, att_s), (conv_p, conv_s)],
                           w_out[l].astype(BF16), tm=512, tn=D_MODEL,
                           then=(norm_cross[l], w_xq[l].astype(BF16)))

        keep = min(WINDOW, SEQ)
        pk.append(u[T_P - keep:T_P, kk:kv].reshape(1, keep, N_KV, HEAD_DIM))
        pv.append(u[T_P - keep:T_P, kv:].reshape(1, keep, N_KV, HEAD_DIM))
        pc.append(state_p.reshape(1, STATE_ROWS, CONV_C))

        mk = rms_matmul(mem, norm_mem[l], w_xk[l].astype(BF16), tm=N_MEM, tn=X_W)
        mv = rms_matmul(mem, norm_mem[l], w_xv[l].astype(BF16), tm=N_MEM, tn=X_W)
        pmk.append(mk.reshape(1, N_MEM, X_HEADS, X_HEAD_DIM))
        pmv.append(mv.reshape(1, N_MEM, X_HEADS, X_HEAD_DIM))
        o_p = cross_prompt(qx, mk, mv)
        q_ht = qx[T_P:].reshape(DEC_SEQ, DEC_BATCH, X_HEADS, X_HEAD_DIM).transpose(1, 2, 0, 3)
        o_s = cross_sample(q_ht.reshape(DEC_BATCH, X_HEADS * DEC_SEQ, X_HEAD_DIM), mem_k_rows,
                           mem_v_rows, l)
        o_s = o_s.reshape(DEC_BATCH, X_HEADS, DEC_SEQ, X_HEAD_DIM).transpose(2, 0, 1, 3)
        o_all = jnp.concatenate([o_p, o_s.reshape(T_S, X_W)], axis=0)
        wxo = w_xo[l].astype(BF16)

        if l % 2 == 0:
            d = l // 2
            x = dense_ffn(x, o_all, wxo, norm_ffn[l], w_gate[d], w_up[d], w_down[d], tm=1088, tf=256)
        else:
            m = l // 2
            last = l == DEPTH - 1
            x = moe_layer(x, o_all, wxo, norm_ffn[l], w_router[m], we_gate[m], we_up[m], we_down[m],
                          out_norm=final_norm_g if last else None)

    if isinstance(x, tuple):
        y_prompt, y_sample = x
    else:
        y_prompt = final_norm(x, final_norm_g, 0, T_P // 512)
        y_sample = final_norm(x, final_norm_g, T_P // 512, T_S // 512)
    y_prompt = y_prompt.reshape(1, SEQ, D_MODEL)
    y_sample = y_sample.reshape(DEC_SEQ, DEC_BATCH, D_MODEL).transpose(1, 0, 2)
    win_k_s, win_v_s = (w.reshape(DEPTH, DEC_BATCH, N_KV, HEAD_DIM, wb).transpose(0, 1, 4, 2, 3)
                        for w in win_out)
    conv_s_state = state_out.transpose(0, 2, 1, 3)
    return (y_prompt, y_sample, jnp.stack(pk), jnp.stack(pv), jnp.stack(pc), jnp.stack(pmk),
            jnp.stack(pmv), win_k_s, win_v_s, conv_s_state)
```
